```python
import jax
import jax.numpy as jnp
from jax import lax
import numpy as np

D_MODEL = 4096
BATCH = 4
SEQ = 2048
DEPTH = 2
DEC_BATCH = 8
DEC_SEQ = 1
PAST_LEN = 16384
PAGE_SIZE = 128

N_HEADS = 16
KV_HEADS = 4
HEAD_DIM = 128
ROT_DIM = HEAD_DIM // 4
ROPE_THETA = 500000.0
IDX_HEADS = 32
IDX_DIM = 128
IDX_ROT_DIM = IDX_DIM // 4
TOPK_MAX = 256
Q_BLOCK = 128
A_Q = N_HEADS * HEAD_DIM
A_KV = KV_HEADS * HEAD_DIM
CHUNK = 128
GMLP_WIDTH = D_MODEL // 2
GMLP_GROUPS = 8
RWKV_HEAD = 64
RWKV_WIDTH = D_MODEL // 2
RWKV_HEADS = RWKV_WIDTH // RWKV_HEAD
D_DECAY = max(32, int(round(D_MODEL ** 0.5 * 1.8 / 32)) * 32)
D_AAA = max(32, int(round(D_MODEL ** 0.5 * 1.8 / 32)) * 32)
D_FF = ((8 * D_MODEL // 3 + 255) // 256) * 256
A_COLS = A_Q + 2 * A_KV + IDX_HEADS * IDX_DIM + IDX_DIM + IDX_HEADS
B_COLS = 2 * GMLP_WIDTH
C_COLS = 3 * RWKV_WIDTH + D_DECAY + D_AAA
G_COLS = 3 * D_MODEL
N_IN = A_COLS + B_COLS + C_COLS + G_COLS
ALPHA = (2 * DEPTH) ** 0.25
BETA = (8 * DEPTH) ** -0.25
LN_EPS = 1e-5
GN_EPS = 64e-5

kernel_name = 'hybrid_dsa_gmlp_rwkv7_step'


def _split(h, sizes):
    offs = []
    acc = 0
    for s in sizes[:-1]:
        acc += s
        offs.append(acc)
    return jnp.split(h, offs, axis=-1)


def _layernorm(x, g, b, eps):
    xf = x.astype(jnp.float32)
    mu = jnp.mean(xf, -1, keepdims=True)
    var = jnp.mean(jnp.square(xf - mu), -1, keepdims=True)
    out = (xf - mu) * lax.rsqrt(var + eps) * g.astype(jnp.float32) + b.astype(jnp.float32)
    return out.astype(x.dtype)


def _rope(x, pos, rot):
    half = rot // 2
    inv = ROPE_THETA ** (-2.0 * jnp.arange(half, dtype=jnp.float32) / rot)
    ang = pos.astype(jnp.float32)[:, None] * inv[None, :]
    cos = jnp.cos(ang)[:, None, :]
    sin = jnp.sin(ang)[:, None, :]
    xf = x.astype(jnp.float32)
    x1, x2 = xf[..., :half], xf[..., half:rot]
    out = jnp.concatenate([x1 * cos - x2 * sin, x2 * cos + x1 * sin, xf[..., rot:]], axis=-1)
    return out.astype(x.dtype)


def _take(arr, idx):
    return jax.vmap(lambda a, i: a[i])(arr, idx)


def _sparse_attend(q, qi, wi, ki_all, q_pos, n_sel, gather_kv):
    Bn, Q = q.shape[0], q.shape[1]
    L = ki_all.shape[1]
    dots = jnp.einsum('bqhd,bsd->bqhs', qi.astype(jnp.float32), ki_all.astype(jnp.float32))
    score = jnp.einsum('bqhs,bqh->bqs', jax.nn.relu(dots), wi.astype(jnp.float32))
    causal = jnp.arange(L, dtype=jnp.int32)[None, :] <= q_pos[:, None]
    score = jnp.where(causal[None], score, -jnp.inf)
    _, sel = lax.top_k(score, n_sel)
    valid = sel <= q_pos[None, :, None]
    k_sel, v_sel = gather_kv(sel)
    qg = q.astype(jnp.float32).reshape(Bn, Q, KV_HEADS, N_HEADS // KV_HEADS, HEAD_DIM)
    logits = jnp.einsum('bqhgd,bqnhd->bqhgn', qg, k_sel.astype(jnp.float32)) * (HEAD_DIM ** -0.5)
    logits = jnp.where(valid[:, :, None, None, :], logits, -jnp.inf)
    p = jax.nn.softmax(logits, axis=-1)
    o = jnp.einsum('bqhgn,bqnhd->bqhgd', p, v_sel.astype(jnp.float32))
    return o.reshape(Bn, Q, N_HEADS * HEAD_DIM).astype(q.dtype)


def _attend_prompt(q, k, v, qi, ki, wi):
    Bn, T = q.shape[0], q.shape[1]
    nb = T // Q_BLOCK
    n_sel = min(TOPK_MAX, T // 4)

    def block(args):
        qb, qib, wib, pos = args
        return _sparse_attend(qb, qib, wib, ki, pos, n_sel, lambda sel: (_take(k, sel), _take(v, sel)))

    def to_blocks(t):
        return jnp.swapaxes(t.reshape((Bn, nb, Q_BLOCK) + t.shape[2:]), 0, 1)

    pos = jnp.arange(T, dtype=jnp.int32).reshape(nb, Q_BLOCK)
    out = lax.map(block, (to_blocks(q), to_blocks(qi), to_blocks(wi), pos))
    return jnp.swapaxes(out, 0, 1).reshape(Bn, T, A_Q)


def _make_attend_sample(pool_k, pool_v, pool_ik, page_table):
    def attend(q, k, v, qi, ki, wi):
        Bn, T = q.shape[0], q.shape[1]
        ik_past = pool_ik[page_table].reshape(Bn, PAST_LEN, IDX_DIM)
        ki_all = jnp.concatenate([ik_past, ki.astype(ik_past.dtype)], axis=1)
        n_sel = min(TOPK_MAX, (PAST_LEN + T) // 4)
        q_pos = PAST_LEN + jnp.arange(T, dtype=jnp.int32)

        def gather(sel):
            in_past = (sel < PAST_LEN)[..., None, None]
            ps = jnp.minimum(sel, PAST_LEN - 1)
            phys = _take(page_table, ps // PAGE_SIZE)
            off = ps % PAGE_SIZE
            kp = pool_k[phys, off]
            vp = pool_v[phys, off]
            ns = jnp.clip(sel - PAST_LEN, 0, T - 1)
            kn = _take(k, ns).astype(kp.dtype)
            vn = _take(v, ns).astype(vp.dtype)
            return jnp.where(in_past, kp, kn), jnp.where(in_past, vp, vn)

        return _sparse_attend(q, qi, wi, ki_all, q_pos, n_sel, gather)
    return attend


def _gmlp_branch(u, v, ln_g, ln_b, w_s, b_s):
    Bn, T, _ = v.shape
    vn = _layernorm(v, ln_g, ln_b, LN_EPS)
    pad = (-T) % CHUNK
    vp = jnp.pad(vn, ((0, 0), (0, pad), (0, 0)))
    nc = (T + pad) // CHUNK
    vr = vp.reshape(Bn, nc, CHUNK, GMLP_GROUPS, GMLP_WIDTH // GMLP_GROUPS)
    causal = jnp.tril(jnp.ones((CHUNK, CHUNK), dtype=bool))
    wm = jnp.where(causal[None], w_s, 0)
    mixed = jnp.einsum('gij,bcjgd->bcigd', wm, vr) + jnp.transpose(b_s)[:, :, None]
    mixed = mixed.reshape(Bn, nc * CHUNK, GMLP_WIDTH)[:, :T]
    return u * mixed, vn


def _rwkv_branch(pc, prev, wkv0, mu, w0, w2, a0, a2, k_k, k_a, r_k, gn_g, gn_b):
    Bn, T, _ = pc.shape
    f32 = jnp.float32
    pcf = pc.astype(f32)
    shifted = jnp.concatenate([prev.astype(f32)[:, None, :], pcf[:, :-1]], axis=1)
    y = pcf + mu.astype(f32) * (shifted - pcf)
    r, k, v, wd, ad = _split(y, (RWKV_WIDTH, RWKV_WIDTH, RWKV_WIDTH, D_DECAY, D_AAA))
    w_log = -jax.nn.softplus(-(w0.astype(f32) + jnp.tanh(wd) @ w2.astype(f32))) - 0.5
    decay = jnp.exp(-jnp.exp(w_log))
    a = jax.nn.sigmoid(a0.astype(f32) + ad @ a2.astype(f32))

    def heads(t):
        return t.reshape(Bn, T, RWKV_HEADS, RWKV_HEAD)

    kk = heads(k * k_k.astype(f32))
    kk = kk / jnp.maximum(jnp.sqrt(jnp.sum(kk * kk, -1, keepdims=True)), 1e-12)
    k = k * (1.0 + (a - 1.0) * k_a.astype(f32))
    rh, kh, vh, wh, ah = heads(r), heads(k), heads(v), heads(decay), heads(a)

    def step(S, inp):
        r_t, w_t, k_t, v_t, kk_t, a_t = inp
        sa = jnp.einsum('bhij,bhj->bhi', S, -kk_t)
        S = S * w_t[:, :, None, :] + sa[..., None] * (kk_t * a_t)[:, :, None, :] + v_t[..., None] * k_t[:, :, None, :]
        o = jnp.einsum('bhij,bhj->bhi', S, r_t)
        return S, o

    xs = (jnp.swapaxes(rh, 0, 1), jnp.swapaxes(wh, 0, 1), jnp.swapaxes(kh, 0, 1),
          jnp.swapaxes(vh, 0, 1), jnp.swapaxes(kk, 0, 1), jnp.swapaxes(ah, 0, 1))
    S_T, o = lax.scan(step, wkv0.astype(f32), xs)
    o = jnp.swapaxes(o, 0, 1)
    m = jnp.mean(o, -1, keepdims=True)
    var = jnp.mean(jnp.square(o - m), -1, keepdims=True)
    on = ((o - m) * lax.rsqrt(var + GN_EPS)).reshape(Bn, T, RWKV_WIDTH) * gn_g.astype(f32) + gn_b.astype(f32)
    bonus = jnp.sum(rh * kh * r_k.astype(f32), -1, keepdims=True) * vh
    out = (on + bonus.reshape(Bn, T, RWKV_WIDTH)).astype(pc.dtype)
    return out, pc[:, -1], S_T.astype(wkv0.dtype)


def _trunk_layer(x, pos, attend, shift_prev, wkv0, lp):
    Bn, T, _ = x.shape
    h = x @ lp['w_in']
    ha, hb, hc, hg = _split(h, (A_COLS, B_COLS, C_COLS, G_COLS))
    q, k, v, qi, ki, wi = _split(ha, (A_Q, A_KV, A_KV, IDX_HEADS * IDX_DIM, IDX_DIM, IDX_HEADS))
    q = _rope(q.reshape(Bn, T, N_HEADS, HEAD_DIM), pos, ROT_DIM)
    k = _rope(k.reshape(Bn, T, KV_HEADS, HEAD_DIM), pos, ROT_DIM)
    v = v.reshape(Bn, T, KV_HEADS, HEAD_DIM)
    qi = _rope(qi.reshape(Bn, T, IDX_HEADS, IDX_DIM), pos, IDX_ROT_DIM)
    ki = _rope(ki[:, :, None, :], pos, IDX_ROT_DIM)[:, :, 0]
    wi = wi * ((IDX_HEADS * IDX_DIM) ** -0.5)
    out_a = attend(q, k, v, qi, ki, wi)
    u, vb = _split(hb, (GMLP_WIDTH, GMLP_WIDTH))
    out_b, v_rows = _gmlp_branch(u, vb, lp['gmlp_ln_g'], lp['gmlp_ln_b'], lp['gmlp_ws'], lp['gmlp_bs'])
    out_c, shift_last, wkv_T = _rwkv_branch(hc, shift_prev, wkv0, lp['rwkv_mu'], lp['rwkv_w0'], lp['rwkv_w2'],
                                            lp['rwkv_a0'], lp['rwkv_a2'], lp['rwkv_kk'], lp['rwkv_ka'],
                                            lp['rwkv_rk'], lp['rwkv_gn_g'], lp['rwkv_gn_b'])
    g = jax.nn.sigmoid(hg.reshape(Bn, T, 3, D_MODEL))
    merged = (g[:, :, 0] * (out_a @ lp['attn_wo']) + g[:, :, 1] * (out_b @ lp['gmlp_wo'])
              + g[:, :, 2] * (out_c @ lp['rwkv_wo']))
    x = _layernorm(ALPHA * x + merged @ lp['w_out'], lp['ln1_g'], lp['ln1_b'], LN_EPS)
    gate, up = _split(x @ lp['ffn_w_in'], (D_FF, D_FF))
    x = _layernorm(ALPHA * x + (jax.nn.silu(gate) * up) @ lp['ffn_w_out'], lp['ln2_g'], lp['ln2_b'], LN_EPS)
    return x, (k, v, ki, wkv_T, shift_last, v_rows)


def setup_inputs(seed: int = 0) -> dict:
    key = jax.random.key(seed)
    ks = iter(jax.random.split(key, 40))
    f32 = jnp.float32

    def nrm(shape, scale):
        return jax.random.normal(next(ks), shape, f32) * scale

    n_pages = PAST_LEN // PAGE_SIZE
    n_phys = (5 * DEC_BATCH * n_pages + 3) // 4
    x_prompt = nrm((BATCH, SEQ, D_MODEL), 1.0)
    x_sample = nrm((DEC_BATCH, DEC_SEQ, D_MODEL), 1.0)
    cache_k = nrm((DEPTH, n_phys, PAGE_SIZE, KV_HEADS, HEAD_DIM), 1.0)
    cache_v = nrm((DEPTH, n_phys, PAGE_SIZE, KV_HEADS, HEAD_DIM), 1.0)
    cache_idx_k = nrm((DEPTH, n_phys, PAGE_SIZE, IDX_DIM), 1.0)
    perm = jax.random.permutation(next(ks), n_phys)
    page_table = perm[:DEC_BATCH * n_pages].reshape(DEC_BATCH, n_pages).astype(jnp.int32)
    state_wkv = nrm((DEPTH, DEC_BATCH, RWKV_HEADS, RWKV_HEAD, RWKV_HEAD), 0.5)
    state_shift = nrm((DEPTH, DEC_BATCH, C_COLS), 1.0)
    return {
        'x_prompt': x_prompt,
        'x_sample': x_sample,
        'cache_k': cache_k,
        'cache_v': cache_v,
        'cache_idx_k': cache_idx_k,
        'page_table': page_table,
        'state_wkv': state_wkv,
        'state_shift': state_shift,
        'w_in': nrm((DEPTH, D_MODEL, N_IN), D_MODEL ** -0.5),
        'attn_wo': nrm((DEPTH, A_Q, D_MODEL), A_Q ** -0.5 * BETA),
        'gmlp_ln_g': 1.0 + nrm((DEPTH, GMLP_WIDTH), 0.02),
        'gmlp_ln_b': nrm((DEPTH, GMLP_WIDTH), 0.02),
        'gmlp_ws': nrm((DEPTH, GMLP_GROUPS, CHUNK, CHUNK), CHUNK ** -0.5),
        'gmlp_bs': 1.0 + nrm((DEPTH, GMLP_GROUPS, CHUNK), 0.02),
        'gmlp_wo': nrm((DEPTH, GMLP_WIDTH, D_MODEL), GMLP_WIDTH ** -0.5 * BETA),
        'rwkv_mu': jax.random.uniform(next(ks), (DEPTH, C_COLS), f32),
        'rwkv_w0': jax.random.uniform(next(ks), (DEPTH, RWKV_WIDTH), f32, minval=-6.0, maxval=1.0),
        'rwkv_w2': nrm((DEPTH, D_DECAY, RWKV_WIDTH), 0.1 * D_DECAY ** -0.5),
        'rwkv_a0': nrm((DEPTH, RWKV_WIDTH), 0.1),
        'rwkv_a2': nrm((DEPTH, D_AAA, RWKV_WIDTH), 0.1 * D_AAA ** -0.5),
        'rwkv_kk': 0.85 + nrm((DEPTH, RWKV_WIDTH), 0.02),
        'rwkv_ka': 1.0 + nrm((DEPTH, RWKV_WIDTH), 0.02),
        'rwkv_rk': nrm((DEPTH, RWKV_HEADS, RWKV_HEAD), 0.1),
        'rwkv_gn_g': 1.0 + nrm((DEPTH, RWKV_WIDTH), 0.02),
        'rwkv_gn_b': nrm((DEPTH, RWKV_WIDTH), 0.02),
        'rwkv_wo': nrm((DEPTH, RWKV_WIDTH, D_MODEL), RWKV_WIDTH ** -0.5 * BETA),
        'w_out': nrm((DEPTH, D_MODEL, D_MODEL), D_MODEL ** -0.5 * BETA),
        'ln1_g': 1.0 + nrm((DEPTH, D_MODEL), 0.02),
        'ln1_b': nrm((DEPTH, D_MODEL), 0.02),
        'ffn_w_in': nrm((DEPTH, D_MODEL, 2 * D_FF), D_MODEL ** -0.5),
        'ffn_w_out': nrm((DEPTH, D_FF, D_MODEL), D_FF ** -0.5 * BETA),
        'ln2_g': 1.0 + nrm((DEPTH, D_MODEL), 0.02),
        'ln2_b': nrm((DEPTH, D_MODEL), 0.02),
    }


def reference(x_prompt, x_sample, cache_k, cache_v, cache_idx_k, page_table, state_wkv, state_shift,
              w_in, attn_wo, gmlp_ln_g, gmlp_ln_b, gmlp_ws, gmlp_bs, gmlp_wo,
              rwkv_mu, rwkv_w0, rwkv_w2, rwkv_a0, rwkv_a2, rwkv_kk, rwkv_ka, rwkv_rk,
              rwkv_gn_g, rwkv_gn_b, rwkv_wo, w_out, ln1_g, ln1_b, ffn_w_in, ffn_w_out, ln2_g, ln2_b):
    Bp, Tp = x_prompt.shape[0], x_prompt.shape[1]
    Ts = x_sample.shape[1]
    pos_p = jnp.arange(Tp, dtype=jnp.int32)
    pos_s = PAST_LEN + jnp.arange(Ts, dtype=jnp.int32)
    shift0_p = jnp.zeros((Bp, C_COLS), x_prompt.dtype)
    wkv0_p = jnp.zeros((Bp, RWKV_HEADS, RWKV_HEAD, RWKV_HEAD), jnp.float32)
    yp, ys = x_prompt, x_sample
    kp_l, vp_l, ikp_l, wkvp_l, shp_l = [], [], [], [], []
    ks_l, vs_l, iks_l, wkvs_l, shs_l, gvs_l = [], [], [], [], [], []
    for l in range(DEPTH):
        lp = dict(w_in=w_in[l], attn_wo=attn_wo[l], gmlp_ln_g=gmlp_ln_g[l], gmlp_ln_b=gmlp_ln_b[l],
                  gmlp_ws=gmlp_ws[l], gmlp_bs=gmlp_bs[l], gmlp_wo=gmlp_wo[l], rwkv_mu=rwkv_mu[l],
                  rwkv_w0=rwkv_w0[l], rwkv_w2=rwkv_w2[l], rwkv_a0=rwkv_a0[l], rwkv_a2=rwkv_a2[l],
                  rwkv_kk=rwkv_kk[l], rwkv_ka=rwkv_ka[l], rwkv_rk=rwkv_rk[l], rwkv_gn_g=rwkv_gn_g[l],
                  rwkv_gn_b=rwkv_gn_b[l], rwkv_wo=rwkv_wo[l], w_out=w_out[l], ln1_g=ln1_g[l], ln1_b=ln1_b[l],
                  ffn_w_in=ffn_w_in[l], ffn_w_out=ffn_w_out[l], ln2_g=ln2_g[l], ln2_b=ln2_b[l])
        yp, (kp, vp, ikp, wkvp, shp, _) = _trunk_layer(yp, pos_p, _attend_prompt, shift0_p, wkv0_p, lp)
        attend_s = _make_attend_sample(cache_k[l], cache_v[l], cache_idx_k[l], page_table)
        ys, (kss, vss, iks, wkvs, shs, gvs) = _trunk_layer(ys, pos_s, attend_s, state_shift[l], state_wkv[l], lp)
        kp_l.append(kp)
        vp_l.append(vp)
        ikp_l.append(ikp)
        wkvp_l.append(wkvp)
        shp_l.append(shp)
        ks_l.append(kss)
        vs_l.append(vss)
        iks_l.append(iks)
        wkvs_l.append(wkvs)
        shs_l.append(shs)
        gvs_l.append(gvs)
    k_prompt = jnp.stack(kp_l)
    v_prompt = jnp.stack(vp_l)
    idx_k_prompt = jnp.stack(ikp_l)
    wkv_prompt = jnp.stack(wkvp_l)
    shift_prompt = jnp.stack(shp_l)
    k_sample = jnp.stack(ks_l)
    v_sample = jnp.stack(vs_l)
    idx_k_sample = jnp.stack(iks_l)
    wkv_sample = jnp.stack(wkvs_l)
    shift_sample = jnp.stack(shs_l)
    gmlp_v_sample = jnp.stack(gvs_l)
    return (yp, ys, k_prompt, v_prompt, idx_k_prompt, wkv_prompt, shift_prompt,
            k_sample, v_sample, idx_k_sample, wkv_sample, shift_sample, gmlp_v_sample)
```

```python
import functools

import jax
import jax.numpy as jnp
from jax import lax
from jax.experimental import pallas as pl
from jax.experimental.pallas import tpu as pltpu

D_MODEL = 4096
PAST_LEN = 16384
PAGE_SIZE = 128
N_HEADS = 16
KV_HEADS = 4
HEAD_DIM = 128
ROT_DIM = HEAD_DIM // 4
ROPE_THETA = 500000.0
IDX_HEADS = 32
IDX_DIM = 128
TOPK_MAX = 256
A_Q = N_HEADS * HEAD_DIM
A_KV = KV_HEADS * HEAD_DIM
CHUNK = 128
GMLP_WIDTH = D_MODEL // 2
GMLP_GROUPS = 8
RWKV_HEAD = 64
RWKV_WIDTH = D_MODEL // 2
RWKV_HEADS = RWKV_WIDTH // RWKV_HEAD
D_DECAY = max(32, int(round(D_MODEL ** 0.5 * 1.8 / 32)) * 32)
D_AAA = D_DECAY
D_FF = ((8 * D_MODEL // 3 + 255) // 256) * 256
A_COLS = A_Q + 2 * A_KV + IDX_HEADS * IDX_DIM + IDX_DIM + IDX_HEADS
B_COLS = 2 * GMLP_WIDTH
C_COLS = 3 * RWKV_WIDTH + D_DECAY + D_AAA
G_COLS = 3 * D_MODEL
DEPTH = 2
ALPHA = (2 * DEPTH) ** 0.25
LN_EPS = 1e-5
GN_EPS = 64e-5

LANES = 128
SAMPLE_ROWS = 16
VMEM_LIMIT = 56 * 1024 * 1024
INT_MIN = -(2 ** 31)
NEG_BIG = -1e30

_F32 = jnp.float32
_BF16 = jnp.bfloat16
_NT = (((1,), (1,)), ((), ()))


def _params(n_grid):
    return pltpu.CompilerParams(dimension_semantics=("arbitrary",) * n_grid,
                                vmem_limit_bytes=VMEM_LIMIT)


def _ws_matmul(name, lhs, pairs, outs, epilogue, *, tm, n_steps, seq, tile_extras=(), row_extras=()):
    mp = lhs[0][0].shape[0]
    n_i = mp // tm
    tpb = seq // tm
    n_batch = mp // seq

    in_specs, args = [], []
    for a, (xp, xs) in enumerate(lhs):
        kb, kidx = [(p["kb"], p["kidx"]) for p in pairs if p["lhs"] == a][0]
        in_specs += [pl.BlockSpec((tm, kb), lambda j, i, kidx=kidx: (i, kidx)),
                     pl.BlockSpec((SAMPLE_ROWS, kb), lambda j, i, kidx=kidx: (0, kidx))]
        args += [xp, xs]
    for p in pairs:
        in_specs.append(pl.BlockSpec((p["kb"], p["ncols"]),
                                     lambda j, i, p=p: (p["kidx"], p["wcol"](j))))
        args.append(p["w"])
    for e in tile_extras:
        in_specs += [pl.BlockSpec((tm, e["ncols"]), lambda j, i, e=e: (i, e["col"](j))),
                     pl.BlockSpec((SAMPLE_ROWS, e["ncols"]), lambda j, i, e=e: (0, e["col"](j)))]
        args += [e["p"], e["s"]]
    for e in row_extras:
        in_specs += [pl.BlockSpec((tm, LANES), lambda j, i: (i % tpb, 0)),
                     pl.BlockSpec((SAMPLE_ROWS, LANES), lambda j, i: (0, 0))]
        args += [e["p"], e["s"]]

    out_specs, out_shapes = [], []
    for o in outs:
        if o.get("hm"):
            hpt = o["ncols"] // LANES
            out_shapes.append(jax.ShapeDtypeStruct((n_batch, o["total"] // LANES, seq, LANES), o["dtype"]))
            out_specs.append(pl.BlockSpec((1, hpt, tm, LANES),
                                          lambda j, i, o=o: (i // tpb, o["col"](j), i % tpb, 0)))
        else:
            out_shapes.append(jax.ShapeDtypeStruct((mp, o["total"]), o["dtype"]))
            out_specs.append(pl.BlockSpec((tm, o["ncols"]), lambda j, i, o=o: (i, o["col"](j))))
        out_shapes.append(jax.ShapeDtypeStruct((SAMPLE_ROWS, o["total"]), o["dtype"]))
        out_specs.append(pl.BlockSpec((SAMPLE_ROWS, o["ncols"]), lambda j, i, o=o: (0, o["col"](j))))

    scratch = [pltpu.VMEM((p["kb"], p["ncols"]), _BF16) for p in pairs]
    n_lhs, n_pairs, n_te, n_re, n_out = len(lhs), len(pairs), len(tile_extras), len(row_extras), len(outs)

    def kernel(*refs):
        pos = 0
        x_refs = [(refs[pos + 2 * a], refs[pos + 2 * a + 1]) for a in range(n_lhs)]
        pos += 2 * n_lhs
        w_refs = refs[pos:pos + n_pairs]
        pos += n_pairs
        te_refs = [(refs[pos + 2 * a], refs[pos + 2 * a + 1]) for a in range(n_te)]
        pos += 2 * n_te
        re_refs = [(refs[pos + 2 * a], refs[pos + 2 * a + 1]) for a in range(n_re)]
        pos += 2 * n_re
        o_refs = [(refs[pos + 2 * a], refs[pos + 2 * a + 1]) for a in range(n_out)]
        pos += 2 * n_out
        wb_refs = refs[pos:pos + n_pairs]
        i = pl.program_id(1)

        def run(which):
            accs = [jnp.dot(x_refs[p["lhs"]][which][...], wb[...], preferred_element_type=_F32)
                    for p, wb in zip(pairs, wb_refs)]
            res = epilogue(accs, [t[which][...] for t in te_refs], [r[which][...] for r in re_refs])
            for o, (op_ref, os_ref), val in zip(outs, o_refs, res):
                if which == 0 and o.get("hm"):
                    for h in range(o["ncols"] // LANES):
                        op_ref[0, h] = val[:, h * LANES:(h + 1) * LANES].astype(o["dtype"])
                else:
                    (op_ref if which == 0 else os_ref)[...] = val.astype(o["dtype"])

        @pl.when(i == 0)
        def _():
            for p, w_ref, wb in zip(pairs, w_refs, wb_refs):
                rows = 256 if p["kb"] % 256 == 0 else 128

                def cast(c, carry, w_ref=w_ref, wb=wb, rows=rows):
                    r0 = pl.multiple_of(c * rows, rows)
                    wb[pl.ds(r0, rows), :] = w_ref[pl.ds(r0, rows), :].astype(_BF16)
                    return carry

                lax.fori_loop(0, p["kb"] // rows, cast, 0)
            run(1)

        run(0)

    res = pl.pallas_call(
        kernel, grid=(n_steps, n_i), in_specs=in_specs, out_specs=out_specs, out_shape=out_shapes,
        scratch_shapes=scratch, compiler_params=_params(2), name=name)(*args)
    return [(res[2 * a], res[2 * a + 1]) for a in range(n_out)]


def _rope_tiles(x, c, s1, s2):
    outs = []
    for h in range(x.shape[1] // LANES):
        xh = x[:, h * LANES:(h + 1) * LANES]
        outs.append(xh * c + pltpu.roll(xh, ROT_DIM // 2, 1) * s1 + pltpu.roll(xh, LANES - ROT_DIM // 2, 1) * s2)
    return outs[0] if len(outs) == 1 else jnp.concatenate(outs, axis=1)


def _rope_tables(pos):
    half = ROT_DIM // 2
    inv = ROPE_THETA ** (-2.0 * jnp.arange(half, dtype=_F32) / ROT_DIM)
    ang = pos.astype(_F32)[:, None] * inv[None, :]
    cos, sin = jnp.cos(ang), jnp.sin(ang)
    n = pos.shape[0]
    c = jnp.concatenate([cos, cos, jnp.ones((n, LANES - ROT_DIM), _F32)], axis=1)
    s1 = jnp.concatenate([jnp.zeros((n, half), _F32), sin, jnp.zeros((n, LANES - ROT_DIM), _F32)], axis=1)
    s2 = jnp.concatenate([-sin, jnp.zeros((n, LANES - half), _F32)], axis=1)
    return c, s1, s2


def _ln_kernel(x_ref, g_ref, b_ref, o_ref, ob_ref):
    x = x_ref[...]
    mu = jnp.mean(x, axis=1, keepdims=True)
    xc = x - mu
    var = jnp.mean(xc * xc, axis=1, keepdims=True)
    y = xc * lax.rsqrt(var + LN_EPS) * g_ref[...] + b_ref[...]
    o_ref[...] = y
    ob_ref[...] = y.astype(_BF16)


def _layernorm(x, g, b, name):
    m, d = x.shape
    tr = 256 if m % 256 == 0 else m
    return pl.pallas_call(
        _ln_kernel, grid=(m // tr,),
        in_specs=[pl.BlockSpec((tr, d), lambda i: (i, 0)), pl.BlockSpec((1, d), lambda i: (0, 0)),
                  pl.BlockSpec((1, d), lambda i: (0, 0))],
        out_specs=[pl.BlockSpec((tr, d), lambda i: (i, 0)), pl.BlockSpec((tr, d), lambda i: (i, 0))],
        out_shape=[jax.ShapeDtypeStruct((m, d), _F32), jax.ShapeDtypeStruct((m, d), _BF16)],
        compiler_params=_params(1), name=name)(x, g.reshape(1, d), b.reshape(1, d))


def _sortable(score):
    bits = pltpu.bitcast(score, jnp.int32)
    return bits ^ ((bits >> 31) & jnp.int32(0x7FFFFFFF))


def _kth_largest(load_keys, shape, k_sel, reduce_axes):
    def body(it, t):
        cand = t + lax.shift_left(jnp.int32(1), jnp.int32(31) - it)
        cnt = jnp.sum(jnp.where(load_keys() >= cand, 1.0, 0.0), axis=reduce_axes, keepdims=True)
        return jnp.where(cnt >= k_sel, cand, t)
    return lax.fori_loop(0, 32, body, jnp.full(shape, INT_MIN, jnp.int32))


def _select_bias(key_ref, valid, col, k_sel, bias_ref, reduce_axes, index_bits, neg):
    shape = tuple(1 if a in reduce_axes else s for a, s in enumerate(key_ref.shape))
    t = _kth_largest(lambda: key_ref[...], shape, float(k_sel), reduce_axes)
    keys = key_ref[...]
    sel0 = (keys >= t) & valid
    n_sel0 = jnp.sum(jnp.where(sel0, 1.0, 0.0), axis=reduce_axes, keepdims=True)
    bias_ref[...] = jnp.where(sel0, 0.0, neg)

    @pl.when(jnp.max(n_sel0) > float(k_sel))
    def _():
        gt = keys > t
        eq = (keys == t) & valid
        need = float(k_sel) - jnp.sum(jnp.where(gt, 1.0, 0.0), axis=reduce_axes, keepdims=True)

        def body(it, m):
            cand = m + lax.shift_left(jnp.int32(1), jnp.int32(index_bits - 1) - it)
            cnt = jnp.sum(jnp.where(eq & (col < cand), 1.0, 0.0), axis=reduce_axes, keepdims=True)
            return jnp.where(cnt < need, cand, m)

        m = lax.fori_loop(0, index_bits, body, jnp.zeros(shape, jnp.int32))
        bias_ref[...] = jnp.where(gt | (eq & (col <= m)), 0.0, neg)


def _attn_prompt_kernel(q_ref, qi_ref, ki_ref, wi_ref, k_ref, v_ref, o_ref,
                        acc_ref, key_ref, bias_ref, wb_ref, oh_ref, *, tq, seq, k_sel):
    q0 = pl.program_id(1) * tq
    grp = N_HEADS // KV_HEADS
    wi = wi_ref[...] * ((IDX_HEADS * IDX_DIM) ** -0.5)
    for h in range(IDX_HEADS):
        wb_ref[h] = jnp.broadcast_to(wi[:, h:h + 1], (tq, LANES))
    acc_ref[...] = jnp.zeros((tq, seq), _F32)
    ki = ki_ref[...]

    def idx_body(c, carry):
        qc = qi_ref[0, pl.ds(c * grp, grp)].reshape(grp * tq, IDX_DIM)
        d = lax.dot_general(qc, ki, _NT, preferred_element_type=_F32)
        part = None
        for hh in range(grp):
            w = jnp.tile(wb_ref[c * grp + hh], (1, seq // LANES))
            term = jnp.maximum(d[hh * tq:(hh + 1) * tq], 0.0) * w
            part = term if part is None else part + term
        acc_ref[...] += part
        return carry

    lax.fori_loop(0, IDX_HEADS // grp, idx_body, 0)

    col = lax.broadcasted_iota(jnp.int32, (tq, seq), 1)
    row = q0 + lax.broadcasted_iota(jnp.int32, (tq, seq), 0)
    causal = col <= row
    key_ref[...] = jnp.where(causal, _sortable(acc_ref[...]), INT_MIN)
    _select_bias(key_ref, causal, col, k_sel, bias_ref, (1,), (seq - 1).bit_length() + 1, -jnp.inf)

    def att_body(g, carry):
        qg = q_ref[0, pl.ds(g * grp, grp)].reshape(grp * tq, HEAD_DIM)
        kg = k_ref[0, g]
        vg = v_ref[0, g]
        logits = lax.dot_general(qg, kg, _NT, preferred_element_type=_F32) * (HEAD_DIM ** -0.5)
        for hh in range(grp):
            lh = logits[hh * tq:(hh + 1) * tq] + bias_ref[...]
            m = jnp.max(lh, axis=1, keepdims=True)
            p = jnp.exp(lh - m)
            s = jnp.sum(p, axis=1, keepdims=True)
            oh_ref[g * grp + hh] = jnp.dot(p.astype(_BF16), vg, preferred_element_type=_F32) / s
        return carry

    lax.fori_loop(0, KV_HEADS, att_body, 0)
    for h in range(N_HEADS):
        o_ref[:, h * HEAD_DIM:(h + 1) * HEAD_DIM] = oh_ref[h].astype(_BF16)


def _attn_prompt(q_hm, qi_hm, ki_b, wi, k_hm, v_hm, n_batch, seq):
    tq = 128
    nqb = seq // tq
    k_sel = min(TOPK_MAX, seq // 4)
    kern = functools.partial(_attn_prompt_kernel, tq=tq, seq=seq, k_sel=k_sel)
    return pl.pallas_call(
        kern, grid=(n_batch, nqb),
        in_specs=[
            pl.BlockSpec((1, N_HEADS, tq, HEAD_DIM), lambda b, q: (b, 0, q, 0)),
            pl.BlockSpec((1, IDX_HEADS, tq, IDX_DIM), lambda b, q: (b, 0, q, 0)),
            pl.BlockSpec((seq, IDX_DIM), lambda b, q: (b, 0)),
            pl.BlockSpec((tq, LANES), lambda b, q: (b * nqb + q, 0)),
            pl.BlockSpec((1, KV_HEADS, seq, HEAD_DIM), lambda b, q: (b, 0, 0, 0)),
            pl.BlockSpec((1, KV_HEADS, seq, HEAD_DIM), lambda b, q: (b, 0, 0, 0)),
        ],
        out_specs=pl.BlockSpec((tq, A_Q), lambda b, q: (b * nqb + q, 0)),
        out_shape=jax.ShapeDtypeStruct((n_batch * seq, A_Q), _BF16),
        scratch_shapes=[pltpu.VMEM((tq, seq), _F32), pltpu.VMEM((tq, seq), jnp.int32),
                        pltpu.VMEM((tq, seq), _F32), pltpu.VMEM((IDX_HEADS, tq, LANES), _F32),
                        pltpu.VMEM((N_HEADS, tq, HEAD_DIM), _F32)],
        compiler_params=_params(2), name="attn_prompt")(q_hm, qi_hm, ki_b, wi, k_hm, v_hm)


PAGES_PER_STEP = 8


def _attn_sample_score_kernel(pt_ref, qi_ref, wi_ref, kin_ref, *rest, n_pages, k_sel):
    pages = rest[:PAGES_PER_STEP]
    bias_ref, score_ref, key_ref = rest[PAGES_PER_STEP:]
    del pt_ref
    s = pl.program_id(1)
    n_rows = n_pages + 8
    qi = qi_ref[0]
    wcol = wi_ref[0] * ((IDX_HEADS * IDX_DIM) ** -0.5)
    for r in range(PAGES_PER_STEP):
        page = pages[r][0].astype(_BF16)
        d = lax.dot_general(qi, page, _NT, preferred_element_type=_F32)
        sc = jnp.sum(jnp.maximum(d, 0.0) * wcol, axis=0, keepdims=True)
        score_ref[pl.ds(s * PAGES_PER_STEP + r, 1), :] = sc

    @pl.when(s == pl.num_programs(1) - 1)
    def _():
        kin = kin_ref[0].astype(_BF16).astype(_F32)
        dn = jnp.sum(qi.astype(_F32) * kin, axis=1, keepdims=True)
        sn = jnp.sum(jnp.maximum(dn, 0.0) * wcol, axis=0, keepdims=True)
        score_ref[pl.ds(n_pages, 8), :] = jnp.broadcast_to(sn, (8, LANES))
        rowi = lax.broadcasted_iota(jnp.int32, (n_rows, LANES), 0)
        lane = lax.broadcasted_iota(jnp.int32, (n_rows, LANES), 1)
        valid = (rowi < n_pages) | ((rowi == n_pages) & (lane == 0))
        pos = rowi * LANES + lane
        key_ref[...] = jnp.where(valid, _sortable(score_ref[...]), INT_MIN)
        _select_bias(key_ref, valid, pos, k_sel, bias_ref.at[0], (0, 1),
                     (n_rows * LANES - 1).bit_length() + 1, NEG_BIG)


def _attn_sample_kernel(pt_ref, q_ref, kn_ref, vn_ref, bias_ref, *rest, n_pages):
    kpages = rest[:PAGES_PER_STEP]
    vpages = rest[PAGES_PER_STEP:2 * PAGES_PER_STEP]
    o_ref, m_ref, s_ref, acc_ref = rest[2 * PAGES_PER_STEP:]
    del pt_ref
    st = pl.program_id(1)
    scale = HEAD_DIM ** -0.5
    q = q_ref[0]

    @pl.when(st == 0)
    def _():
        m_ref[...] = jnp.full(m_ref.shape, NEG_BIG, _F32)
        s_ref[...] = jnp.zeros(s_ref.shape, _F32)
        acc_ref[...] = jnp.zeros(acc_ref.shape, _F32)

    def update(logits, pv_fn):
        m_old = m_ref[...]
        m_new = jnp.maximum(m_old, jnp.max(logits, axis=1, keepdims=True))
        alpha = jnp.exp(m_old - m_new)
        p = jnp.exp(logits - m_new)
        s_ref[...] = s_ref[...] * alpha + jnp.sum(p, axis=1, keepdims=True)
        acc_ref[...] = acc_ref[...] * alpha + pv_fn(p)
        m_ref[...] = m_new

    for r in range(PAGES_PER_STEP):
        kp = kpages[r][0].astype(_BF16)
        vp = vpages[r][0].astype(_BF16)
        logits = lax.dot_general(q, kp, _NT, preferred_element_type=_F32) * scale
        logits = logits + bias_ref[0, pl.ds(st * PAGES_PER_STEP + r, 1), :]
        update(logits, lambda p, vp=vp: jnp.dot(p.astype(_BF16), vp, preferred_element_type=_F32))

    @pl.when(st == pl.num_programs(1) - 1)
    def _():
        kn = kn_ref[0].astype(_BF16).astype(_F32)
        vn = vn_ref[0].astype(_BF16).astype(_F32)
        ln = jnp.sum(q.astype(_F32) * kn, axis=1, keepdims=True) * scale
        ln = ln + bias_ref[0, pl.ds(n_pages, 1), :][:, 0:1]
        update(ln, lambda p: p * vn)
        out = acc_ref[...] / s_ref[...]
        grp = N_HEADS // KV_HEADS
        for h in range(N_HEADS):
            g = h // grp
            o_ref[0, h:h + 1, :] = out[h:h + 1, g * HEAD_DIM:(g + 1) * HEAD_DIM]


def _attn_sample(q_s, qi_s, wi_s, ki_s, k_s, v_s, pool_k, pool_v, pool_ik, page_table):
    n_b, n_pages = page_table.shape
    n_phys = pool_ik.shape[0]
    k_sel = min(TOPK_MAX, (n_pages * PAGE_SIZE + 1) // 4)
    n_steps = n_pages // PAGES_PER_STEP
    n_rows = n_pages + 8
    pt = page_table.reshape(-1)
    grp = N_HEADS // KV_HEADS

    def page_map(r):
        return lambda b, s, pt_ref: (pt_ref[b * n_pages + s * PAGES_PER_STEP + r], 0, 0)

    qi3 = qi_s[:n_b].reshape(n_b, IDX_HEADS, IDX_DIM)
    wi3 = wi_s[:n_b, :IDX_HEADS].reshape(n_b, IDX_HEADS, 1)
    kin3 = ki_s[:n_b].reshape(n_b, 1, IDX_DIM)
    bias = pl.pallas_call(
        functools.partial(_attn_sample_score_kernel, n_pages=n_pages, k_sel=k_sel),
        grid_spec=pltpu.PrefetchScalarGridSpec(
            num_scalar_prefetch=1, grid=(n_b, n_steps),
            in_specs=[pl.BlockSpec((1, IDX_HEADS, IDX_DIM), lambda b, s, pt_ref: (b, 0, 0)),
                      pl.BlockSpec((1, IDX_HEADS, 1), lambda b, s, pt_ref: (b, 0, 0)),
                      pl.BlockSpec((1, 1, IDX_DIM), lambda b, s, pt_ref: (b, 0, 0))]
            + [pl.BlockSpec((1, PAGE_SIZE, IDX_DIM), page_map(r)) for r in range(PAGES_PER_STEP)],
            out_specs=pl.BlockSpec((1, n_rows, LANES), lambda b, s, pt_ref: (b, 0, 0)),
            scratch_shapes=[pltpu.VMEM((n_rows, LANES), _F32), pltpu.VMEM((n_rows, LANES), jnp.int32)]),
        out_shape=jax.ShapeDtypeStruct((n_b, n_rows, LANES), _F32),
        compiler_params=_params(2), name="attn_sample_score")(
            pt, qi3, wi3, kin3, *([pool_ik] * PAGES_PER_STEP))

    q3 = q_s[:n_b].reshape(n_b, N_HEADS, 1, HEAD_DIM)
    head_mask = (jnp.arange(N_HEADS)[:, None] // grp == jnp.arange(KV_HEADS)[None, :])
    qblk = jnp.where(head_mask[None, :, :, None], q3, jnp.zeros((), q3.dtype)).reshape(n_b, N_HEADS, A_KV)
    pk = pool_k.reshape(n_phys, PAGE_SIZE, A_KV)
    pv = pool_v.reshape(n_phys, PAGE_SIZE, A_KV)
    out = pl.pallas_call(
        functools.partial(_attn_sample_kernel, n_pages=n_pages),
        grid_spec=pltpu.PrefetchScalarGridSpec(
            num_scalar_prefetch=1, grid=(n_b, n_steps),
            in_specs=[pl.BlockSpec((1, N_HEADS, A_KV), lambda b, s, pt_ref: (b, 0, 0)),
                      pl.BlockSpec((1, 1, A_KV), lambda b, s, pt_ref: (b, 0, 0)),
                      pl.BlockSpec((1, 1, A_KV), lambda b, s, pt_ref: (b, 0, 0)),
                      pl.BlockSpec((1, n_rows, LANES), lambda b, s, pt_ref: (b, 0, 0))]
            + [pl.BlockSpec((1, PAGE_SIZE, A_KV), page_map(r)) for r in range(PAGES_PER_STEP)]
            + [pl.BlockSpec((1, PAGE_SIZE, A_KV), page_map(r)) for r in range(PAGES_PER_STEP)],
            out_specs=pl.BlockSpec((1, N_HEADS, HEAD_DIM), lambda b, s, pt_ref: (b, 0, 0)),
            scratch_shapes=[pltpu.VMEM((N_HEADS, 1), _F32), pltpu.VMEM((N_HEADS, 1), _F32),
                            pltpu.VMEM((N_HEADS, A_KV), _F32)]),
        out_shape=jax.ShapeDtypeStruct((n_b, N_HEADS, HEAD_DIM), _F32),
        compiler_params=_params(2), name="attn_sample")(
            pt, qblk, k_s[:n_b].reshape(n_b, 1, A_KV), v_s[:n_b].reshape(n_b, 1, A_KV), bias,
            *([pk] * PAGES_PER_STEP), *([pv] * PAGES_PER_STEP))
    return out.reshape(n_b, A_Q)


def _gmlp_kernel(u_ref, v_ref, g_ref, b_ref, ws_ref, bs_ref, o_ref, vn_ref=None):
    v = v_ref[...]
    mu = jnp.mean(v, axis=1, keepdims=True)
    vc = v - mu
    var = jnp.mean(vc * vc, axis=1, keepdims=True)
    vn = vc * lax.rsqrt(var + LN_EPS) * g_ref[...] + b_ref[...]
    if vn_ref is not None:
        vn_ref[...] = vn
    gw = GMLP_WIDTH // GMLP_GROUPS
    tril = (lax.broadcasted_iota(jnp.int32, (CHUNK, CHUNK), 0)
            >= lax.broadcasted_iota(jnp.int32, (CHUNK, CHUNK), 1))
    bs = bs_ref[...]
    for g in range(GMLP_GROUPS):
        wm = jnp.where(tril, ws_ref[g], 0.0).astype(_BF16)
        mixed = jnp.dot(wm, vn[:, g * gw:(g + 1) * gw].astype(_BF16), preferred_element_type=_F32)
        mixed = mixed + bs[:, g:g + 1]
        o_ref[:, g * gw:(g + 1) * gw] = (u_ref[:, g * gw:(g + 1) * gw] * mixed).astype(_BF16)


def _gmlp(hb, ln_g, ln_b, ws, bs, want_vn, name):
    m = hb.shape[0]
    w = GMLP_WIDTH
    out_shape = [jax.ShapeDtypeStruct((m, w), _BF16)]
    out_specs = [pl.BlockSpec((CHUNK, w), lambda c: (c, 0))]
    if want_vn:
        out_shape.append(jax.ShapeDtypeStruct((m, w), _F32))
        out_specs.append(pl.BlockSpec((CHUNK, w), lambda c: (c, 0)))
    return pl.pallas_call(
        _gmlp_kernel, grid=(m // CHUNK,),
        in_specs=[pl.BlockSpec((CHUNK, w), lambda c: (c, 0)), pl.BlockSpec((CHUNK, w), lambda c: (c, 1)),
                  pl.BlockSpec((1, w), lambda c: (0, 0)), pl.BlockSpec((1, w), lambda c: (0, 0)),
                  pl.BlockSpec((GMLP_GROUPS, CHUNK, CHUNK), lambda c: (0, 0, 0)),
                  pl.BlockSpec((CHUNK, GMLP_GROUPS), lambda c: (0, 0))],
        out_specs=out_specs, out_shape=out_shape, compiler_params=_params(1), name=name)(
            hb, hb, ln_g.reshape(1, w), ln_b.reshape(1, w), ws, jnp.transpose(bs))


def _head_sums(x, blk):
    outs = []
    for c in range(x.shape[1] // LANES):
        xc = x[:, c * LANES:(c + 1) * LANES]
        hi = xc.astype(_BF16)
        lo = (xc - hi.astype(_F32)).astype(_BF16)
        outs.append(jnp.dot(hi, blk, preferred_element_type=_F32) + jnp.dot(lo, blk, preferred_element_type=_F32))
    return jnp.concatenate(outs, axis=1)


def _rwkv_pre_kernel(pc_ref, sh_ref, prev_ref, mu_ref, w0_ref, w2_ref, a0_ref, a2_ref, kkp_ref, ka_ref,
                     blk_ref, r_ref, w_ref, k_ref, v_ref, kk_ref, b_ref, *, tr, tiles_per_batch, roll_shift):
    pc = pc_ref[...]
    if roll_shift:
        first = pl.program_id(0) % tiles_per_batch == 0
        before = jnp.where(first, prev_ref[0], sh_ref[7:8, :])
        rolled = pltpu.roll(pc, 1, 0)
        rowi = lax.broadcasted_iota(jnp.int32, pc.shape, 0)
        shifted = jnp.where(rowi == 0, before, rolled)
    else:
        shifted = sh_ref[...]
    y = pc + mu_ref[...] * (shifted - pc)
    wdt = RWKV_WIDTH
    r = y[:, 0:wdt]
    k = y[:, wdt:2 * wdt]
    v = y[:, 2 * wdt:3 * wdt]
    wd = y[:, 3 * wdt:3 * wdt + D_DECAY]
    ad = y[:, 3 * wdt + D_DECAY:3 * wdt + D_DECAY + D_AAA]
    z = -(w0_ref[...] + jnp.dot(jnp.tanh(wd).astype(_BF16), w2_ref[...].astype(_BF16),
                                preferred_element_type=_F32))
    softplus = jnp.maximum(z, 0.0) + jnp.log(1.0 + jnp.exp(-jnp.abs(z)))
    decay = jnp.exp(-jnp.exp(-softplus - 0.5))
    a = jax.nn.sigmoid(a0_ref[...] + jnp.dot(ad.astype(_BF16), a2_ref[...].astype(_BF16),
                                             preferred_element_type=_F32))
    kk = k * kkp_ref[...]
    nrm = jnp.maximum(jnp.sqrt(_head_sums(kk * kk, blk_ref[...])), 1e-12)
    kk = kk / nrm
    r_ref[...] = r
    w_ref[...] = decay
    k_ref[...] = k * (1.0 + (a - 1.0) * ka_ref[...])
    v_ref[...] = v
    kk_ref[...] = kk
    b_ref[...] = kk * a


def _seg_blk():
    seg = jnp.arange(LANES) // RWKV_HEAD
    return (seg[:, None] == seg[None, :]).astype(_BF16)


def _rwkv_pre(hc, shift_src, lp, seq, roll_shift, name):
    m = hc.shape[0]
    tr = 256 if (roll_shift and seq % 256 == 0) else (seq if roll_shift else m)
    tpb = seq // tr if roll_shift else 1
    wdt = RWKV_WIDTH
    if roll_shift:
        n_b = m // seq
        sh_spec = pl.BlockSpec((8, C_COLS), lambda i: (jnp.maximum(i * (tr // 8) - 1, 0), 0))
        prev = shift_src.reshape(n_b, 1, C_COLS)
        prev_spec = pl.BlockSpec((1, 1, C_COLS), lambda i: (i // tpb, 0, 0))
        sh_arg = hc
    else:
        sh_spec = pl.BlockSpec((tr, C_COLS), lambda i: (i, 0))
        prev = jnp.zeros((1, 1, C_COLS), _F32)
        prev_spec = pl.BlockSpec((1, 1, C_COLS), lambda i: (0, 0, 0))
        sh_arg = shift_src

    def vec(n):
        return pl.BlockSpec((1, n), lambda i: (0, 0))

    kern = functools.partial(_rwkv_pre_kernel, tr=tr, tiles_per_batch=tpb, roll_shift=roll_shift)
    return pl.pallas_call(
        kern, grid=(m // tr,),
        in_specs=[pl.BlockSpec((tr, C_COLS), lambda i: (i, 0)), sh_spec, prev_spec, vec(C_COLS), vec(wdt),
                  pl.BlockSpec((D_DECAY, wdt), lambda i: (0, 0)), vec(wdt),
                  pl.BlockSpec((D_AAA, wdt), lambda i: (0, 0)), vec(wdt), vec(wdt),
                  pl.BlockSpec((LANES, LANES), lambda i: (0, 0))],
        out_specs=[pl.BlockSpec((tr, wdt), lambda i: (i, 0))] * 6,
        out_shape=[jax.ShapeDtypeStruct((m, wdt), _F32)] * 6,
        compiler_params=_params(1), name=name)(
            hc, sh_arg, prev, lp["rwkv_mu"].reshape(1, C_COLS), lp["rwkv_w0"].reshape(1, wdt), lp["rwkv_w2"],
            lp["rwkv_a0"].reshape(1, wdt), lp["rwkv_a2"], lp["rwkv_kk"].reshape(1, wdt),
            lp["rwkv_ka"].reshape(1, wdt), _seg_blk())


SCAN_BATCH = LANES // RWKV_HEADS


def _rwkv_scan_kernel(r_ref, w_ref, k_ref, v_ref, kk_ref, b_ref, s0_ref, o_ref, st_ref,
                      zr, zw, zk, zv, zkk, zb, zo, *, tt):
    n = RWKV_HEAD
    step_idx = pl.program_id(0)

    @pl.when(step_idx == 0)
    def _():
        st_ref[...] = s0_ref[...]

    def to_lanes(t, carry):
        for src, dst in ((r_ref, zr), (w_ref, zw), (k_ref, zk), (v_ref, zv), (kk_ref, zkk), (b_ref, zb)):
            x = jnp.concatenate([src[bb, t] for bb in range(SCAN_BATCH)], axis=0)
            dst[t] = x.T
        return carry

    lax.fori_loop(0, tt, to_lanes, 0)

    def step(t, carry):
        acc = [jnp.zeros((n, LANES), _F32), jnp.zeros((n, LANES), _F32)]
        for j in range(n):
            acc[j % 2] = acc[j % 2] + st_ref[j] * zkk[t, pl.ds(j, 1), :]
        sa = -(acc[0] + acc[1])
        vt = zv[t]
        out = [jnp.zeros((n, LANES), _F32), jnp.zeros((n, LANES), _F32)]
        for j in range(n):
            sj = (st_ref[j] * zw[t, pl.ds(j, 1), :] + sa * zb[t, pl.ds(j, 1), :]
                  + vt * zk[t, pl.ds(j, 1), :])
            st_ref[j] = sj
            out[j % 2] = out[j % 2] + sj * zr[t, pl.ds(j, 1), :]
        zo[t] = out[0] + out[1]
        return carry

    lax.fori_loop(0, tt, step, 0)

    def from_lanes(t, carry):
        y = zo[t].T
        for bb in range(SCAN_BATCH):
            o_ref[bb, t] = y[bb * RWKV_HEADS:(bb + 1) * RWKV_HEADS]
        return carry

    lax.fori_loop(0, tt, from_lanes, 0)


def _rwkv_scan(vecs, s0, seq, name):
    n = RWKV_HEAD
    tt = 16 if seq % 16 == 0 else seq
    shape4 = (SCAN_BATCH, seq, RWKV_HEADS, n)
    vspec = pl.BlockSpec((SCAN_BATCH, tt, RWKV_HEADS, n), lambda s: (0, s, 0, 0))
    sspec = pl.BlockSpec((n, n, LANES), lambda s: (0, 0, 0))
    o, st = pl.pallas_call(
        functools.partial(_rwkv_scan_kernel, tt=tt), grid=(seq // tt,),
        in_specs=[vspec] * 6 + [sspec], out_specs=[vspec, sspec],
        out_shape=[jax.ShapeDtypeStruct(shape4, _F32), jax.ShapeDtypeStruct((n, n, LANES), _F32)],
        scratch_shapes=[pltpu.VMEM((tt, n, LANES), _F32)] * 7,
        compiler_params=_params(1), name=name)(*[x.reshape(shape4) for x in vecs], s0)
    return o.reshape(SCAN_BATCH * seq, RWKV_WIDTH), st


def _state_to_lanes(s):
    return jnp.transpose(s, (3, 2, 0, 1)).reshape(RWKV_HEAD, RWKV_HEAD, LANES)


def _state_from_lanes(s):
    return jnp.transpose(s.reshape(RWKV_HEAD, RWKV_HEAD, SCAN_BATCH, RWKV_HEADS), (2, 3, 1, 0))


def _rwkv_post_kernel(o_ref, r_ref, k_ref, v_ref, g_ref, b_ref, rk_ref, blk_ref, out_ref):
    blk = blk_ref[...]
    o = o_ref[...]
    inv = 1.0 / RWKV_HEAD
    m = _head_sums(o, blk) * inv
    oc = o - m
    var = _head_sums(oc * oc, blk) * inv
    on = oc * lax.rsqrt(var + GN_EPS) * g_ref[...] + b_ref[...]
    bonus = _head_sums(r_ref[...] * k_ref[...] * rk_ref[...], blk) * v_ref[...]
    out_ref[...] = (on + bonus).astype(_BF16)


def _rwkv_post(o, r, k, v, lp, name):
    m = o.shape[0]
    tr = 256 if m % 256 == 0 else m
    wdt = RWKV_WIDTH
    row = pl.BlockSpec((tr, wdt), lambda i: (i, 0))
    vec = pl.BlockSpec((1, wdt), lambda i: (0, 0))
    return pl.pallas_call(
        _rwkv_post_kernel, grid=(m // tr,),
        in_specs=[row, row, row, row, vec, vec, vec, pl.BlockSpec((LANES, LANES), lambda i: (0, 0))],
        out_specs=row, out_shape=jax.ShapeDtypeStruct((m, wdt), _BF16),
        compiler_params=_params(1), name=name)(
            o, r, k, v, lp["rwkv_gn_g"].reshape(1, wdt), lp["rwkv_gn_b"].reshape(1, wdt),
            lp["rwkv_rk"].reshape(1, wdt), _seg_blk())


def _pad_rows(x):
    return jnp.zeros((SAMPLE_ROWS,) + x.shape[1:], x.dtype).at[:x.shape[0]].set(x)


def _layer(xp, xpb, xs, xsb, seq, n_dec, lp, cache_k, cache_v, cache_ik, page_table, st_shift, st_wkv):
    mp = xp.shape[0]
    n_batch = mp // seq
    tm = 1024 if seq % 1024 == 0 else seq
    w_in = lp["w_in"]
    d = D_MODEL
    rope_p = _rope_tables(jnp.arange(seq, dtype=jnp.int32))
    rope_s = _rope_tables(jnp.full((SAMPLE_ROWS,), PAST_LEN, jnp.int32))
    rope_extras = [dict(p=a, s=b) for a, b in zip(rope_p, rope_s)]
    lhs_x = [(xpb, xsb)]

    def rope_ep(accs, tex, rex):
        return [_rope_tiles(accs[0], *rex)]

    def pair(w, ncols, wcol, kb=None, kidx=0, lhs=0):
        return dict(lhs=lhs, w=w, kb=w.shape[0] if kb is None else kb, kidx=kidx, ncols=ncols, wcol=wcol)

    tn = 512
    (q_hm, q_s), = _ws_matmul(
        "proj_q", lhs_x, [pair(w_in, tn, lambda j: j)],
        [dict(ncols=tn, dtype=_BF16, col=lambda j: j, total=A_Q, hm=True)],
        rope_ep, tm=tm, n_steps=A_Q // tn, seq=seq, row_extras=rope_extras)
    qi_off = (A_Q + 2 * A_KV) // tn
    (qi_hm, qi_s), = _ws_matmul(
        "proj_qi", lhs_x, [pair(w_in, tn, lambda j: j + qi_off)],
        [dict(ncols=tn, dtype=_BF16, col=lambda j: j, total=IDX_HEADS * IDX_DIM, hm=True)],
        rope_ep, tm=tm, n_steps=IDX_HEADS * IDX_DIM // tn, seq=seq, row_extras=rope_extras)

    ki_col = (A_Q + 2 * A_KV + IDX_HEADS * IDX_DIM) // IDX_DIM

    def k_ep(accs, tex, rex):
        kr = _rope_tiles(accs[0], *rex)
        kir = _rope_tiles(accs[1], *rex)
        return [kr, kr, kir, kir]

    zero = lambda j: 0
    (k_f, k_fs), (k_hm, _), (ki_f, ki_fs), (ki_b, _) = _ws_matmul(
        "proj_k", lhs_x,
        [pair(w_in, A_KV, lambda j: A_Q // A_KV), pair(w_in, IDX_DIM, lambda j: ki_col)],
        [dict(ncols=A_KV, dtype=_F32, col=zero, total=A_KV),
         dict(ncols=A_KV, dtype=_BF16, col=zero, total=A_KV, hm=True),
         dict(ncols=IDX_DIM, dtype=_F32, col=zero, total=IDX_DIM),
         dict(ncols=IDX_DIM, dtype=_BF16, col=zero, total=IDX_DIM)],
        k_ep, tm=tm, n_steps=1, seq=seq, row_extras=rope_extras)

    def v_ep(accs, tex, rex):
        return [accs[0], accs[0], accs[1]]

    (v_f, v_fs), (v_hm, _), (wi_f, wi_fs) = _ws_matmul(
        "proj_v", lhs_x,
        [pair(w_in, A_KV, lambda j: A_Q // A_KV + 1), pair(w_in, LANES, lambda j: ki_col + 1)],
        [dict(ncols=A_KV, dtype=_F32, col=zero, total=A_KV),
         dict(ncols=A_KV, dtype=_BF16, col=zero, total=A_KV, hm=True),
         dict(ncols=LANES, dtype=_F32, col=zero, total=LANES)],
        v_ep, tm=tm, n_steps=1, seq=seq)

    w_b = lax.slice_in_dim(w_in, A_COLS, A_COLS + B_COLS, axis=1)
    w_c = lax.slice_in_dim(w_in, A_COLS + B_COLS, A_COLS + B_COLS + C_COLS, axis=1)
    w_g = lax.slice_in_dim(w_in, A_COLS + B_COLS + C_COLS, A_COLS + B_COLS + C_COLS + G_COLS, axis=1)
    ident = lambda accs, tex, rex: [accs[0]]
    (hb, hb_s), = _ws_matmul(
        "proj_b", lhs_x, [pair(w_b, tn, lambda j: j)],
        [dict(ncols=tn, dtype=_F32, col=lambda j: j, total=B_COLS)],
        ident, tm=tm, n_steps=B_COLS // tn, seq=seq)
    tn_c = 640
    tm_c = 512 if seq % 512 == 0 else seq
    (hc, hc_s), = _ws_matmul(
        "proj_c", lhs_x, [pair(w_c, tn_c, lambda j: j)],
        [dict(ncols=tn_c, dtype=_F32, col=lambda j: j, total=C_COLS)],
        ident, tm=tm_c, n_steps=C_COLS // tn_c, seq=seq)
    (gate, gate_s), = _ws_matmul(
        "proj_g", lhs_x, [pair(w_g, tn, lambda j: j)],
        [dict(ncols=tn, dtype=_BF16, col=lambda j: j, total=G_COLS)],
        lambda accs, tex, rex: [jax.nn.sigmoid(accs[0])], tm=tm, n_steps=G_COLS // tn, seq=seq)

    out_a = _attn_prompt(q_hm, qi_hm, ki_b, wi_f, k_hm, v_hm, n_batch, seq)
    out_a_s = _attn_sample(q_s, qi_s, wi_fs, ki_fs, k_fs, v_fs, cache_k, cache_v, cache_ik, page_table)
    out_a_s = _pad_rows(out_a_s.astype(_BF16))

    out_b, = _gmlp(hb, lp["gmlp_ln_g"], lp["gmlp_ln_b"], lp["gmlp_ws"], lp["gmlp_bs"], False, "gmlp_prompt")
    hb_chunks = jnp.zeros((n_dec, CHUNK, B_COLS), _F32).at[:, 0].set(hb_s[:n_dec]).reshape(n_dec * CHUNK, B_COLS)
    ob_s, vn_s = _gmlp(hb_chunks, lp["gmlp_ln_g"], lp["gmlp_ln_b"], lp["gmlp_ws"], lp["gmlp_bs"], True,
                       "gmlp_sample")
    out_b_s = _pad_rows(ob_s.reshape(n_dec, CHUNK, GMLP_WIDTH)[:, 0])
    vn_rows = vn_s.reshape(n_dec, CHUNK, GMLP_WIDTH)[:, 0]

    pre_p = _rwkv_pre(hc, jnp.zeros((n_batch, C_COLS), _F32), lp, seq, True, "rwkv_pre_prompt")
    o_parts, st_parts = [], []
    for b0 in range(0, n_batch, SCAN_BATCH):
        rows = slice(b0 * seq, (b0 + SCAN_BATCH) * seq)
        o_c, st_c = _rwkv_scan([x[rows] for x in pre_p], jnp.zeros((RWKV_HEAD, RWKV_HEAD, LANES), _F32),
                               seq, "rwkv_scan_prompt")
        o_parts.append(o_c)
        st_parts.append(_state_from_lanes(st_c))
    o_p = o_parts[0] if len(o_parts) == 1 else jnp.concatenate(o_parts, axis=0)
    wkv_p = st_parts[0] if len(st_parts) == 1 else jnp.concatenate(st_parts, axis=0)
    out_c = _rwkv_post(o_p, pre_p[0], pre_p[2], pre_p[3], lp, "rwkv_post_prompt")

    pre_s = _rwkv_pre(hc_s, _pad_rows(st_shift), lp, 1, False, "rwkv_pre_sample")
    o_parts, st_parts = [], []
    for b0 in range(0, n_dec, SCAN_BATCH):
        o_c, st_c = _rwkv_scan([x[b0:b0 + SCAN_BATCH] for x in pre_s],
                               _state_to_lanes(st_wkv[b0:b0 + SCAN_BATCH]), 1, "rwkv_scan_sample")
        o_parts.append(o_c)
        st_parts.append(_state_from_lanes(st_c))
    o_s = _pad_rows(jnp.concatenate(o_parts, axis=0))
    wkv_s = jnp.concatenate(st_parts, axis=0)
    out_c_s = _rwkv_post(o_s, pre_s[0], pre_s[2], pre_s[3], lp, "rwkv_post_sample")

    tm_m = 512 if seq % 512 == 0 else seq
    n_g = d // tn

    def merge_ep(accs, tex, rex):
        return [tex[0].astype(_F32) * accs[0] + tex[1].astype(_F32) * accs[1] + tex[2].astype(_F32) * accs[2]]

    (merged, merged_s), = _ws_matmul(
        "merge", [(out_a, out_a_s), (out_b, out_b_s), (out_c, out_c_s)],
        [pair(lp["attn_wo"], tn, lambda j: j, lhs=0), pair(lp["gmlp_wo"], tn, lambda j: j, lhs=1),
         pair(lp["rwkv_wo"], tn, lambda j: j, lhs=2)],
        [dict(ncols=tn, dtype=_BF16, col=lambda j: j, total=d)],
        merge_ep, tm=tm_m, n_steps=d // tn, seq=seq,
        tile_extras=[dict(p=gate, s=gate_s, ncols=tn, col=lambda j, a=a: j + a * n_g) for a in range(3)])

    def resid_ep(accs, tex, rex):
        return [ALPHA * tex[0] + accs[0]]

    (pre1, pre1_s), = _ws_matmul(
        "out_proj", [(merged, merged_s)], [pair(lp["w_out"], tn, lambda j: j)],
        [dict(ncols=tn, dtype=_F32, col=lambda j: j, total=d)],
        resid_ep, tm=tm, n_steps=d // tn, seq=seq,
        tile_extras=[dict(p=xp, s=xs, ncols=tn, col=lambda j: j)])
    x1, x1b = _layernorm(pre1, lp["ln1_g"], lp["ln1_b"], "ln1_prompt")
    x1_s, x1b_s = _layernorm(pre1_s, lp["ln1_g"], lp["ln1_b"], "ln1_sample")

    tn_f = 256
    n_f = D_FF // tn_f

    def swiglu_ep(accs, tex, rex):
        return [jax.nn.silu(accs[0]) * accs[1]]

    (act, act_s), = _ws_matmul(
        "ffn_in", [(x1b, x1b_s)],
        [pair(lp["ffn_w_in"], tn_f, lambda j: j), pair(lp["ffn_w_in"], tn_f, lambda j: j + n_f)],
        [dict(ncols=tn_f, dtype=_BF16, col=lambda j: j, total=D_FF)],
        swiglu_ep, tm=tm, n_steps=n_f, seq=seq)

    kb = D_FF // 2
    part, part_s, scale = x1, x1_s, ALPHA
    for half in range(2):
        def acc_ep(accs, tex, rex, scale=scale):
            return [scale * tex[0] + accs[0]]

        (part, part_s), = _ws_matmul(
            "ffn_out%d" % half, [(act, act_s)],
            [pair(lp["ffn_w_out"], tn, lambda j: j, kb=kb, kidx=half)],
            [dict(ncols=tn, dtype=_F32, col=lambda j: j, total=d)],
            acc_ep, tm=tm_m, n_steps=d // tn, seq=seq,
            tile_extras=[dict(p=part, s=part_s, ncols=tn, col=lambda j: j)])
        scale = 1.0
    x2, x2b = _layernorm(part, lp["ln2_g"], lp["ln2_b"], "ln2_prompt")
    x2_s, x2b_s = _layernorm(part_s, lp["ln2_g"], lp["ln2_b"], "ln2_sample")

    caches = dict(
        k_p=k_f.reshape(n_batch, seq, KV_HEADS, HEAD_DIM), v_p=v_f.reshape(n_batch, seq, KV_HEADS, HEAD_DIM),
        ik_p=ki_f.reshape(n_batch, seq, IDX_DIM), wkv_p=wkv_p,
        sh_p=hc.reshape(n_batch, seq, C_COLS)[:, -1],
        k_s=k_fs[:n_dec].reshape(n_dec, 1, KV_HEADS, HEAD_DIM), v_s=v_fs[:n_dec].reshape(n_dec, 1, KV_HEADS, HEAD_DIM),
        ik_s=ki_fs[:n_dec].reshape(n_dec, 1, IDX_DIM), wkv_s=wkv_s, sh_s=hc_s[:n_dec],
        gv_s=vn_rows.reshape(n_dec, 1, GMLP_WIDTH))
    return x2, x2b, x2_s, x2b_s, caches


def kernel(x_prompt, x_sample, cache_k, cache_v, cache_idx_k, page_table, state_wkv, state_shift, w_in, attn_wo, gmlp_ln_g, gmlp_ln_b, gmlp_ws, gmlp_bs, gmlp_wo, rwkv_mu, rwkv_w0, rwkv_w2, rwkv_a0, rwkv_a2, rwkv_kk, rwkv_ka, rwkv_rk, rwkv_gn_g, rwkv_gn_b, rwkv_wo, w_out, ln1_g, ln1_b, ffn_w_in, ffn_w_out, ln2_g, ln2_b):
    n_batch, seq, d = x_prompt.shape
    n_dec = x_sample.shape[0]
    depth = w_in.shape[0]
    assert x_sample.shape[1] == 1 and n_dec <= SAMPLE_ROWS and n_dec % SCAN_BATCH == 0
    assert n_batch % SCAN_BATCH == 0 and d == D_MODEL and seq % CHUNK == 0
    xp = x_prompt.reshape(n_batch * seq, d)
    xs = _pad_rows(x_sample.reshape(n_dec, d))
    xpb, xsb = xp.astype(_BF16), xs.astype(_BF16)
    per_layer = []
    for l in range(depth):
        lp = dict(w_in=w_in[l], attn_wo=attn_wo[l], gmlp_ln_g=gmlp_ln_g[l], gmlp_ln_b=gmlp_ln_b[l],
                  gmlp_ws=gmlp_ws[l], gmlp_bs=gmlp_bs[l], gmlp_wo=gmlp_wo[l], rwkv_mu=rwkv_mu[l],
                  rwkv_w0=rwkv_w0[l], rwkv_w2=rwkv_w2[l], rwkv_a0=rwkv_a0[l], rwkv_a2=rwkv_a2[l],
                  rwkv_kk=rwkv_kk[l], rwkv_ka=rwkv_ka[l], rwkv_rk=rwkv_rk[l], rwkv_gn_g=rwkv_gn_g[l],
                  rwkv_gn_b=rwkv_gn_b[l], rwkv_wo=rwkv_wo[l], w_out=w_out[l], ln1_g=ln1_g[l], ln1_b=ln1_b[l],
                  ffn_w_in=ffn_w_in[l], ffn_w_out=ffn_w_out[l], ln2_g=ln2_g[l], ln2_b=ln2_b[l])
        xp, xpb, xs, xsb, c = _layer(xp, xpb, xs, xsb, seq, n_dec, lp, cache_k[l], cache_v[l], cache_idx_k[l],
                                     page_table, state_shift[l], state_wkv[l])
        per_layer.append(c)

    def stack(name):
        return jnp.stack([c[name] for c in per_layer])

    return (xp.reshape(n_batch, seq, d), xs[:n_dec].reshape(n_dec, 1, d),
            stack("k_p"), stack("v_p"), stack("ik_p"), stack("wkv_p"), stack("sh_p"),
            stack("k_s"), stack("v_s"), stack("ik_s"), stack("wkv_s"), stack("sh_s"), stack("gv_s"))
```

```python
import functools

import jax
import jax.numpy as jnp
from jax import lax
from jax.experimental import pallas as pl
from jax.experimental.pallas import tpu as pltpu

D_MODEL = 4096
PAST_LEN = 16384
PAGE_SIZE = 128
N_HEADS = 16
KV_HEADS = 4
HEAD_DIM = 128
ROT_DIM = HEAD_DIM // 4
ROPE_THETA = 500000.0
IDX_HEADS = 32
IDX_DIM = 128
TOPK_MAX = 256
A_Q = N_HEADS * HEAD_DIM
A_KV = KV_HEADS * HEAD_DIM
CHUNK = 128
GMLP_WIDTH = D_MODEL // 2
GMLP_GROUPS = 8
RWKV_HEAD = 64
RWKV_WIDTH = D_MODEL // 2
RWKV_HEADS = RWKV_WIDTH // RWKV_HEAD
D_DECAY = max(32, int(round(D_MODEL ** 0.5 * 1.8 / 32)) * 32)
D_AAA = D_DECAY
D_FF = ((8 * D_MODEL // 3 + 255) // 256) * 256
A_COLS = A_Q + 2 * A_KV + IDX_HEADS * IDX_DIM + IDX_DIM + IDX_HEADS
B_COLS = 2 * GMLP_WIDTH
C_COLS = 3 * RWKV_WIDTH + D_DECAY + D_AAA
G_COLS = 3 * D_MODEL
DEPTH = 2
ALPHA = (2 * DEPTH) ** 0.25
LN_EPS = 1e-5
GN_EPS = 64e-5

LANES = 128
SAMPLE_ROWS = 16
VMEM_LIMIT = 56 * 1024 * 1024
KEY_CHUNK = 512
INT_MIN = -(2 ** 31)
NEG_BIG = -1e30

_F32 = jnp.float32
_BF16 = jnp.bfloat16
_NT = (((1,), (1,)), ((), ()))


def _params(n_grid):
    return pltpu.CompilerParams(dimension_semantics=("arbitrary",) * n_grid,
                                vmem_limit_bytes=VMEM_LIMIT)


def _ws_matmul(name, lhs, pairs, outs, epilogue, *, tm, n_steps, seq, tile_extras=(), row_extras=()):
    mp = lhs[0][0].shape[0]
    n_i = mp // tm
    tpb = seq // tm
    n_batch = mp // seq

    in_specs, args = [], []
    for a, (xp, xs) in enumerate(lhs):
        kb, kidx = [(p["kb"], p["kidx"]) for p in pairs if p["lhs"] == a][0]
        in_specs += [pl.BlockSpec((tm, kb), lambda j, i, kidx=kidx: (i, kidx)),
                     pl.BlockSpec((SAMPLE_ROWS, kb), lambda j, i, kidx=kidx: (0, kidx))]
        args += [xp, xs]
    for p in pairs:
        if p.get("row0") is not None:
            in_specs.append(pl.BlockSpec((pl.Element(p["ncols"]), pl.Element(p["kb"])),
                                         lambda j, i, p=p: (p["row0"](j), 0)))
        else:
            in_specs.append(pl.BlockSpec((None, p["kb"], p["ncols"]),
                                         lambda j, i, p=p: (p["layer"], p["kidx"], p["wcol"](j))))
        args.append(p["w"])
    for e in tile_extras:
        in_specs += [pl.BlockSpec((tm, e["ncols"]), lambda j, i, e=e: (i, e["col"](j))),
                     pl.BlockSpec((SAMPLE_ROWS, e["ncols"]), lambda j, i, e=e: (0, e["col"](j)))]
        args += [e["p"], e["s"]]
    for e in row_extras:
        in_specs += [pl.BlockSpec((tm, LANES), lambda j, i: (i % tpb, 0)),
                     pl.BlockSpec((SAMPLE_ROWS, LANES), lambda j, i: (0, 0))]
        args += [e["p"], e["s"]]

    out_specs, out_shapes = [], []
    for o in outs:
        if o.get("hm"):
            hpt = o["ncols"] // LANES
            out_shapes.append(jax.ShapeDtypeStruct((n_batch, o["total"] // LANES, seq, LANES), o["dtype"]))
            out_specs.append(pl.BlockSpec((1, hpt, tm, LANES),
                                          lambda j, i, o=o: (i // tpb, o["col"](j), i % tpb, 0)))
        else:
            out_shapes.append(jax.ShapeDtypeStruct((mp, o["total"]), o["dtype"]))
            out_specs.append(pl.BlockSpec((tm, o["ncols"]), lambda j, i, o=o: (i, o["col"](j))))
        out_shapes.append(jax.ShapeDtypeStruct((SAMPLE_ROWS, o["total"]), o["dtype"]))
        out_specs.append(pl.BlockSpec((SAMPLE_ROWS, o["ncols"]), lambda j, i, o=o: (0, o["col"](j))))

    scratch = [pltpu.VMEM((p["kb"], p["ncols"]), _BF16) for p in pairs]
    n_lhs, n_pairs, n_te, n_re, n_out = len(lhs), len(pairs), len(tile_extras), len(row_extras), len(outs)

    def kernel(*refs):
        pos = 0
        x_refs = [(refs[pos + 2 * a], refs[pos + 2 * a + 1]) for a in range(n_lhs)]
        pos += 2 * n_lhs
        w_refs = refs[pos:pos + n_pairs]
        pos += n_pairs
        te_refs = [(refs[pos + 2 * a], refs[pos + 2 * a + 1]) for a in range(n_te)]
        pos += 2 * n_te
        re_refs = [(refs[pos + 2 * a], refs[pos + 2 * a + 1]) for a in range(n_re)]
        pos += 2 * n_re
        o_refs = [(refs[pos + 2 * a], refs[pos + 2 * a + 1]) for a in range(n_out)]
        pos += 2 * n_out
        wb_refs = refs[pos:pos + n_pairs]
        i = pl.program_id(1)

        def run(which):
            accs = [jnp.dot(x_refs[p["lhs"]][which][...], wb[...], preferred_element_type=_F32)
                    for p, wb in zip(pairs, wb_refs)]
            res = epilogue(accs, [t[which][...] for t in te_refs], [r[which][...] for r in re_refs])
            for o, (op_ref, os_ref), val in zip(outs, o_refs, res):
                if which == 0 and o.get("hm"):
                    for h in range(o["ncols"] // LANES):
                        op_ref[0, h] = val[:, h * LANES:(h + 1) * LANES].astype(o["dtype"])
                else:
                    (op_ref if which == 0 else os_ref)[...] = val.astype(o["dtype"])

        @pl.when(i == 0)
        def _():
            for p, w_ref, wb in zip(pairs, w_refs, wb_refs):
                if p.get("row0") is not None:
                    for kc in range(p["kb"] // LANES):
                        blk = w_ref[:, kc * LANES:(kc + 1) * LANES]
                        wb[kc * LANES:(kc + 1) * LANES, :] = blk.T.astype(_BF16)
                    continue
                rows = 256 if p["kb"] % 256 == 0 else 128

                def cast(c, carry, w_ref=w_ref, wb=wb, rows=rows):
                    r0 = pl.multiple_of(c * rows, rows)
                    wb[pl.ds(r0, rows), :] = w_ref[pl.ds(r0, rows), :].astype(_BF16)
                    return carry

                lax.fori_loop(0, p["kb"] // rows, cast, 0)
            run(1)

        run(0)

    res = pl.pallas_call(
        kernel, grid=(n_steps, n_i), in_specs=in_specs, out_specs=out_specs, out_shape=out_shapes,
        scratch_shapes=scratch, compiler_params=_params(2), name=name)(*args)
    return [(res[2 * a], res[2 * a + 1]) for a in range(n_out)]


def _rope_tiles(x, c, s1, s2):
    outs = []
    for h in range(x.shape[1] // LANES):
        xh = x[:, h * LANES:(h + 1) * LANES]
        outs.append(xh * c + pltpu.roll(xh, ROT_DIM // 2, 1) * s1 + pltpu.roll(xh, LANES - ROT_DIM // 2, 1) * s2)
    return outs[0] if len(outs) == 1 else jnp.concatenate(outs, axis=1)


def _rope_tables(pos):
    half = ROT_DIM // 2
    inv = ROPE_THETA ** (-2.0 * jnp.arange(half, dtype=_F32) / ROT_DIM)
    ang = pos.astype(_F32)[:, None] * inv[None, :]
    cos, sin = jnp.cos(ang), jnp.sin(ang)
    n = pos.shape[0]
    c = jnp.concatenate([cos, cos, jnp.ones((n, LANES - ROT_DIM), _F32)], axis=1)
    s1 = jnp.concatenate([jnp.zeros((n, half), _F32), sin, jnp.zeros((n, LANES - ROT_DIM), _F32)], axis=1)
    s2 = jnp.concatenate([-sin, jnp.zeros((n, LANES - half), _F32)], axis=1)
    return c, s1, s2


def _ln_kernel(x_ref, g_ref, b_ref, o_ref, ob_ref):
    x = x_ref[...]
    mu = jnp.mean(x, axis=1, keepdims=True)
    xc = x - mu
    var = jnp.mean(xc * xc, axis=1, keepdims=True)
    y = xc * lax.rsqrt(var + LN_EPS) * g_ref[...] + b_ref[...]
    o_ref[...] = y
    ob_ref[...] = y.astype(_BF16)


def _layernorm(x, g, b, name):
    m, d = x.shape
    tr = 256 if m % 256 == 0 else m
    return pl.pallas_call(
        _ln_kernel, grid=(m // tr,),
        in_specs=[pl.BlockSpec((tr, d), lambda i: (i, 0)), pl.BlockSpec((1, d), lambda i: (0, 0)),
                  pl.BlockSpec((1, d), lambda i: (0, 0))],
        out_specs=[pl.BlockSpec((tr, d), lambda i: (i, 0)), pl.BlockSpec((tr, d), lambda i: (i, 0))],
        out_shape=[jax.ShapeDtypeStruct((m, d), _F32), jax.ShapeDtypeStruct((m, d), _BF16)],
        compiler_params=_params(1), name=name)(x, g.reshape(1, d), b.reshape(1, d))


def _sortable(score):
    bits = pltpu.bitcast(score, jnp.int32)
    return bits ^ ((bits >> 31) & jnp.int32(0x7FFFFFFF))


def _topk_bias(n_ch, load_key, meta, store_bias, k_sel, shape, axes, index_bits, neg):
    kf = float(k_sel)

    def count(pred):
        def cb(c, cnt):
            return cnt + jnp.sum(jnp.where(pred(c), 1.0, 0.0), axis=axes, keepdims=True)
        return lax.fori_loop(0, n_ch, cb, jnp.zeros(shape, _F32))

    def bisect(it, t):
        cand = t + lax.shift_left(jnp.int32(1), jnp.int32(31) - it)
        return jnp.where(count(lambda c: load_key(c) >= cand) >= kf, cand, t)

    t = lax.fori_loop(0, 32, bisect, jnp.full(shape, INT_MIN, jnp.int32))
    n_ge = count(lambda c: (load_key(c) >= t) & meta(c)[0])

    def tie_limit():
        need = kf - count(lambda c: load_key(c) > t)

        def body(it, m):
            cand = m + lax.shift_left(jnp.int32(1), jnp.int32(index_bits - 1) - it)
            cnt = count(lambda c: (load_key(c) == t) & meta(c)[0] & (meta(c)[1] < cand))
            return jnp.where(cnt < need, cand, m)

        return lax.fori_loop(0, index_bits, body, jnp.zeros(shape, jnp.int32))

    no_limit = jnp.full(shape, 2 ** 30, jnp.int32)
    m = lax.cond(jnp.max(n_ge) > kf, tie_limit, lambda: no_limit)

    def emit(c, carry):
        keys = load_key(c)
        valid, idx = meta(c)
        store_bias(c, jnp.where((keys > t) | ((keys == t) & valid & (idx <= m)), 0.0, neg))
        return carry

    lax.fori_loop(0, n_ch, emit, 0)


def _attn_prompt_kernel(q_ref, qi_ref, ki_ref, wi_ref, k_ref, v_ref, o_ref,
                        acc_ref, key_ref, bias_ref, wb_ref, l_ref, oacc_ref, sacc_ref, oh_ref,
                        *, tq, kc, k_sel, index_bits):
    q0 = pl.program_id(1) * tq
    n_ch = (q0 + tq + kc - 1) // kc
    grp = N_HEADS // KV_HEADS
    wi = wi_ref[...] * ((IDX_HEADS * IDX_DIM) ** -0.5)
    for h in range(IDX_HEADS):
        wb_ref[h] = jnp.broadcast_to(wi[:, h:h + 1], (tq, LANES))

    def meta(c):
        col = c * kc + lax.broadcasted_iota(jnp.int32, (tq, kc), 1)
        row = q0 + lax.broadcasted_iota(jnp.int32, (tq, kc), 0)
        return col <= row, col

    def score_chunk(c, carry):
        ki = ki_ref[pl.ds(pl.multiple_of(c * kc, kc), kc), :]
        acc_ref[...] = jnp.zeros((tq, kc), _F32)

        def idx_body(hc, carry2):
            qc = qi_ref[0, pl.ds(hc * grp, grp)].reshape(grp * tq, IDX_DIM)
            d = lax.dot_general(qc, ki, _NT, preferred_element_type=_F32)
            part = None
            for hh in range(grp):
                w = jnp.tile(wb_ref[hc * grp + hh], (1, kc // LANES))
                term = jnp.maximum(d[hh * tq:(hh + 1) * tq], 0.0) * w
                part = term if part is None else part + term
            acc_ref[...] += part
            return carry2

        lax.fori_loop(0, IDX_HEADS // grp, idx_body, 0)
        key_ref[c] = jnp.where(meta(c)[0], _sortable(acc_ref[...]), INT_MIN)
        return carry

    lax.fori_loop(0, n_ch, score_chunk, 0)

    def store_bias(c, val):
        bias_ref[c] = val

    _topk_bias(n_ch, lambda c: key_ref[c], meta, store_bias, k_sel, (tq, 1), (1,), index_bits, -jnp.inf)

    def att_body(g, carry):
        qg = q_ref[0, pl.ds(g * grp, grp)].reshape(grp * tq, HEAD_DIM)

        def logits_chunk(c, m):
            kg = k_ref[0, g, pl.ds(pl.multiple_of(c * kc, kc), kc), :]
            lg = lax.dot_general(qg, kg, _NT, preferred_element_type=_F32) * (HEAD_DIM ** -0.5)
            lg = lg + jnp.tile(bias_ref[c], (grp, 1))
            l_ref[c] = lg
            return jnp.maximum(m, jnp.max(lg, axis=1, keepdims=True))

        m = lax.fori_loop(0, n_ch, logits_chunk, jnp.full((grp * tq, 1), -jnp.inf, _F32))
        oacc_ref[...] = jnp.zeros(oacc_ref.shape, _F32)
        sacc_ref[...] = jnp.zeros(sacc_ref.shape, _F32)

        def pv_chunk(c, carry2):
            vg = v_ref[0, g, pl.ds(pl.multiple_of(c * kc, kc), kc), :]
            p = jnp.exp(l_ref[c] - m)
            sacc_ref[...] += jnp.sum(p, axis=1, keepdims=True)
            oacc_ref[...] += jnp.dot(p.astype(_BF16), vg, preferred_element_type=_F32)
            return carry2

        lax.fori_loop(0, n_ch, pv_chunk, 0)
        out = oacc_ref[...] / sacc_ref[...]
        for hh in range(grp):
            oh_ref[g * grp + hh] = out[hh * tq:(hh + 1) * tq]
        return carry

    lax.fori_loop(0, KV_HEADS, att_body, 0)
    for h in range(N_HEADS):
        o_ref[:, h * HEAD_DIM:(h + 1) * HEAD_DIM] = oh_ref[h].astype(_BF16)


def _attn_prompt(q_hm, qi_hm, ki_b, wi, k_hm, v_hm, n_batch, seq):
    tq = 128
    kc = KEY_CHUNK if seq % KEY_CHUNK == 0 else seq
    nqb = seq // tq
    grp = N_HEADS // KV_HEADS
    k_sel = min(TOPK_MAX, seq // 4)
    kern = functools.partial(_attn_prompt_kernel, tq=tq, kc=kc, k_sel=k_sel,
                             index_bits=(seq - 1).bit_length() + 1)
    return pl.pallas_call(
        kern, grid=(n_batch, nqb),
        in_specs=[
            pl.BlockSpec((1, N_HEADS, tq, HEAD_DIM), lambda b, q: (b, 0, q, 0)),
            pl.BlockSpec((1, IDX_HEADS, tq, IDX_DIM), lambda b, q: (b, 0, q, 0)),
            pl.BlockSpec((seq, IDX_DIM), lambda b, q: (b, 0)),
            pl.BlockSpec((tq, LANES), lambda b, q: (b * nqb + q, 0)),
            pl.BlockSpec((1, KV_HEADS, seq, HEAD_DIM), lambda b, q: (b, 0, 0, 0)),
            pl.BlockSpec((1, KV_HEADS, seq, HEAD_DIM), lambda b, q: (b, 0, 0, 0)),
        ],
        out_specs=pl.BlockSpec((tq, A_Q), lambda b, q: (b * nqb + q, 0)),
        out_shape=jax.ShapeDtypeStruct((n_batch * seq, A_Q), _BF16),
        scratch_shapes=[pltpu.VMEM((tq, kc), _F32), pltpu.VMEM((seq // kc, tq, kc), jnp.int32),
                        pltpu.VMEM((seq // kc, tq, kc), _F32), pltpu.VMEM((IDX_HEADS, tq, LANES), _F32),
                        pltpu.VMEM((seq // kc, grp * tq, kc), _F32), pltpu.VMEM((grp * tq, HEAD_DIM), _F32),
                        pltpu.VMEM((grp * tq, 1), _F32), pltpu.VMEM((N_HEADS, tq, HEAD_DIM), _F32)],
        compiler_params=_params(2), name="attn_prompt")(q_hm, qi_hm, ki_b, wi, k_hm, v_hm)


PAGES_PER_STEP = 8


def _attn_sample_score_kernel(pt_ref, qi_ref, wi_ref, kin_ref, exp_ref, *rest, n_pages, k_sel):
    pages = rest[:PAGES_PER_STEP]
    bias_ref, score_ref, key_ref, sel_ref = rest[PAGES_PER_STEP:]
    del pt_ref
    s = pl.program_id(1)
    n_rows = n_pages + 8
    qi = qi_ref[0]
    wcol = wi_ref[0] * ((IDX_HEADS * IDX_DIM) ** -0.5)
    for r in range(PAGES_PER_STEP):
        page = pages[r][0].astype(_BF16)
        d = lax.dot_general(qi, page, _NT, preferred_element_type=_F32)
        sc = jnp.sum(jnp.maximum(d, 0.0) * wcol, axis=0, keepdims=True)
        score_ref[pl.ds(s * PAGES_PER_STEP + r, 1), :] = sc

    @pl.when(s == pl.num_programs(1) - 1)
    def _():
        kin = kin_ref[0].astype(_BF16).astype(_F32)
        dn = jnp.sum(qi.astype(_F32) * kin, axis=1, keepdims=True)
        sn = jnp.sum(jnp.maximum(dn, 0.0) * wcol, axis=0, keepdims=True)
        score_ref[pl.ds(n_pages, 8), :] = jnp.broadcast_to(sn, (8, LANES))
        rowi = lax.broadcasted_iota(jnp.int32, (n_rows, LANES), 0)
        lane = lax.broadcasted_iota(jnp.int32, (n_rows, LANES), 1)
        valid = (rowi < n_pages) | ((rowi == n_pages) & (lane == 0))
        pos = rowi * LANES + lane
        key_ref[...] = jnp.where(valid, _sortable(score_ref[...]), INT_MIN)

        def store_sel(c, val):
            sel_ref[...] = val

        _topk_bias(1, lambda c: key_ref[...], lambda c: (valid, pos), store_sel, k_sel, (1, 1), (0, 1),
                   (n_rows * LANES - 1).bit_length() + 1, 1.0)
        rep = jnp.dot(sel_ref[...].astype(_BF16), exp_ref[...], preferred_element_type=_F32)
        bias_ref[0] = jnp.where(rep < 0.5, 0.0, NEG_BIG)


def _attn_sample_kernel(pt_ref, q_ref, kn_ref, vn_ref, bias_ref, *rest, n_pages):
    kpages = rest[:PAGES_PER_STEP]
    vpages = rest[PAGES_PER_STEP:2 * PAGES_PER_STEP]
    o_ref, m_ref, s_ref, acc_ref = rest[2 * PAGES_PER_STEP:]
    del pt_ref
    st = pl.program_id(1)
    scale = HEAD_DIM ** -0.5
    grp = N_HEADS // KV_HEADS
    n_cols = PAGE_SIZE * KV_HEADS
    q = q_ref[0]
    own = (lax.broadcasted_iota(jnp.int32, (N_HEADS, n_cols), 1) % KV_HEADS
           == lax.broadcasted_iota(jnp.int32, (N_HEADS, n_cols), 0) // grp)
    head_bias = jnp.where(own, 0.0, NEG_BIG)

    @pl.when(st == 0)
    def _():
        m_ref[...] = jnp.full(m_ref.shape, NEG_BIG, _F32)
        s_ref[...] = jnp.zeros(s_ref.shape, _F32)
        acc_ref[...] = jnp.zeros(acc_ref.shape, _F32)

    def update(logits, pv_fn):
        m_old = m_ref[...]
        m_new = jnp.maximum(m_old, jnp.max(logits, axis=1, keepdims=True))
        alpha = jnp.exp(m_old - m_new)
        p = jnp.exp(logits - m_new)
        s_ref[...] = s_ref[...] * alpha + jnp.sum(p, axis=1, keepdims=True)
        acc_ref[...] = acc_ref[...] * alpha + pv_fn(p)
        m_ref[...] = m_new

    for r in range(PAGES_PER_STEP):
        kp = kpages[r][0].astype(_BF16)
        vp = vpages[r][0].astype(_BF16)
        logits = lax.dot_general(q, kp, _NT, preferred_element_type=_F32) * scale
        logits = logits + head_bias + bias_ref[0, pl.ds(st * PAGES_PER_STEP + r, 1), :]
        update(logits, lambda p, vp=vp: jnp.dot(p.astype(_BF16), vp, preferred_element_type=_F32))

    @pl.when(st == pl.num_programs(1) - 1)
    def _():
        kn = kn_ref[0].astype(_BF16).astype(_F32)
        vn = vn_ref[0].astype(_BF16).astype(_F32)
        ln = jnp.sum(q.astype(_F32) * kn, axis=1, keepdims=True) * scale
        ln = ln + bias_ref[0, pl.ds(n_pages, 1), :][:, 0:1]
        update(ln, lambda p: p * vn)
        o_ref[0] = acc_ref[...] / s_ref[...]


def _attn_sample(q_s, qi_s, wi_s, ki_s, k_s, v_s, cache_k, cache_v, cache_ik, page_table, layer):
    n_b, n_pages = page_table.shape
    depth, n_phys = cache_ik.shape[:2]
    k_sel = min(TOPK_MAX, (n_pages * PAGE_SIZE + 1) // 4)
    n_steps = n_pages // PAGES_PER_STEP
    n_rows = n_pages + 8
    n_cols = PAGE_SIZE * KV_HEADS
    pt = page_table.reshape(-1)
    grp = N_HEADS // KV_HEADS

    def page_map(r):
        return lambda b, s, pt_ref: (layer, pt_ref[b * n_pages + s * PAGES_PER_STEP + r], 0, 0)

    qi3 = qi_s[:n_b].reshape(n_b, IDX_HEADS, IDX_DIM)
    wi3 = wi_s[:n_b, :IDX_HEADS].reshape(n_b, IDX_HEADS, 1)
    kin3 = ki_s[:n_b].reshape(n_b, 1, IDX_DIM)
    expand = (jnp.arange(LANES)[:, None] == jnp.arange(n_cols)[None, :] // KV_HEADS).astype(_BF16)
    bias = pl.pallas_call(
        functools.partial(_attn_sample_score_kernel, n_pages=n_pages, k_sel=k_sel),
        grid_spec=pltpu.PrefetchScalarGridSpec(
            num_scalar_prefetch=1, grid=(n_b, n_steps),
            in_specs=[pl.BlockSpec((1, IDX_HEADS, IDX_DIM), lambda b, s, pt_ref: (b, 0, 0)),
                      pl.BlockSpec((1, IDX_HEADS, 1), lambda b, s, pt_ref: (b, 0, 0)),
                      pl.BlockSpec((1, 1, IDX_DIM), lambda b, s, pt_ref: (b, 0, 0)),
                      pl.BlockSpec((LANES, n_cols), lambda b, s, pt_ref: (0, 0))]
            + [pl.BlockSpec((None, 1, PAGE_SIZE, IDX_DIM), page_map(r)) for r in range(PAGES_PER_STEP)],
            out_specs=pl.BlockSpec((1, n_rows, n_cols), lambda b, s, pt_ref: (b, 0, 0)),
            scratch_shapes=[pltpu.VMEM((n_rows, LANES), _F32), pltpu.VMEM((n_rows, LANES), jnp.int32),
                            pltpu.VMEM((n_rows, LANES), _F32)]),
        out_shape=jax.ShapeDtypeStruct((n_b, n_rows, n_cols), _F32),
        compiler_params=_params(2), name="attn_sample_score")(
            pt, qi3, wi3, kin3, expand, *([cache_ik] * PAGES_PER_STEP))

    pk = cache_k.reshape(depth, n_phys, n_cols, HEAD_DIM)
    pv = cache_v.reshape(depth, n_phys, n_cols, HEAD_DIM)
    q3 = q_s[:n_b].reshape(n_b, N_HEADS, HEAD_DIM)
    kn = jnp.repeat(k_s[:n_b].reshape(n_b, KV_HEADS, HEAD_DIM), grp, axis=1)
    vn = jnp.repeat(v_s[:n_b].reshape(n_b, KV_HEADS, HEAD_DIM), grp, axis=1)
    head_spec = pl.BlockSpec((1, N_HEADS, HEAD_DIM), lambda b, s, pt_ref: (b, 0, 0))
    out = pl.pallas_call(
        functools.partial(_attn_sample_kernel, n_pages=n_pages),
        grid_spec=pltpu.PrefetchScalarGridSpec(
            num_scalar_prefetch=1, grid=(n_b, n_steps),
            in_specs=[head_spec, head_spec, head_spec,
                      pl.BlockSpec((1, n_rows, n_cols), lambda b, s, pt_ref: (b, 0, 0))]
            + [pl.BlockSpec((None, 1, n_cols, HEAD_DIM), page_map(r)) for r in range(PAGES_PER_STEP)]
            + [pl.BlockSpec((None, 1, n_cols, HEAD_DIM), page_map(r)) for r in range(PAGES_PER_STEP)],
            out_specs=head_spec,
            scratch_shapes=[pltpu.VMEM((N_HEADS, 1), _F32), pltpu.VMEM((N_HEADS, 1), _F32),
                            pltpu.VMEM((N_HEADS, HEAD_DIM), _F32)]),
        out_shape=jax.ShapeDtypeStruct((n_b, N_HEADS, HEAD_DIM), _F32),
        compiler_params=_params(2), name="attn_sample")(
            pt, q3, kn, vn, bias, *([pk] * PAGES_PER_STEP), *([pv] * PAGES_PER_STEP))
    return out.reshape(n_b, A_Q)


def _gmlp_kernel(u_ref, v_ref, g_ref, b_ref, ws_ref, bs_ref, o_ref, vn_ref=None):
    v = v_ref[...]
    mu = jnp.mean(v, axis=1, keepdims=True)
    vc = v - mu
    var = jnp.mean(vc * vc, axis=1, keepdims=True)
    vn = vc * lax.rsqrt(var + LN_EPS) * g_ref[...] + b_ref[...]
    if vn_ref is not None:
        vn_ref[...] = vn
    gw = GMLP_WIDTH // GMLP_GROUPS
    tril = (lax.broadcasted_iota(jnp.int32, (CHUNK, CHUNK), 0)
            >= lax.broadcasted_iota(jnp.int32, (CHUNK, CHUNK), 1))
    bs = bs_ref[...]
    for g in range(GMLP_GROUPS):
        wm = jnp.where(tril, ws_ref[g], 0.0).astype(_BF16)
        mixed = jnp.dot(wm, vn[:, g * gw:(g + 1) * gw].astype(_BF16), preferred_element_type=_F32)
        mixed = mixed + bs[:, g:g + 1]
        o_ref[:, g * gw:(g + 1) * gw] = (u_ref[:, g * gw:(g + 1) * gw] * mixed).astype(_BF16)


def _gmlp(hb, ln_g, ln_b, ws, bs, want_vn, name):
    m = hb.shape[0]
    w = GMLP_WIDTH
    out_shape = [jax.ShapeDtypeStruct((m, w), _BF16)]
    out_specs = [pl.BlockSpec((CHUNK, w), lambda c: (c, 0))]
    if want_vn:
        out_shape.append(jax.ShapeDtypeStruct((m, w), _F32))
        out_specs.append(pl.BlockSpec((CHUNK, w), lambda c: (c, 0)))
    return pl.pallas_call(
        _gmlp_kernel, grid=(m // CHUNK,),
        in_specs=[pl.BlockSpec((CHUNK, w), lambda c: (c, 0)), pl.BlockSpec((CHUNK, w), lambda c: (c, 1)),
                  pl.BlockSpec((1, w), lambda c: (0, 0)), pl.BlockSpec((1, w), lambda c: (0, 0)),
                  pl.BlockSpec((GMLP_GROUPS, CHUNK, CHUNK), lambda c: (0, 0, 0)),
                  pl.BlockSpec((CHUNK, GMLP_GROUPS), lambda c: (0, 0))],
        out_specs=out_specs, out_shape=out_shape, compiler_params=_params(1), name=name)(
            hb, hb, ln_g.reshape(1, w), ln_b.reshape(1, w), ws, jnp.transpose(bs))


def _head_sums(x, blk):
    outs = []
    for c in range(x.shape[1] // LANES):
        xc = x[:, c * LANES:(c + 1) * LANES]
        hi = xc.astype(_BF16)
        lo = (xc - hi.astype(_F32)).astype(_BF16)
        outs.append(jnp.dot(hi, blk, preferred_element_type=_F32) + jnp.dot(lo, blk, preferred_element_type=_F32))
    return jnp.concatenate(outs, axis=1)


def _rwkv_pre_kernel(pc_ref, sh_ref, prev_ref, mu_ref, w0_ref, w2_ref, a0_ref, a2_ref, kkp_ref, ka_ref,
                     blk_ref, r_ref, w_ref, k_ref, v_ref, kk_ref, b_ref, *, tr, tiles_per_batch, roll_shift):
    pc = pc_ref[...]
    if roll_shift:
        first = pl.program_id(0) % tiles_per_batch == 0
        before = jnp.where(first, prev_ref[0], sh_ref[7:8, :])
        rolled = pltpu.roll(pc, 1, 0)
        rowi = lax.broadcasted_iota(jnp.int32, pc.shape, 0)
        shifted = jnp.where(rowi == 0, before, rolled)
    else:
        shifted = sh_ref[...]
    y = pc + mu_ref[...] * (shifted - pc)
    wdt = RWKV_WIDTH
    r = y[:, 0:wdt]
    k = y[:, wdt:2 * wdt]
    v = y[:, 2 * wdt:3 * wdt]
    wd = y[:, 3 * wdt:3 * wdt + D_DECAY]
    ad = y[:, 3 * wdt + D_DECAY:3 * wdt + D_DECAY + D_AAA]
    z = -(w0_ref[...] + jnp.dot(jnp.tanh(wd).astype(_BF16), w2_ref[...].astype(_BF16),
                                preferred_element_type=_F32))
    softplus = jnp.maximum(z, 0.0) + jnp.log(1.0 + jnp.exp(-jnp.abs(z)))
    decay = jnp.exp(-jnp.exp(-softplus - 0.5))
    a = jax.nn.sigmoid(a0_ref[...] + jnp.dot(ad.astype(_BF16), a2_ref[...].astype(_BF16),
                                             preferred_element_type=_F32))
    kk = k * kkp_ref[...]
    nrm = jnp.maximum(jnp.sqrt(_head_sums(kk * kk, blk_ref[...])), 1e-12)
    kk = kk / nrm
    r_ref[...] = r
    w_ref[...] = decay
    k_ref[...] = k * (1.0 + (a - 1.0) * ka_ref[...])
    v_ref[...] = v
    kk_ref[...] = kk
    b_ref[...] = kk * a


def _seg_blk():
    seg = jnp.arange(LANES) // RWKV_HEAD
    return (seg[:, None] == seg[None, :]).astype(_BF16)


def _rwkv_pre(hc, shift_src, lp, seq, roll_shift, name):
    m = hc.shape[0]
    tr = 256 if (roll_shift and seq % 256 == 0) else (seq if roll_shift else m)
    tpb = seq // tr if roll_shift else 1
    wdt = RWKV_WIDTH
    if roll_shift:
        n_b = m // seq
        sh_spec = pl.BlockSpec((8, C_COLS), lambda i: (jnp.maximum(i * (tr // 8) - 1, 0), 0))
        prev = shift_src.reshape(n_b, 1, C_COLS)
        prev_spec = pl.BlockSpec((1, 1, C_COLS), lambda i: (i // tpb, 0, 0))
        sh_arg = hc
    else:
        sh_spec = pl.BlockSpec((tr, C_COLS), lambda i: (i, 0))
        prev = jnp.zeros((1, 1, C_COLS), _F32)
        prev_spec = pl.BlockSpec((1, 1, C_COLS), lambda i: (0, 0, 0))
        sh_arg = shift_src

    def vec(n):
        return pl.BlockSpec((1, n), lambda i: (0, 0))

    kern = functools.partial(_rwkv_pre_kernel, tr=tr, tiles_per_batch=tpb, roll_shift=roll_shift)
    return pl.pallas_call(
        kern, grid=(m // tr,),
        in_specs=[pl.BlockSpec((tr, C_COLS), lambda i: (i, 0)), sh_spec, prev_spec, vec(C_COLS), vec(wdt),
                  pl.BlockSpec((D_DECAY, wdt), lambda i: (0, 0)), vec(wdt),
                  pl.BlockSpec((D_AAA, wdt), lambda i: (0, 0)), vec(wdt), vec(wdt),
                  pl.BlockSpec((LANES, LANES), lambda i: (0, 0))],
        out_specs=[pl.BlockSpec((tr, wdt), lambda i: (i, 0))] * 6,
        out_shape=[jax.ShapeDtypeStruct((m, wdt), _F32)] * 6,
        compiler_params=_params(1), name=name)(
            hc, sh_arg, prev, lp["rwkv_mu"].reshape(1, C_COLS), lp["rwkv_w0"].reshape(1, wdt), lp["rwkv_w2"],
            lp["rwkv_a0"].reshape(1, wdt), lp["rwkv_a2"], lp["rwkv_kk"].reshape(1, wdt),
            lp["rwkv_ka"].reshape(1, wdt), _seg_blk())


SCAN_BATCH = LANES // RWKV_HEADS


def _rwkv_scan_kernel(*refs, tt, halves):
    cur = refs[0:6]
    nxt = refs[6:12]
    s0_ref, o_ref, st_ref = refs[12:15]
    z = (refs[15:21], refs[21:27])
    zo = refs[27]
    n = RWKV_HEAD
    step_idx = pl.program_id(0)

    def to_lanes(srcs, off, dsts, t):
        for src, dst in zip(srcs, dsts):
            x = jnp.concatenate([src[bb, off + t] for bb in range(SCAN_BATCH)], axis=0)
            dst[t] = x.T

    def from_lanes(t):
        y = zo[t].T
        for bb in range(SCAN_BATCH):
            o_ref[bb, t] = y[bb * RWKV_HEADS:(bb + 1) * RWKV_HEADS]

    def update(zs, t, t_out):
        zr, zw, zk, zv, zkk, zb = zs
        acc = st_ref[0] * zkk[t, pl.ds(0, 1), :]
        for j in range(1, n):
            acc = acc + st_ref[j] * zkk[t, pl.ds(j, 1), :]
        sa = -acc
        vt = zv[t]
        out = None
        for j in range(n):
            sj = st_ref[j] * zw[t, pl.ds(j, 1), :] + sa * zb[t, pl.ds(j, 1), :] + vt * zk[t, pl.ds(j, 1), :]
            st_ref[j] = sj
            term = sj * zr[t, pl.ds(j, 1), :]
            out = term if out is None else out + term
        zo[t_out] = out

    def fill_first():
        def body(t, carry):
            to_lanes(cur, 0, z[0], t)
            return carry
        lax.fori_loop(0, tt, body, 0)

    if halves == 1:
        @pl.when(step_idx == 0)
        def _():
            st_ref[...] = s0_ref[...]

        fill_first()

        def body(t, carry):
            update(z[0], t, t)
            return carry

        lax.fori_loop(0, tt, body, 0)

        def out_body(t, carry):
            from_lanes(t)
            return carry

        lax.fori_loop(0, tt, out_body, 0)
        return

    @pl.when(step_idx == 0)
    def _():
        st_ref[...] = s0_ref[...]
        zo[...] = jnp.zeros(zo.shape, _F32)
        fill_first()

    def first_half(t, carry):
        to_lanes(cur, tt, z[1], t)
        from_lanes(jnp.maximum(t - 1, 0))
        update(z[0], t, t)
        return carry

    lax.fori_loop(0, tt, first_half, 0)

    def second_half(t, carry):
        to_lanes(nxt, 0, z[0], t)
        from_lanes(tt + t - 1)
        update(z[1], t, tt + t)
        return carry

    lax.fori_loop(0, tt, second_half, 0)
    from_lanes(2 * tt - 1)


def _rwkv_scan(vecs, s0, seq, name):
    n = RWKV_HEAD
    halves, tt = (2, 8) if seq % 16 == 0 else (1, seq)
    blk = halves * tt
    n_half = seq // tt
    shape4 = (SCAN_BATCH, seq, RWKV_HEADS, n)
    vspec = pl.BlockSpec((SCAN_BATCH, blk, RWKV_HEADS, n), lambda s: (0, s, 0, 0))
    nspec = pl.BlockSpec((SCAN_BATCH, tt, RWKV_HEADS, n),
                         lambda s: (0, jnp.minimum(halves * (s + 1), n_half - 1), 0, 0))
    sspec = pl.BlockSpec((n, n, LANES), lambda s: (0, 0, 0))
    args = [x.reshape(shape4) for x in vecs]
    o, st = pl.pallas_call(
        functools.partial(_rwkv_scan_kernel, tt=tt, halves=halves), grid=(seq // blk,),
        in_specs=[vspec] * 6 + [nspec] * 6 + [sspec], out_specs=[vspec, sspec],
        out_shape=[jax.ShapeDtypeStruct(shape4, _F32), jax.ShapeDtypeStruct((n, n, LANES), _F32)],
        scratch_shapes=[pltpu.VMEM((tt, n, LANES), _F32)] * 12 + [pltpu.VMEM((blk, n, LANES), _F32)],
        compiler_params=_params(1), name=name)(*args, *args, s0)
    return o.reshape(SCAN_BATCH * seq, RWKV_WIDTH), st


def _state_to_lanes(s):
    return jnp.transpose(s, (3, 2, 0, 1)).reshape(RWKV_HEAD, RWKV_HEAD, LANES)


def _state_from_lanes(s):
    return jnp.transpose(s.reshape(RWKV_HEAD, RWKV_HEAD, SCAN_BATCH, RWKV_HEADS), (2, 3, 1, 0))


def _rwkv_post_kernel(o_ref, r_ref, k_ref, v_ref, g_ref, b_ref, rk_ref, blk_ref, out_ref):
    blk = blk_ref[...]
    o = o_ref[...]
    inv = 1.0 / RWKV_HEAD
    m = _head_sums(o, blk) * inv
    oc = o - m
    var = _head_sums(oc * oc, blk) * inv
    on = oc * lax.rsqrt(var + GN_EPS) * g_ref[...] + b_ref[...]
    bonus = _head_sums(r_ref[...] * k_ref[...] * rk_ref[...], blk) * v_ref[...]
    out_ref[...] = (on + bonus).astype(_BF16)


def _rwkv_post(o, r, k, v, lp, name):
    m = o.shape[0]
    tr = 256 if m % 256 == 0 else m
    wdt = RWKV_WIDTH
    row = pl.BlockSpec((tr, wdt), lambda i: (i, 0))
    vec = pl.BlockSpec((1, wdt), lambda i: (0, 0))
    return pl.pallas_call(
        _rwkv_post_kernel, grid=(m // tr,),
        in_specs=[row, row, row, row, vec, vec, vec, pl.BlockSpec((LANES, LANES), lambda i: (0, 0))],
        out_specs=row, out_shape=jax.ShapeDtypeStruct((m, wdt), _BF16),
        compiler_params=_params(1), name=name)(
            o, r, k, v, lp["rwkv_gn_g"].reshape(1, wdt), lp["rwkv_gn_b"].reshape(1, wdt),
            lp["rwkv_rk"].reshape(1, wdt), _seg_blk())


def _pad_rows(x):
    return jnp.zeros((SAMPLE_ROWS,) + x.shape[1:], x.dtype).at[:x.shape[0]].set(x)


def _layer(xp, xpb, xs, xsb, seq, n_dec, layer, lp, big, cache_k, cache_v, cache_ik, page_table,
           st_shift, st_wkv):
    mp = xp.shape[0]
    n_batch = mp // seq
    tm = 1024 if seq % 1024 == 0 else seq
    d = D_MODEL
    rope_p = _rope_tables(jnp.arange(seq, dtype=jnp.int32))
    rope_s = _rope_tables(jnp.full((SAMPLE_ROWS,), PAST_LEN, jnp.int32))
    rope_extras = [dict(p=a, s=b) for a, b in zip(rope_p, rope_s)]
    lhs_x = [(xpb, xsb)]

    def rope_ep(accs, tex, rex):
        return [_rope_tiles(accs[0], *rex)]

    def pair(w, ncols, wcol, kb=None, kidx=0, lhs=0):
        return dict(lhs=lhs, w=w, layer=layer, kb=w.shape[1] if kb is None else kb, kidx=kidx, ncols=ncols,
                    wcol=wcol)

    def pair_in(ncols, row0):
        n_in = big["w_in_t"].shape[0] // big["depth"]
        return dict(lhs=0, w=big["w_in_t"], layer=layer, kb=d, kidx=0, ncols=ncols,
                    row0=lambda j: pl.multiple_of(layer * n_in + row0(j) + 0 * j, 8))

    tn = 512
    (q_hm, q_s), = _ws_matmul(
        "proj_q", lhs_x, [pair_in(tn, lambda j: j * tn)],
        [dict(ncols=tn, dtype=_BF16, col=lambda j: j, total=A_Q, hm=True)],
        rope_ep, tm=tm, n_steps=A_Q // tn, seq=seq, row_extras=rope_extras)
    qi_start = A_Q + 2 * A_KV
    (qi_hm, qi_s), = _ws_matmul(
        "proj_qi", lhs_x, [pair_in(tn, lambda j: qi_start + j * tn)],
        [dict(ncols=tn, dtype=_BF16, col=lambda j: j, total=IDX_HEADS * IDX_DIM, hm=True)],
        rope_ep, tm=tm, n_steps=IDX_HEADS * IDX_DIM // tn, seq=seq, row_extras=rope_extras)

    ki_start = qi_start + IDX_HEADS * IDX_DIM

    def k_ep(accs, tex, rex):
        kr = _rope_tiles(accs[0], *rex)
        kir = _rope_tiles(accs[1], *rex)
        return [kr, kr, kir, kir]

    zero = lambda j: 0
    (k_f, k_fs), (k_hm, _), (ki_f, ki_fs), (ki_b, _) = _ws_matmul(
        "proj_k", lhs_x,
        [pair_in(A_KV, lambda j: A_Q), pair_in(IDX_DIM, lambda j: ki_start)],
        [dict(ncols=A_KV, dtype=_F32, col=zero, total=A_KV),
         dict(ncols=A_KV, dtype=_BF16, col=zero, total=A_KV, hm=True),
         dict(ncols=IDX_DIM, dtype=_F32, col=zero, total=IDX_DIM),
         dict(ncols=IDX_DIM, dtype=_BF16, col=zero, total=IDX_DIM)],
        k_ep, tm=tm, n_steps=1, seq=seq, row_extras=rope_extras)

    def v_ep(accs, tex, rex):
        return [accs[0], accs[0], accs[1]]

    (v_f, v_fs), (v_hm, _), (wi_f, wi_fs) = _ws_matmul(
        "proj_v", lhs_x,
        [pair_in(A_KV, lambda j: A_Q + A_KV), pair_in(LANES, lambda j: ki_start + IDX_DIM)],
        [dict(ncols=A_KV, dtype=_F32, col=zero, total=A_KV),
         dict(ncols=A_KV, dtype=_BF16, col=zero, total=A_KV, hm=True),
         dict(ncols=LANES, dtype=_F32, col=zero, total=LANES)],
        v_ep, tm=tm, n_steps=1, seq=seq)

    ident = lambda accs, tex, rex: [accs[0]]
    (hb, hb_s), = _ws_matmul(
        "proj_b", lhs_x, [pair_in(tn, lambda j: A_COLS + j * tn)],
        [dict(ncols=tn, dtype=_F32, col=lambda j: j, total=B_COLS)],
        ident, tm=tm, n_steps=B_COLS // tn, seq=seq)
    tn_c = 640
    tm_c = 512 if seq % 512 == 0 else seq
    (hc, hc_s), = _ws_matmul(
        "proj_c", lhs_x, [pair_in(tn_c, lambda j: A_COLS + B_COLS + j * tn_c)],
        [dict(ncols=tn_c, dtype=_F32, col=lambda j: j, total=C_COLS)],
        ident, tm=tm_c, n_steps=C_COLS // tn_c, seq=seq)
    (gate, gate_s), = _ws_matmul(
        "proj_g", lhs_x, [pair_in(tn, lambda j: A_COLS + B_COLS + C_COLS + j * tn)],
        [dict(ncols=tn, dtype=_BF16, col=lambda j: j, total=G_COLS)],
        lambda accs, tex, rex: [jax.nn.sigmoid(accs[0])], tm=tm, n_steps=G_COLS // tn, seq=seq)

    out_a = _attn_prompt(q_hm, qi_hm, ki_b, wi_f, k_hm, v_hm, n_batch, seq)
    out_a_s = _attn_sample(q_s, qi_s, wi_fs, ki_fs, k_fs, v_fs, cache_k, cache_v, cache_ik, page_table, layer)
    out_a_s = _pad_rows(out_a_s.astype(_BF16))

    out_b, = _gmlp(hb, lp["gmlp_ln_g"], lp["gmlp_ln_b"], lp["gmlp_ws"], lp["gmlp_bs"], False, "gmlp_prompt")
    hb_chunks = jnp.zeros((n_dec, CHUNK, B_COLS), _F32).at[:, 0].set(hb_s[:n_dec]).reshape(n_dec * CHUNK, B_COLS)
    ob_s, vn_s = _gmlp(hb_chunks, lp["gmlp_ln_g"], lp["gmlp_ln_b"], lp["gmlp_ws"], lp["gmlp_bs"], True,
                       "gmlp_sample")
    out_b_s = _pad_rows(ob_s.reshape(n_dec, CHUNK, GMLP_WIDTH)[:, 0])
    vn_rows = vn_s.reshape(n_dec, CHUNK, GMLP_WIDTH)[:, 0]

    pre_p = _rwkv_pre(hc, jnp.zeros((n_batch, C_COLS), _F32), lp, seq, True, "rwkv_pre_prompt")
    o_parts, st_parts = [], []
    for b0 in range(0, n_batch, SCAN_BATCH):
        rows = slice(b0 * seq, (b0 + SCAN_BATCH) * seq)
        o_c, st_c = _rwkv_scan([x[rows] for x in pre_p], jnp.zeros((RWKV_HEAD, RWKV_HEAD, LANES), _F32),
                               seq, "rwkv_scan_prompt")
        o_parts.append(o_c)
        st_parts.append(_state_from_lanes(st_c))
    o_p = o_parts[0] if len(o_parts) == 1 else jnp.concatenate(o_parts, axis=0)
    wkv_p = st_parts[0] if len(st_parts) == 1 else jnp.concatenate(st_parts, axis=0)
    out_c = _rwkv_post(o_p, pre_p[0], pre_p[2], pre_p[3], lp, "rwkv_post_prompt")

    pre_s = _rwkv_pre(hc_s, _pad_rows(st_shift), lp, 1, False, "rwkv_pre_sample")
    o_parts, st_parts = [], []
    for b0 in range(0, n_dec, SCAN_BATCH):
        o_c, st_c = _rwkv_scan([x[b0:b0 + SCAN_BATCH] for x in pre_s],
                               _state_to_lanes(st_wkv[b0:b0 + SCAN_BATCH]), 1, "rwkv_scan_sample")
        o_parts.append(o_c)
        st_parts.append(_state_from_lanes(st_c))
    o_s = _pad_rows(jnp.concatenate(o_parts, axis=0))
    wkv_s = jnp.concatenate(st_parts, axis=0)
    out_c_s = _rwkv_post(o_s, pre_s[0], pre_s[2], pre_s[3], lp, "rwkv_post_sample")

    tm_m = 512 if seq % 512 == 0 else seq
    n_g = d // tn

    def merge_ep(accs, tex, rex):
        return [tex[0].astype(_F32) * accs[0] + tex[1].astype(_F32) * accs[1] + tex[2].astype(_F32) * accs[2]]

    (merged, merged_s), = _ws_matmul(
        "merge", [(out_a, out_a_s), (out_b, out_b_s), (out_c, out_c_s)],
        [pair(big["attn_wo"], tn, lambda j: j, lhs=0), pair(big["gmlp_wo"], tn, lambda j: j, lhs=1),
         pair(big["rwkv_wo"], tn, lambda j: j, lhs=2)],
        [dict(ncols=tn, dtype=_BF16, col=lambda j: j, total=d)],
        merge_ep, tm=tm_m, n_steps=d // tn, seq=seq,
        tile_extras=[dict(p=gate, s=gate_s, ncols=tn, col=lambda j, a=a: j + a * n_g) for a in range(3)])

    def resid_ep(accs, tex, rex):
        return [ALPHA * tex[0] + accs[0]]

    (pre1, pre1_s), = _ws_matmul(
        "out_proj", [(merged, merged_s)], [pair(big["w_out"], tn, lambda j: j)],
        [dict(ncols=tn, dtype=_F32, col=lambda j: j, total=d)],
        resid_ep, tm=tm, n_steps=d // tn, seq=seq,
        tile_extras=[dict(p=xp, s=xs, ncols=tn, col=lambda j: j)])
    x1, x1b = _layernorm(pre1, lp["ln1_g"], lp["ln1_b"], "ln1_prompt")
    x1_s, x1b_s = _layernorm(pre1_s, lp["ln1_g"], lp["ln1_b"], "ln1_sample")

    tn_f = 256
    n_f = D_FF // tn_f

    def swiglu_ep(accs, tex, rex):
        return [jax.nn.silu(accs[0]) * accs[1]]

    (act, act_s), = _ws_matmul(
        "ffn_in", [(x1b, x1b_s)],
        [pair(big["ffn_w_in"], tn_f, lambda j: j), pair(big["ffn_w_in"], tn_f, lambda j: j + n_f)],
        [dict(ncols=tn_f, dtype=_BF16, col=lambda j: j, total=D_FF)],
        swiglu_ep, tm=tm, n_steps=n_f, seq=seq)

    kb = D_FF // 2
    part, part_s, scale = x1, x1_s, ALPHA
    for half in range(2):
        def acc_ep(accs, tex, rex, scale=scale):
            return [scale * tex[0] + accs[0]]

        (part, part_s), = _ws_matmul(
            "ffn_out%d" % half, [(act, act_s)],
            [pair(big["ffn_w_out"], tn, lambda j: j, kb=kb, kidx=half)],
            [dict(ncols=tn, dtype=_F32, col=lambda j: j, total=d)],
            acc_ep, tm=tm_m, n_steps=d // tn, seq=seq,
            tile_extras=[dict(p=part, s=part_s, ncols=tn, col=lambda j: j)])
        scale = 1.0
    x2, x2b = _layernorm(part, lp["ln2_g"], lp["ln2_b"], "ln2_prompt")
    x2_s, x2b_s = _layernorm(part_s, lp["ln2_g"], lp["ln2_b"], "ln2_sample")

    caches = dict(
        k_p=k_f.reshape(n_batch, seq, KV_HEADS, HEAD_DIM), v_p=v_f.reshape(n_batch, seq, KV_HEADS, HEAD_DIM),
        ik_p=ki_f.reshape(n_batch, seq, IDX_DIM), wkv_p=wkv_p,
        sh_p=hc.reshape(n_batch, seq, C_COLS)[:, -1],
        k_s=k_fs[:n_dec].reshape(n_dec, 1, KV_HEADS, HEAD_DIM), v_s=v_fs[:n_dec].reshape(n_dec, 1, KV_HEADS, HEAD_DIM),
        ik_s=ki_fs[:n_dec].reshape(n_dec, 1, IDX_DIM), wkv_s=wkv_s, sh_s=hc_s[:n_dec],
        gv_s=vn_rows.reshape(n_dec, 1, GMLP_WIDTH))
    return x2, x2b, x2_s, x2b_s, caches


def kernel(x_prompt, x_sample, cache_k, cache_v, cache_idx_k, page_table, state_wkv, state_shift, w_in, attn_wo, gmlp_ln_g, gmlp_ln_b, gmlp_ws, gmlp_bs, gmlp_wo, rwkv_mu, rwkv_w0, rwkv_w2, rwkv_a0, rwkv_a2, rwkv_kk, rwkv_ka, rwkv_rk, rwkv_gn_g, rwkv_gn_b, rwkv_wo, w_out, ln1_g, ln1_b, ffn_w_in, ffn_w_out, ln2_g, ln2_b):
    n_batch, seq, d = x_prompt.shape
    n_dec = x_sample.shape[0]
    depth = w_in.shape[0]
    assert x_sample.shape[1] == 1 and n_dec <= SAMPLE_ROWS and n_dec % SCAN_BATCH == 0
    assert n_batch % SCAN_BATCH == 0 and d == D_MODEL and seq % CHUNK == 0
    xp = x_prompt.reshape(n_batch * seq, d)
    xs = _pad_rows(x_sample.reshape(n_dec, d))
    xpb, xsb = xp.astype(_BF16), xs.astype(_BF16)
    big = dict(w_in_t=jnp.swapaxes(w_in, 1, 2).reshape(depth * w_in.shape[2], d), depth=depth, attn_wo=attn_wo, gmlp_wo=gmlp_wo, rwkv_wo=rwkv_wo, w_out=w_out,
               ffn_w_in=ffn_w_in, ffn_w_out=ffn_w_out)
    per_layer = []
    for l in range(depth):
        lp = dict(gmlp_ln_g=gmlp_ln_g[l], gmlp_ln_b=gmlp_ln_b[l], gmlp_ws=gmlp_ws[l], gmlp_bs=gmlp_bs[l],
                  rwkv_mu=rwkv_mu[l], rwkv_w0=rwkv_w0[l], rwkv_w2=rwkv_w2[l], rwkv_a0=rwkv_a0[l],
                  rwkv_a2=rwkv_a2[l], rwkv_kk=rwkv_kk[l], rwkv_ka=rwkv_ka[l], rwkv_rk=rwkv_rk[l],
                  rwkv_gn_g=rwkv_gn_g[l], rwkv_gn_b=rwkv_gn_b[l], ln1_g=ln1_g[l], ln1_b=ln1_b[l],
                  ln2_g=ln2_g[l], ln2_b=ln2_b[l])
        xp, xpb, xs, xsb, c = _layer(xp, xpb, xs, xsb, seq, n_dec, l, lp, big, cache_k, cache_v, cache_idx_k,
                                     page_table, state_shift[l], state_wkv[l])
        per_layer.append(c)

    def stack(name):
        return jnp.stack([c[name] for c in per_layer])

    return (xp.reshape(n_batch, seq, d), xs[:n_dec].reshape(n_dec, 1, d),
            stack("k_p"), stack("v_p"), stack("ik_p"), stack("wkv_p"), stack("sh_p"),
            stack("k_s"), stack("v_s"), stack("ik_s"), stack("wkv_s"), stack("sh_s"), stack("gv_s"))
```

```python
import functools

import jax
import jax.numpy as jnp
from jax import lax
from jax.experimental import pallas as pl
from jax.experimental.pallas import tpu as pltpu

D_MODEL = 4096
PAST_LEN = 16384
PAGE_SIZE = 128
N_HEADS = 16
KV_HEADS = 4
HEAD_DIM = 128
ROT_DIM = HEAD_DIM // 4
ROPE_THETA = 500000.0
IDX_HEADS = 32
IDX_DIM = 128
TOPK_MAX = 256
A_Q = N_HEADS * HEAD_DIM
A_KV = KV_HEADS * HEAD_DIM
CHUNK = 128
GMLP_WIDTH = D_MODEL // 2
GMLP_GROUPS = 8
RWKV_HEAD = 64
RWKV_WIDTH = D_MODEL // 2
RWKV_HEADS = RWKV_WIDTH // RWKV_HEAD
D_DECAY = max(32, int(round(D_MODEL ** 0.5 * 1.8 / 32)) * 32)
D_AAA = D_DECAY
D_FF = ((8 * D_MODEL // 3 + 255) // 256) * 256
A_COLS = A_Q + 2 * A_KV + IDX_HEADS * IDX_DIM + IDX_DIM + IDX_HEADS
B_COLS = 2 * GMLP_WIDTH
C_COLS = 3 * RWKV_WIDTH + D_DECAY + D_AAA
G_COLS = 3 * D_MODEL
DEPTH = 2
ALPHA = (2 * DEPTH) ** 0.25
LN_EPS = 1e-5
GN_EPS = 64e-5

LANES = 128
SAMPLE_ROWS = 16
VMEM_LIMIT = 58 * 1024 * 1024
KEY_CHUNK = 512
INT_MIN = -(2 ** 31)
NEG_BIG = -1e30

_F32 = jnp.float32
_BF16 = jnp.bfloat16
_NT = (((1,), (1,)), ((), ()))


def _params(n_grid):
    return pltpu.CompilerParams(dimension_semantics=("arbitrary",) * n_grid,
                                vmem_limit_bytes=VMEM_LIMIT)


def _ws_matmul(name, lhs, pairs, outs, epilogue, *, tm, n_steps, seq, tile_extras=(), row_extras=()):
    mp = lhs[0][0].shape[0]
    n_i = mp // tm
    tpb = seq // tm
    n_batch = mp // seq

    in_specs, args = [], []
    for a, (xp, xs) in enumerate(lhs):
        kb, kidx = [(p["kb"], p["kidx"]) for p in pairs if p["lhs"] == a][0]
        in_specs += [pl.BlockSpec((tm, kb), lambda j, i, kidx=kidx: (i, kidx)),
                     pl.BlockSpec((SAMPLE_ROWS, kb), lambda j, i, kidx=kidx: (0, kidx))]
        args += [xp, xs]
    for p in pairs:
        if p.get("row0") is not None:
            in_specs.append(pl.BlockSpec((pl.Element(p["ncols"]), pl.Element(p["kb"])),
                                         lambda j, i, p=p: (p["row0"](j), 0)))
        else:
            in_specs.append(pl.BlockSpec((None, p["kb"], p["ncols"]),
                                         lambda j, i, p=p: (p["layer"], p["kidx"], p["wcol"](j))))
        args.append(p["w"])
    for e in tile_extras:
        in_specs += [pl.BlockSpec((tm, e["ncols"]), lambda j, i, e=e: (i, e["col"](j))),
                     pl.BlockSpec((SAMPLE_ROWS, e["ncols"]), lambda j, i, e=e: (0, e["col"](j)))]
        args += [e["p"], e["s"]]
    for e in row_extras:
        in_specs += [pl.BlockSpec((tm, LANES), lambda j, i: (i % tpb, 0)),
                     pl.BlockSpec((SAMPLE_ROWS, LANES), lambda j, i: (0, 0))]
        args += [e["p"], e["s"]]

    out_specs, out_shapes = [], []
    for o in outs:
        if o.get("hm"):
            hpt = o["ncols"] // LANES
            out_shapes.append(jax.ShapeDtypeStruct((n_batch, o["total"] // LANES, seq, LANES), o["dtype"]))
            out_specs.append(pl.BlockSpec((1, hpt, tm, LANES),
                                          lambda j, i, o=o: (i // tpb, o["col"](j), i % tpb, 0)))
        else:
            out_shapes.append(jax.ShapeDtypeStruct((mp, o["total"]), o["dtype"]))
            out_specs.append(pl.BlockSpec((tm, o["ncols"]), lambda j, i, o=o: (i, o["col"](j))))
        out_shapes.append(jax.ShapeDtypeStruct((SAMPLE_ROWS, o["total"]), o["dtype"]))
        out_specs.append(pl.BlockSpec((SAMPLE_ROWS, o["ncols"]), lambda j, i, o=o: (0, o["col"](j))))

    scratch = [pltpu.VMEM((p["kb"], p["ncols"]), _BF16) for p in pairs]
    n_lhs, n_pairs, n_te, n_re, n_out = len(lhs), len(pairs), len(tile_extras), len(row_extras), len(outs)

    def kernel(*refs):
        pos = 0
        x_refs = [(refs[pos + 2 * a], refs[pos + 2 * a + 1]) for a in range(n_lhs)]
        pos += 2 * n_lhs
        w_refs = refs[pos:pos + n_pairs]
        pos += n_pairs
        te_refs = [(refs[pos + 2 * a], refs[pos + 2 * a + 1]) for a in range(n_te)]
        pos += 2 * n_te
        re_refs = [(refs[pos + 2 * a], refs[pos + 2 * a + 1]) for a in range(n_re)]
        pos += 2 * n_re
        o_refs = [(refs[pos + 2 * a], refs[pos + 2 * a + 1]) for a in range(n_out)]
        pos += 2 * n_out
        wb_refs = refs[pos:pos + n_pairs]
        i = pl.program_id(1)

        def run(which):
            accs = [jnp.dot(x_refs[p["lhs"]][which][...], wb[...], preferred_element_type=_F32)
                    for p, wb in zip(pairs, wb_refs)]
            res = epilogue(accs, [t[which][...] for t in te_refs], [r[which][...] for r in re_refs])
            for o, (op_ref, os_ref), val in zip(outs, o_refs, res):
                if which == 0 and o.get("hm"):
                    for h in range(o["ncols"] // LANES):
                        op_ref[0, h] = val[:, h * LANES:(h + 1) * LANES].astype(o["dtype"])
                else:
                    (op_ref if which == 0 else os_ref)[...] = val.astype(o["dtype"])

        @pl.when(i == 0)
        def _():
            for p, w_ref, wb in zip(pairs, w_refs, wb_refs):
                if p.get("row0") is not None:
                    for kc in range(p["kb"] // LANES):
                        blk = w_ref[:, kc * LANES:(kc + 1) * LANES]
                        wb[kc * LANES:(kc + 1) * LANES, :] = blk.T.astype(_BF16)
                    continue
                rows = 256 if p["kb"] % 256 == 0 else 128

                def cast(c, carry, w_ref=w_ref, wb=wb, rows=rows):
                    r0 = pl.multiple_of(c * rows, rows)
                    wb[pl.ds(r0, rows), :] = w_ref[pl.ds(r0, rows), :].astype(_BF16)
                    return carry

                lax.fori_loop(0, p["kb"] // rows, cast, 0)
            run(1)

        run(0)

    res = pl.pallas_call(
        kernel, grid=(n_steps, n_i), in_specs=in_specs, out_specs=out_specs, out_shape=out_shapes,
        scratch_shapes=scratch, compiler_params=_params(2), name=name)(*args)
    return [(res[2 * a], res[2 * a + 1]) for a in range(n_out)]


def _rope_tiles(x, c, s1, s2):
    outs = []
    for h in range(x.shape[1] // LANES):
        xh = x[:, h * LANES:(h + 1) * LANES]
        outs.append(xh * c + pltpu.roll(xh, ROT_DIM // 2, 1) * s1 + pltpu.roll(xh, LANES - ROT_DIM // 2, 1) * s2)
    return outs[0] if len(outs) == 1 else jnp.concatenate(outs, axis=1)


def _rope_tables(pos):
    half = ROT_DIM // 2
    inv = ROPE_THETA ** (-2.0 * jnp.arange(half, dtype=_F32) / ROT_DIM)
    ang = pos.astype(_F32)[:, None] * inv[None, :]
    cos, sin = jnp.cos(ang), jnp.sin(ang)
    n = pos.shape[0]
    c = jnp.concatenate([cos, cos, jnp.ones((n, LANES - ROT_DIM), _F32)], axis=1)
    s1 = jnp.concatenate([jnp.zeros((n, half), _F32), sin, jnp.zeros((n, LANES - ROT_DIM), _F32)], axis=1)
    s2 = jnp.concatenate([-sin, jnp.zeros((n, LANES - half), _F32)], axis=1)
    return c, s1, s2


def _ln_kernel(x_ref, g_ref, b_ref, o_ref, ob_ref):
    x = x_ref[...]
    mu = jnp.mean(x, axis=1, keepdims=True)
    xc = x - mu
    var = jnp.mean(xc * xc, axis=1, keepdims=True)
    y = xc * lax.rsqrt(var + LN_EPS) * g_ref[...] + b_ref[...]
    o_ref[...] = y
    ob_ref[...] = y.astype(_BF16)


def _layernorm(x, g, b, name):
    m, d = x.shape
    tr = 256 if m % 256 == 0 else m
    return pl.pallas_call(
        _ln_kernel, grid=(m // tr,),
        in_specs=[pl.BlockSpec((tr, d), lambda i: (i, 0)), pl.BlockSpec((1, d), lambda i: (0, 0)),
                  pl.BlockSpec((1, d), lambda i: (0, 0))],
        out_specs=[pl.BlockSpec((tr, d), lambda i: (i, 0)), pl.BlockSpec((tr, d), lambda i: (i, 0))],
        out_shape=[jax.ShapeDtypeStruct((m, d), _F32), jax.ShapeDtypeStruct((m, d), _BF16)],
        compiler_params=_params(1), name=name)(x, g.reshape(1, d), b.reshape(1, d))


def _sortable(score):
    bits = pltpu.bitcast(score, jnp.int32)
    return bits ^ ((bits >> 31) & jnp.int32(0x7FFFFFFF))


def _topk_bias(n_ch, load_key, meta, store_bias, k_sel, shape, axes, index_bits, neg):
    kf = float(k_sel)

    def count(pred):
        if axes == (0,):
            def cb(c, part):
                x = jnp.where(pred(c), 1.0, 0.0)
                slabs = [x[i * 8:(i + 1) * 8] for i in range(x.shape[0] // 8)]
                while len(slabs) > 1:
                    slabs = [slabs[i] + slabs[i + 1] for i in range(0, len(slabs), 2)]
                return part + slabs[0]
            part = lax.fori_loop(0, n_ch, cb, jnp.zeros((8, shape[1]), _F32))
            return jnp.sum(part, axis=0, keepdims=True)

        def cb(c, cnt):
            return cnt + jnp.sum(jnp.where(pred(c), 1.0, 0.0), axis=axes, keepdims=True)
        return lax.fori_loop(0, n_ch, cb, jnp.zeros(shape, _F32))

    def bisect(it, t):
        cand = t + lax.shift_left(jnp.int32(1), jnp.int32(31) - it)
        return jnp.where(count(lambda c: load_key(c) >= cand) >= kf, cand, t)

    t = lax.fori_loop(0, 32, bisect, jnp.full(shape, INT_MIN, jnp.int32))
    n_ge = count(lambda c: (load_key(c) >= t) & meta(c)[0])

    def tie_limit():
        need = kf - count(lambda c: load_key(c) > t)

        def body(it, m):
            cand = m + lax.shift_left(jnp.int32(1), jnp.int32(index_bits - 1) - it)
            cnt = count(lambda c: (load_key(c) == t) & meta(c)[0] & (meta(c)[1] < cand))
            return jnp.where(cnt < need, cand, m)

        return lax.fori_loop(0, index_bits, body, jnp.zeros(shape, jnp.int32))

    no_limit = jnp.full(shape, 2 ** 30, jnp.int32)
    m = lax.cond(jnp.max(n_ge) > kf, tie_limit, lambda: no_limit)

    def emit(c, carry):
        keys = load_key(c)
        valid, idx = meta(c)
        store_bias(c, jnp.where((keys > t) | ((keys == t) & valid & (idx <= m)), 0.0, neg))
        return carry

    lax.fori_loop(0, n_ch, emit, 0)


def _attn_prompt_kernel(q_ref, qi_ref, ki_ref, wi_ref, k_ref, v_ref, o_ref,
                        acc_ref, key_ref, bias_ref, wb_ref, l_ref, oacc_ref, sacc_ref, oh_ref,
                        *, tq, kc, k_sel, index_bits):
    q0 = pl.program_id(1) * tq
    n_ch = (q0 + tq + kc - 1) // kc
    grp = N_HEADS // KV_HEADS
    wi = wi_ref[...] * ((IDX_HEADS * IDX_DIM) ** -0.5)
    for h in range(IDX_HEADS):
        wb_ref[h] = jnp.broadcast_to(wi[:, h:h + 1], (tq, LANES))

    def meta(c):
        col = c * kc + lax.broadcasted_iota(jnp.int32, (tq, kc), 1)
        row = q0 + lax.broadcasted_iota(jnp.int32, (tq, kc), 0)
        return col <= row, col

    def meta_t(c):
        col = c * kc + lax.broadcasted_iota(jnp.int32, (kc, tq), 0)
        row = q0 + lax.broadcasted_iota(jnp.int32, (kc, tq), 1)
        return col <= row, col

    def score_chunk(c, carry):
        ki = ki_ref[pl.ds(pl.multiple_of(c * kc, kc), kc), :]
        acc_ref[...] = jnp.zeros((tq, kc), _F32)

        def idx_body(hc, carry2):
            qc = qi_ref[0, pl.ds(hc * grp, grp)].reshape(grp * tq, IDX_DIM)
            d = lax.dot_general(qc, ki, _NT, preferred_element_type=_F32)
            part = None
            for hh in range(grp):
                w = jnp.tile(wb_ref[hc * grp + hh], (1, kc // LANES))
                term = jnp.maximum(d[hh * tq:(hh + 1) * tq], 0.0) * w
                part = term if part is None else part + term
            acc_ref[...] += part
            return carry2

        lax.fori_loop(0, IDX_HEADS // grp, idx_body, 0)
        keys = jnp.where(meta(c)[0], _sortable(acc_ref[...]), INT_MIN)
        key_ref[c] = pltpu.bitcast(pltpu.bitcast(keys, _F32).T, jnp.int32)
        return carry

    lax.fori_loop(0, n_ch, score_chunk, 0)

    def store_bias(c, val):
        bias_ref[c] = val.T

    _topk_bias(n_ch, lambda c: key_ref[c], meta_t, store_bias, k_sel, (1, tq), (0,), index_bits, -jnp.inf)

    def att_body(g, carry):
        qg = q_ref[0, pl.ds(g * grp, grp)].reshape(grp * tq, HEAD_DIM)

        def logits_chunk(c, m):
            kg = k_ref[0, g, pl.ds(pl.multiple_of(c * kc, kc), kc), :]
            lg = lax.dot_general(qg, kg, _NT, preferred_element_type=_F32) * (HEAD_DIM ** -0.5)
            lg = lg + jnp.tile(bias_ref[c], (grp, 1))
            l_ref[c] = lg
            return jnp.maximum(m, jnp.max(lg, axis=1, keepdims=True))

        m = lax.fori_loop(0, n_ch, logits_chunk, jnp.full((grp * tq, 1), -jnp.inf, _F32))
        oacc_ref[...] = jnp.zeros(oacc_ref.shape, _F32)
        sacc_ref[...] = jnp.zeros(sacc_ref.shape, _F32)

        def pv_chunk(c, carry2):
            vg = v_ref[0, g, pl.ds(pl.multiple_of(c * kc, kc), kc), :]
            p = jnp.exp(l_ref[c] - m)
            sacc_ref[...] += jnp.sum(p, axis=1, keepdims=True)
            oacc_ref[...] += jnp.dot(p.astype(_BF16), vg, preferred_element_type=_F32)
            return carry2

        lax.fori_loop(0, n_ch, pv_chunk, 0)
        out = oacc_ref[...] / sacc_ref[...]
        for hh in range(grp):
            oh_ref[g * grp + hh] = out[hh * tq:(hh + 1) * tq]
        return carry

    lax.fori_loop(0, KV_HEADS, att_body, 0)
    for h in range(N_HEADS):
        o_ref[:, h * HEAD_DIM:(h + 1) * HEAD_DIM] = oh_ref[h].astype(_BF16)


def _attn_prompt(q_hm, qi_hm, ki_b, wi, k_hm, v_hm, n_batch, seq):
    tq = 128
    kc = KEY_CHUNK if seq % KEY_CHUNK == 0 else seq
    nqb = seq // tq
    grp = N_HEADS // KV_HEADS
    k_sel = min(TOPK_MAX, seq // 4)
    kern = functools.partial(_attn_prompt_kernel, tq=tq, kc=kc, k_sel=k_sel,
                             index_bits=(seq - 1).bit_length() + 1)
    return pl.pallas_call(
        kern, grid=(n_batch, nqb),
        in_specs=[
            pl.BlockSpec((1, N_HEADS, tq, HEAD_DIM), lambda b, q: (b, 0, q, 0)),
            pl.BlockSpec((1, IDX_HEADS, tq, IDX_DIM), lambda b, q: (b, 0, q, 0)),
            pl.BlockSpec((seq, IDX_DIM), lambda b, q: (b, 0)),
            pl.BlockSpec((tq, LANES), lambda b, q: (b * nqb + q, 0)),
            pl.BlockSpec((1, KV_HEADS, seq, HEAD_DIM), lambda b, q: (b, 0, 0, 0)),
            pl.BlockSpec((1, KV_HEADS, seq, HEAD_DIM), lambda b, q: (b, 0, 0, 0)),
        ],
        out_specs=pl.BlockSpec((tq, A_Q), lambda b, q: (b * nqb + q, 0)),
        out_shape=jax.ShapeDtypeStruct((n_batch * seq, A_Q), _BF16),
        scratch_shapes=[pltpu.VMEM((tq, kc), _F32), pltpu.VMEM((seq // kc, kc, tq), jnp.int32),
                        pltpu.VMEM((seq // kc, tq, kc), _F32), pltpu.VMEM((IDX_HEADS, tq, LANES), _F32),
                        pltpu.VMEM((seq // kc, grp * tq, kc), _F32), pltpu.VMEM((grp * tq, HEAD_DIM), _F32),
                        pltpu.VMEM((grp * tq, 1), _F32), pltpu.VMEM((N_HEADS, tq, HEAD_DIM), _F32)],
        compiler_params=_params(2), name="attn_prompt")(q_hm, qi_hm, ki_b, wi, k_hm, v_hm)


SCORE_PAGES_PER_STEP = 32
ATTN_PAGES_PER_STEP = 16


def _attn_sample_score_kernel(pt_ref, qi_ref, wi_ref, kin_ref, exp_ref, *rest, n_pages, k_sel, pps):
    pages = rest[:pps]
    bias_ref, score_ref, key_ref, sel_ref = rest[pps:]
    del pt_ref
    s = pl.program_id(1)
    n_rows = n_pages + 8
    qi = qi_ref[0]
    wcol = wi_ref[0] * ((IDX_HEADS * IDX_DIM) ** -0.5)
    for r in range(pps):
        page = pages[r][0].astype(_BF16)
        d = lax.dot_general(qi, page, _NT, preferred_element_type=_F32)
        sc = jnp.sum(jnp.maximum(d, 0.0) * wcol, axis=0, keepdims=True)
        score_ref[pl.ds(s * pps + r, 1), :] = sc

    @pl.when(s == pl.num_programs(1) - 1)
    def _():
        kin = kin_ref[0].astype(_BF16).astype(_F32)
        dn = jnp.sum(qi.astype(_F32) * kin, axis=1, keepdims=True)
        sn = jnp.sum(jnp.maximum(dn, 0.0) * wcol, axis=0, keepdims=True)
        score_ref[pl.ds(n_pages, 8), :] = jnp.broadcast_to(sn, (8, LANES))
        rowi = lax.broadcasted_iota(jnp.int32, (n_rows, LANES), 0)
        lane = lax.broadcasted_iota(jnp.int32, (n_rows, LANES), 1)
        valid = (rowi < n_pages) | ((rowi == n_pages) & (lane == 0))
        pos = rowi * LANES + lane
        key_ref[...] = jnp.where(valid, _sortable(score_ref[...]), INT_MIN)

        def store_sel(c, val):
            sel_ref[...] = val

        _topk_bias(1, lambda c: key_ref[...], lambda c: (valid, pos), store_sel, k_sel, (1, 1), (0, 1),
                   (n_rows * LANES - 1).bit_length() + 1, 1.0)
        rep = jnp.dot(sel_ref[...].astype(_BF16), exp_ref[...], preferred_element_type=_F32)
        bias_ref[0] = jnp.where(rep < 0.5, 0.0, NEG_BIG)


def _attn_sample_kernel(pt_ref, q_ref, kn_ref, vn_ref, bias_ref, *rest, n_pages, pps):
    kpages = rest[:pps]
    vpages = rest[pps:2 * pps]
    o_ref, m_ref, s_ref, acc_ref = rest[2 * pps:]
    del pt_ref
    st = pl.program_id(1)
    scale = HEAD_DIM ** -0.5
    grp = N_HEADS // KV_HEADS
    n_cols = PAGE_SIZE * KV_HEADS
    q = q_ref[0]
    own = (lax.broadcasted_iota(jnp.int32, (N_HEADS, n_cols), 1) % KV_HEADS
           == lax.broadcasted_iota(jnp.int32, (N_HEADS, n_cols), 0) // grp)
    head_bias = jnp.where(own, 0.0, NEG_BIG)

    @pl.when(st == 0)
    def _():
        m_ref[...] = jnp.full(m_ref.shape, NEG_BIG, _F32)
        s_ref[...] = jnp.zeros(s_ref.shape, _F32)
        acc_ref[...] = jnp.zeros(acc_ref.shape, _F32)

    def update(logit_list, pv_fns):
        m_old = m_ref[...]
        m_new = m_old
        for lg in logit_list:
            m_new = jnp.maximum(m_new, jnp.max(lg, axis=1, keepdims=True))
        alpha = jnp.exp(m_old - m_new)
        s_new = s_ref[...] * alpha
        acc = acc_ref[...] * alpha
        for lg, pv_fn in zip(logit_list, pv_fns):
            p = jnp.exp(lg - m_new)
            s_new = s_new + jnp.sum(p, axis=1, keepdims=True)
            acc = acc + pv_fn(p)
        s_ref[...] = s_new
        acc_ref[...] = acc
        m_ref[...] = m_new

    logit_list, pv_fns = [], []
    for r in range(pps):
        kp = kpages[r][0].astype(_BF16)
        logits = lax.dot_general(q, kp, _NT, preferred_element_type=_F32) * scale
        logit_list.append(logits + head_bias + bias_ref[0, pl.ds(st * pps + r, 1), :])
        pv_fns.append(lambda p, r=r: jnp.dot(p.astype(_BF16), vpages[r][0].astype(_BF16),
                                             preferred_element_type=_F32))
    update(logit_list, pv_fns)

    @pl.when(st == pl.num_programs(1) - 1)
    def _():
        kn = kn_ref[0].astype(_BF16).astype(_F32)
        vn = vn_ref[0].astype(_BF16).astype(_F32)
        ln = jnp.sum(q.astype(_F32) * kn, axis=1, keepdims=True) * scale
        ln = ln + bias_ref[0, pl.ds(n_pages, 1), :][:, 0:1]
        update([ln], [lambda p: p * vn])
        o_ref[0] = acc_ref[...] / s_ref[...]


def _attn_sample(q_s, qi_s, wi_s, ki_s, k_s, v_s, cache_k, cache_v, cache_ik, page_table, layer):
    n_b, n_pages = page_table.shape
    depth, n_phys = cache_ik.shape[:2]
    k_sel = min(TOPK_MAX, (n_pages * PAGE_SIZE + 1) // 4)
    pps_s = min(SCORE_PAGES_PER_STEP, n_pages)
    pps_a = min(ATTN_PAGES_PER_STEP, n_pages)
    assert n_pages % pps_s == 0 and n_pages % pps_a == 0
    n_rows = n_pages + 8
    n_cols = PAGE_SIZE * KV_HEADS
    pt = page_table.reshape(-1)
    grp = N_HEADS // KV_HEADS

    def page_map(r, pps):
        return lambda b, s, pt_ref: (layer, pt_ref[b * n_pages + s * pps + r], 0, 0)

    qi3 = qi_s[:n_b].reshape(n_b, IDX_HEADS, IDX_DIM)
    wi3 = wi_s[:n_b, :IDX_HEADS].reshape(n_b, IDX_HEADS, 1)
    kin3 = ki_s[:n_b].reshape(n_b, 1, IDX_DIM)
    expand = (jnp.arange(LANES)[:, None] == jnp.arange(n_cols)[None, :] // KV_HEADS).astype(_BF16)
    bias = pl.pallas_call(
        functools.partial(_attn_sample_score_kernel, n_pages=n_pages, k_sel=k_sel, pps=pps_s),
        grid_spec=pltpu.PrefetchScalarGridSpec(
            num_scalar_prefetch=1, grid=(n_b, n_pages // pps_s),
            in_specs=[pl.BlockSpec((1, IDX_HEADS, IDX_DIM), lambda b, s, pt_ref: (b, 0, 0)),
                      pl.BlockSpec((1, IDX_HEADS, 1), lambda b, s, pt_ref: (b, 0, 0)),
                      pl.BlockSpec((1, 1, IDX_DIM), lambda b, s, pt_ref: (b, 0, 0)),
                      pl.BlockSpec((LANES, n_cols), lambda b, s, pt_ref: (0, 0))]
            + [pl.BlockSpec((None, 1, PAGE_SIZE, IDX_DIM), page_map(r, pps_s)) for r in range(pps_s)],
            out_specs=pl.BlockSpec((1, n_rows, n_cols), lambda b, s, pt_ref: (b, 0, 0)),
            scratch_shapes=[pltpu.VMEM((n_rows, LANES), _F32), pltpu.VMEM((n_rows, LANES), jnp.int32),
                            pltpu.VMEM((n_rows, LANES), _F32)]),
        out_shape=jax.ShapeDtypeStruct((n_b, n_rows, n_cols), _F32),
        compiler_params=_params(2), name="attn_sample_score")(
            pt, qi3, wi3, kin3, expand, *([cache_ik] * pps_s))

    pk = cache_k.reshape(depth, n_phys, n_cols, HEAD_DIM)
    pv = cache_v.reshape(depth, n_phys, n_cols, HEAD_DIM)
    q3 = q_s[:n_b].reshape(n_b, N_HEADS, HEAD_DIM)
    kn = jnp.repeat(k_s[:n_b].reshape(n_b, KV_HEADS, HEAD_DIM), grp, axis=1)
    vn = jnp.repeat(v_s[:n_b].reshape(n_b, KV_HEADS, HEAD_DIM), grp, axis=1)
    head_spec = pl.BlockSpec((1, N_HEADS, HEAD_DIM), lambda b, s, pt_ref: (b, 0, 0))
    out = pl.pallas_call(
        functools.partial(_attn_sample_kernel, n_pages=n_pages, pps=pps_a),
        grid_spec=pltpu.PrefetchScalarGridSpec(
            num_scalar_prefetch=1, grid=(n_b, n_pages // pps_a),
            in_specs=[head_spec, head_spec, head_spec,
                      pl.BlockSpec((1, n_rows, n_cols), lambda b, s, pt_ref: (b, 0, 0))]
            + [pl.BlockSpec((None, 1, n_cols, HEAD_DIM), page_map(r, pps_a)) for r in range(pps_a)]
            + [pl.BlockSpec((None, 1, n_cols, HEAD_DIM), page_map(r, pps_a)) for r in range(pps_a)],
            out_specs=head_spec,
            scratch_shapes=[pltpu.VMEM((N_HEADS, 1), _F32), pltpu.VMEM((N_HEADS, 1), _F32),
                            pltpu.VMEM((N_HEADS, HEAD_DIM), _F32)]),
        out_shape=jax.ShapeDtypeStruct((n_b, N_HEADS, HEAD_DIM), _F32),
        compiler_params=_params(2), name="attn_sample")(
            pt, q3, kn, vn, bias, *([pk] * pps_a), *([pv] * pps_a))
    return out.reshape(n_b, A_Q)


def _gmlp_kernel(u_ref, v_ref, g_ref, b_ref, ws_ref, bs_ref, o_ref, vn_ref=None):
    v = v_ref[...]
    mu = jnp.mean(v, axis=1, keepdims=True)
    vc = v - mu
    var = jnp.mean(vc * vc, axis=1, keepdims=True)
    vn = vc * lax.rsqrt(var + LN_EPS) * g_ref[...] + b_ref[...]
    if vn_ref is not None:
        vn_ref[...] = vn
    gw = GMLP_WIDTH // GMLP_GROUPS
    tril = (lax.broadcasted_iota(jnp.int32, (CHUNK, CHUNK), 0)
            >= lax.broadcasted_iota(jnp.int32, (CHUNK, CHUNK), 1))
    bs = bs_ref[...]
    for g in range(GMLP_GROUPS):
        wm = jnp.where(tril, ws_ref[g], 0.0).astype(_BF16)
        mixed = jnp.dot(wm, vn[:, g * gw:(g + 1) * gw].astype(_BF16), preferred_element_type=_F32)
        mixed = mixed + bs[:, g:g + 1]
        o_ref[:, g * gw:(g + 1) * gw] = (u_ref[:, g * gw:(g + 1) * gw] * mixed).astype(_BF16)


def _gmlp(hb, ln_g, ln_b, ws, bs, want_vn, name):
    m = hb.shape[0]
    w = GMLP_WIDTH
    out_shape = [jax.ShapeDtypeStruct((m, w), _BF16)]
    out_specs = [pl.BlockSpec((CHUNK, w), lambda c: (c, 0))]
    if want_vn:
        out_shape.append(jax.ShapeDtypeStruct((m, w), _F32))
        out_specs.append(pl.BlockSpec((CHUNK, w), lambda c: (c, 0)))
    return pl.pallas_call(
        _gmlp_kernel, grid=(m // CHUNK,),
        in_specs=[pl.BlockSpec((CHUNK, w), lambda c: (c, 0)), pl.BlockSpec((CHUNK, w), lambda c: (c, 1)),
                  pl.BlockSpec((1, w), lambda c: (0, 0)), pl.BlockSpec((1, w), lambda c: (0, 0)),
                  pl.BlockSpec((GMLP_GROUPS, CHUNK, CHUNK), lambda c: (0, 0, 0)),
                  pl.BlockSpec((CHUNK, GMLP_GROUPS), lambda c: (0, 0))],
        out_specs=out_specs, out_shape=out_shape, compiler_params=_params(1), name=name)(
            hb, hb, ln_g.reshape(1, w), ln_b.reshape(1, w), ws, jnp.transpose(bs))


def _head_sums(x, blk):
    outs = []
    for c in range(x.shape[1] // LANES):
        xc = x[:, c * LANES:(c + 1) * LANES]
        hi = xc.astype(_BF16)
        lo = (xc - hi.astype(_F32)).astype(_BF16)
        outs.append(jnp.dot(hi, blk, preferred_element_type=_F32) + jnp.dot(lo, blk, preferred_element_type=_F32))
    return jnp.concatenate(outs, axis=1)


def _rwkv_pre_kernel(pc_ref, sh_ref, prev_ref, mu_ref, w0_ref, w2_ref, a0_ref, a2_ref, kkp_ref, ka_ref,
                     blk_ref, r_ref, w_ref, k_ref, v_ref, kk_ref, b_ref, *, tr, tiles_per_batch, roll_shift):
    pc = pc_ref[...]
    if roll_shift:
        first = pl.program_id(0) % tiles_per_batch == 0
        before = jnp.where(first, prev_ref[0], sh_ref[7:8, :])
        rolled = pltpu.roll(pc, 1, 0)
        rowi = lax.broadcasted_iota(jnp.int32, pc.shape, 0)
        shifted = jnp.where(rowi == 0, before, rolled)
    else:
        shifted = sh_ref[...]
    y = pc + mu_ref[...] * (shifted - pc)
    wdt = RWKV_WIDTH
    r = y[:, 0:wdt]
    k = y[:, wdt:2 * wdt]
    v = y[:, 2 * wdt:3 * wdt]
    wd = y[:, 3 * wdt:3 * wdt + D_DECAY]
    ad = y[:, 3 * wdt + D_DECAY:3 * wdt + D_DECAY + D_AAA]
    z = -(w0_ref[...] + jnp.dot(jnp.tanh(wd).astype(_BF16), w2_ref[...].astype(_BF16),
                                preferred_element_type=_F32))
    softplus = jnp.maximum(z, 0.0) + jnp.log(1.0 + jnp.exp(-jnp.abs(z)))
    decay = jnp.exp(-jnp.exp(-softplus - 0.5))
    a = jax.nn.sigmoid(a0_ref[...] + jnp.dot(ad.astype(_BF16), a2_ref[...].astype(_BF16),
                                             preferred_element_type=_F32))
    kk = k * kkp_ref[...]
    nrm = jnp.maximum(jnp.sqrt(_head_sums(kk * kk, blk_ref[...])), 1e-12)
    kk = kk / nrm
    r_ref[...] = r
    w_ref[...] = decay
    k_ref[...] = k * (1.0 + (a - 1.0) * ka_ref[...])
    v_ref[...] = v
    kk_ref[...] = kk
    b_ref[...] = kk * a


def _seg_blk():
    seg = jnp.arange(LANES) // RWKV_HEAD
    return (seg[:, None] == seg[None, :]).astype(_BF16)


def _rwkv_pre(hc, shift_src, lp, seq, roll_shift, name):
    m = hc.shape[0]
    tr = 256 if (roll_shift and seq % 256 == 0) else (seq if roll_shift else m)
    tpb = seq // tr if roll_shift else 1
    wdt = RWKV_WIDTH
    if roll_shift:
        n_b = m // seq
        sh_spec = pl.BlockSpec((8, C_COLS), lambda i: (jnp.maximum(i * (tr // 8) - 1, 0), 0))
        prev = shift_src.reshape(n_b, 1, C_COLS)
        prev_spec = pl.BlockSpec((1, 1, C_COLS), lambda i: (i // tpb, 0, 0))
        sh_arg = hc
    else:
        sh_spec = pl.BlockSpec((tr, C_COLS), lambda i: (i, 0))
        prev = jnp.zeros((1, 1, C_COLS), _F32)
        prev_spec = pl.BlockSpec((1, 1, C_COLS), lambda i: (0, 0, 0))
        sh_arg = shift_src

    def vec(n):
        return pl.BlockSpec((1, n), lambda i: (0, 0))

    kern = functools.partial(_rwkv_pre_kernel, tr=tr, tiles_per_batch=tpb, roll_shift=roll_shift)
    return pl.pallas_call(
        kern, grid=(m // tr,),
        in_specs=[pl.BlockSpec((tr, C_COLS), lambda i: (i, 0)), sh_spec, prev_spec, vec(C_COLS), vec(wdt),
                  pl.BlockSpec((D_DECAY, wdt), lambda i: (0, 0)), vec(wdt),
                  pl.BlockSpec((D_AAA, wdt), lambda i: (0, 0)), vec(wdt), vec(wdt),
                  pl.BlockSpec((LANES, LANES), lambda i: (0, 0))],
        out_specs=[pl.BlockSpec((tr, wdt), lambda i: (i, 0))] * 6,
        out_shape=[jax.ShapeDtypeStruct((m, wdt), _F32)] * 6,
        compiler_params=_params(1), name=name)(
            hc, sh_arg, prev, lp["rwkv_mu"].reshape(1, C_COLS), lp["rwkv_w0"].reshape(1, wdt), lp["rwkv_w2"],
            lp["rwkv_a0"].reshape(1, wdt), lp["rwkv_a2"], lp["rwkv_kk"].reshape(1, wdt),
            lp["rwkv_ka"].reshape(1, wdt), _seg_blk())


SCAN_BATCH = LANES // RWKV_HEADS


def _rwkv_scan_kernel(*refs, tt, halves):
    cur = refs[0:6]
    nxt = refs[6:12]
    s0_ref, o_ref, st_ref = refs[12:15]
    z = (refs[15:21], refs[21:27])
    zo = refs[27]
    n = RWKV_HEAD
    step_idx = pl.program_id(0)

    def to_lanes(srcs, off, dsts, t):
        for src, dst in zip(srcs, dsts):
            x = jnp.concatenate([src[bb, off + t] for bb in range(SCAN_BATCH)], axis=0)
            dst[t] = x.T

    def from_lanes(t):
        y = zo[t].T
        for bb in range(SCAN_BATCH):
            o_ref[bb, t] = y[bb * RWKV_HEADS:(bb + 1) * RWKV_HEADS]

    def update(zs, t, t_out):
        zr, zw, zk, zv, zkk, zb = zs
        acc = st_ref[0] * zkk[t, pl.ds(0, 1), :]
        for j in range(1, n):
            acc = acc + st_ref[j] * zkk[t, pl.ds(j, 1), :]
        sa = -acc
        vt = zv[t]
        out = None
        for j in range(n):
            sj = st_ref[j] * zw[t, pl.ds(j, 1), :] + sa * zb[t, pl.ds(j, 1), :] + vt * zk[t, pl.ds(j, 1), :]
            st_ref[j] = sj
            term = sj * zr[t, pl.ds(j, 1), :]
            out = term if out is None else out + term
        zo[t_out] = out

    def fill_first():
        def body(t, carry):
            to_lanes(cur, 0, z[0], t)
            return carry
        lax.fori_loop(0, tt, body, 0)

    if halves == 1:
        @pl.when(step_idx == 0)
        def _():
            st_ref[...] = s0_ref[...]

        fill_first()

        def body(t, carry):
            update(z[0], t, t)
            return carry

        lax.fori_loop(0, tt, body, 0)

        def out_body(t, carry):
            from_lanes(t)
            return carry

        lax.fori_loop(0, tt, out_body, 0)
        return

    @pl.when(step_idx == 0)
    def _():
        st_ref[...] = s0_ref[...]
        zo[...] = jnp.zeros(zo.shape, _F32)
        fill_first()

    def first_half(t, carry):
        to_lanes(cur, tt, z[1], t)
        from_lanes(jnp.maximum(t - 1, 0))
        update(z[0], t, t)
        return carry

    lax.fori_loop(0, tt, first_half, 0)

    def second_half(t, carry):
        to_lanes(nxt, 0, z[0], t)
        from_lanes(tt + t - 1)
        update(z[1], t, tt + t)
        return carry

    lax.fori_loop(0, tt, second_half, 0)
    from_lanes(2 * tt - 1)


def _rwkv_scan(vecs, s0, seq, name):
    n = RWKV_HEAD
    halves, tt = (2, 8) if seq % 16 == 0 else (1, seq)
    blk = halves * tt
    n_half = seq // tt
    shape4 = (SCAN_BATCH, seq, RWKV_HEADS, n)
    vspec = pl.BlockSpec((SCAN_BATCH, blk, RWKV_HEADS, n), lambda s: (0, s, 0, 0))
    nspec = pl.BlockSpec((SCAN_BATCH, tt, RWKV_HEADS, n),
                         lambda s: (0, jnp.minimum(halves * (s + 1), n_half - 1), 0, 0))
    sspec = pl.BlockSpec((n, n, LANES), lambda s: (0, 0, 0))
    args = [x.reshape(shape4) for x in vecs]
    o, st = pl.pallas_call(
        functools.partial(_rwkv_scan_kernel, tt=tt, halves=halves), grid=(seq // blk,),
        in_specs=[vspec] * 6 + [nspec] * 6 + [sspec], out_specs=[vspec, sspec],
        out_shape=[jax.ShapeDtypeStruct(shape4, _F32), jax.ShapeDtypeStruct((n, n, LANES), _F32)],
        scratch_shapes=[pltpu.VMEM((tt, n, LANES), _F32)] * 12 + [pltpu.VMEM((blk, n, LANES), _F32)],
        compiler_params=_params(1), name=name)(*args, *args, s0)
    return o.reshape(SCAN_BATCH * seq, RWKV_WIDTH), st


def _state_to_lanes(s):
    return jnp.transpose(s, (3, 2, 0, 1)).reshape(RWKV_HEAD, RWKV_HEAD, LANES)


def _state_from_lanes(s):
    return jnp.transpose(s.reshape(RWKV_HEAD, RWKV_HEAD, SCAN_BATCH, RWKV_HEADS), (2, 3, 1, 0))


def _rwkv_post_kernel(o_ref, r_ref, k_ref, v_ref, g_ref, b_ref, rk_ref, blk_ref, out_ref):
    blk = blk_ref[...]
    o = o_ref[...]
    inv = 1.0 / RWKV_HEAD
    m = _head_sums(o, blk) * inv
    oc = o - m
    var = _head_sums(oc * oc, blk) * inv
    on = oc * lax.rsqrt(var + GN_EPS) * g_ref[...] + b_ref[...]
    bonus = _head_sums(r_ref[...] * k_ref[...] * rk_ref[...], blk) * v_ref[...]
    out_ref[...] = (on + bonus).astype(_BF16)


def _rwkv_post(o, r, k, v, lp, name):
    m = o.shape[0]
    tr = 256 if m % 256 == 0 else m
    wdt = RWKV_WIDTH
    row = pl.BlockSpec((tr, wdt), lambda i: (i, 0))
    vec = pl.BlockSpec((1, wdt), lambda i: (0, 0))
    return pl.pallas_call(
        _rwkv_post_kernel, grid=(m // tr,),
        in_specs=[row, row, row, row, vec, vec, vec, pl.BlockSpec((LANES, LANES), lambda i: (0, 0))],
        out_specs=row, out_shape=jax.ShapeDtypeStruct((m, wdt), _BF16),
        compiler_params=_params(1), name=name)(
            o, r, k, v, lp["rwkv_gn_g"].reshape(1, wdt), lp["rwkv_gn_b"].reshape(1, wdt),
            lp["rwkv_rk"].reshape(1, wdt), _seg_blk())


def _pad_rows(x):
    return jnp.zeros((SAMPLE_ROWS,) + x.shape[1:], x.dtype).at[:x.shape[0]].set(x)


def _layer(xp, xpb, xs, xsb, seq, n_dec, layer, lp, big, cache_k, cache_v, cache_ik, page_table,
           st_shift, st_wkv):
    mp = xp.shape[0]
    n_batch = mp // seq
    tm = 1024 if seq % 1024 == 0 else seq
    d = D_MODEL
    rope_p = _rope_tables(jnp.arange(seq, dtype=jnp.int32))
    rope_s = _rope_tables(jnp.full((SAMPLE_ROWS,), PAST_LEN, jnp.int32))
    rope_extras = [dict(p=a, s=b) for a, b in zip(rope_p, rope_s)]
    lhs_x = [(xpb, xsb)]

    def rope_ep(accs, tex, rex):
        return [_rope_tiles(accs[0], *rex)]

    def pair(w, ncols, wcol, kb=None, kidx=0, lhs=0):
        return dict(lhs=lhs, w=w, layer=layer, kb=w.shape[1] if kb is None else kb, kidx=kidx, ncols=ncols,
                    wcol=wcol)

    def pair_in(ncols, row0):
        n_in = big["w_in_t"].shape[0] // big["depth"]
        return dict(lhs=0, w=big["w_in_t"], layer=layer, kb=d, kidx=0, ncols=ncols,
                    row0=lambda j: pl.multiple_of(layer * n_in + row0(j) + 0 * j, 8))

    tn = 512
    tn_w = 1024
    tm_w = 512 if seq % 512 == 0 else seq
    (q_hm, q_s), = _ws_matmul(
        "proj_q", lhs_x, [pair_in(tn_w, lambda j: j * tn_w)],
        [dict(ncols=tn_w, dtype=_BF16, col=lambda j: j, total=A_Q, hm=True)],
        rope_ep, tm=tm_w, n_steps=A_Q // tn_w, seq=seq, row_extras=rope_extras)
    qi_start = A_Q + 2 * A_KV
    (qi_hm, qi_s), = _ws_matmul(
        "proj_qi", lhs_x, [pair_in(tn_w, lambda j: qi_start + j * tn_w)],
        [dict(ncols=tn_w, dtype=_BF16, col=lambda j: j, total=IDX_HEADS * IDX_DIM, hm=True)],
        rope_ep, tm=tm_w, n_steps=IDX_HEADS * IDX_DIM // tn_w, seq=seq, row_extras=rope_extras)

    ki_start = qi_start + IDX_HEADS * IDX_DIM

    def k_ep(accs, tex, rex):
        kr = _rope_tiles(accs[0], *rex)
        kir = _rope_tiles(accs[1], *rex)
        return [kr, kr, kir, kir]

    zero = lambda j: 0
    (k_f, k_fs), (k_hm, _), (ki_f, ki_fs), (ki_b, _) = _ws_matmul(
        "proj_k", lhs_x,
        [pair_in(A_KV, lambda j: A_Q), pair_in(IDX_DIM, lambda j: ki_start)],
        [dict(ncols=A_KV, dtype=_F32, col=zero, total=A_KV),
         dict(ncols=A_KV, dtype=_BF16, col=zero, total=A_KV, hm=True),
         dict(ncols=IDX_DIM, dtype=_F32, col=zero, total=IDX_DIM),
         dict(ncols=IDX_DIM, dtype=_BF16, col=zero, total=IDX_DIM)],
        k_ep, tm=tm, n_steps=1, seq=seq, row_extras=rope_extras)

    def v_ep(accs, tex, rex):
        return [accs[0], accs[0], accs[1]]

    (v_f, v_fs), (v_hm, _), (wi_f, wi_fs) = _ws_matmul(
        "proj_v", lhs_x,
        [pair_in(A_KV, lambda j: A_Q + A_KV), pair_in(LANES, lambda j: ki_start + IDX_DIM)],
        [dict(ncols=A_KV, dtype=_F32, col=zero, total=A_KV),
         dict(ncols=A_KV, dtype=_BF16, col=zero, total=A_KV, hm=True),
         dict(ncols=LANES, dtype=_F32, col=zero, total=LANES)],
        v_ep, tm=tm, n_steps=1, seq=seq)

    ident = lambda accs, tex, rex: [accs[0]]
    (hb, hb_s), = _ws_matmul(
        "proj_b", lhs_x, [pair_in(tn_w, lambda j: A_COLS + j * tn_w)],
        [dict(ncols=tn_w, dtype=_F32, col=lambda j: j, total=B_COLS)],
        ident, tm=tm_w, n_steps=B_COLS // tn_w, seq=seq)
    tn_c = 640
    (hc, hc_s), = _ws_matmul(
        "proj_c", lhs_x, [pair_in(tn_c, lambda j: A_COLS + B_COLS + j * tn_c)],
        [dict(ncols=tn_c, dtype=_F32, col=lambda j: j, total=C_COLS)],
        ident, tm=tm, n_steps=C_COLS // tn_c, seq=seq)
    (gate, gate_s), = _ws_matmul(
        "proj_g", lhs_x, [pair_in(tn_w, lambda j: A_COLS + B_COLS + C_COLS + j * tn_w)],
        [dict(ncols=tn_w, dtype=_BF16, col=lambda j: j, total=G_COLS)],
        lambda accs, tex, rex: [jax.nn.sigmoid(accs[0])], tm=tm_w, n_steps=G_COLS // tn_w, seq=seq)

    out_a = _attn_prompt(q_hm, qi_hm, ki_b, wi_f, k_hm, v_hm, n_batch, seq)
    out_a_s = _attn_sample(q_s, qi_s, wi_fs, ki_fs, k_fs, v_fs, cache_k, cache_v, cache_ik, page_table, layer)
    out_a_s = _pad_rows(out_a_s.astype(_BF16))

    out_b, = _gmlp(hb, lp["gmlp_ln_g"], lp["gmlp_ln_b"], lp["gmlp_ws"], lp["gmlp_bs"], False, "gmlp_prompt")
    hb_chunks = jnp.zeros((n_dec, CHUNK, B_COLS), _F32).at[:, 0].set(hb_s[:n_dec]).reshape(n_dec * CHUNK, B_COLS)
    ob_s, vn_s = _gmlp(hb_chunks, lp["gmlp_ln_g"], lp["gmlp_ln_b"], lp["gmlp_ws"], lp["gmlp_bs"], True,
                       "gmlp_sample")
    out_b_s = _pad_rows(ob_s.reshape(n_dec, CHUNK, GMLP_WIDTH)[:, 0])
    vn_rows = vn_s.reshape(n_dec, CHUNK, GMLP_WIDTH)[:, 0]

    pre_p = _rwkv_pre(hc, jnp.zeros((n_batch, C_COLS), _F32), lp, seq, True, "rwkv_pre_prompt")
    o_parts, st_parts = [], []
    for b0 in range(0, n_batch, SCAN_BATCH):
        rows = slice(b0 * seq, (b0 + SCAN_BATCH) * seq)
        o_c, st_c = _rwkv_scan([x[rows] for x in pre_p], jnp.zeros((RWKV_HEAD, RWKV_HEAD, LANES), _F32),
                               seq, "rwkv_scan_prompt")
        o_parts.append(o_c)
        st_parts.append(_state_from_lanes(st_c))
    o_p = o_parts[0] if len(o_parts) == 1 else jnp.concatenate(o_parts, axis=0)
    wkv_p = st_parts[0] if len(st_parts) == 1 else jnp.concatenate(st_parts, axis=0)
    out_c = _rwkv_post(o_p, pre_p[0], pre_p[2], pre_p[3], lp, "rwkv_post_prompt")

    pre_s = _rwkv_pre(hc_s, _pad_rows(st_shift), lp, 1, False, "rwkv_pre_sample")
    o_parts, st_parts = [], []
    for b0 in range(0, n_dec, SCAN_BATCH):
        o_c, st_c = _rwkv_scan([x[b0:b0 + SCAN_BATCH] for x in pre_s],
                               _state_to_lanes(st_wkv[b0:b0 + SCAN_BATCH]), 1, "rwkv_scan_sample")
        o_parts.append(o_c)
        st_parts.append(_state_from_lanes(st_c))
    o_s = _pad_rows(jnp.concatenate(o_parts, axis=0))
    wkv_s = jnp.concatenate(st_parts, axis=0)
    out_c_s = _rwkv_post(o_s, pre_s[0], pre_s[2], pre_s[3], lp, "rwkv_post_sample")

    tm_m = 512 if seq % 512 == 0 else seq
    n_g = d // tn

    def merge_ep(accs, tex, rex):
        return [tex[0].astype(_F32) * accs[0] + tex[1].astype(_F32) * accs[1] + tex[2].astype(_F32) * accs[2]]

    (merged, merged_s), = _ws_matmul(
        "merge", [(out_a, out_a_s), (out_b, out_b_s), (out_c, out_c_s)],
        [pair(big["attn_wo"], tn, lambda j: j, lhs=0), pair(big["gmlp_wo"], tn, lambda j: j, lhs=1),
         pair(big["rwkv_wo"], tn, lambda j: j, lhs=2)],
        [dict(ncols=tn, dtype=_BF16, col=lambda j: j, total=d)],
        merge_ep, tm=tm_m, n_steps=d // tn, seq=seq,
        tile_extras=[dict(p=gate, s=gate_s, ncols=tn, col=lambda j, a=a: j + a * n_g) for a in range(3)])

    def resid_ep(accs, tex, rex):
        return [ALPHA * tex[0] + accs[0]]

    (pre1, pre1_s), = _ws_matmul(
        "out_proj", [(merged, merged_s)], [pair(big["w_out"], tn, lambda j: j)],
        [dict(ncols=tn, dtype=_F32, col=lambda j: j, total=d)],
        resid_ep, tm=tm, n_steps=d // tn, seq=seq,
        tile_extras=[dict(p=xp, s=xs, ncols=tn, col=lambda j: j)])
    x1, x1b = _layernorm(pre1, lp["ln1_g"], lp["ln1_b"], "ln1_prompt")
    x1_s, x1b_s = _layernorm(pre1_s, lp["ln1_g"], lp["ln1_b"], "ln1_sample")

    tn_f = 256
    n_f = D_FF // tn_f

    def swiglu_ep(accs, tex, rex):
        return [jax.nn.silu(accs[0]) * accs[1]]

    (act, act_s), = _ws_matmul(
        "ffn_in", [(x1b, x1b_s)],
        [pair(big["ffn_w_in"], tn_f, lambda j: j), pair(big["ffn_w_in"], tn_f, lambda j: j + n_f)],
        [dict(ncols=tn_f, dtype=_BF16, col=lambda j: j, total=D_FF)],
        swiglu_ep, tm=tm, n_steps=n_f, seq=seq)

    kb = D_FF // 2
    part, part_s, scale = x1, x1_s, ALPHA
    for half in range(2):
        def acc_ep(accs, tex, rex, scale=scale):
            return [scale * tex[0] + accs[0]]

        (part, part_s), = _ws_matmul(
            "ffn_out%d" % half, [(act, act_s)],
            [pair(big["ffn_w_out"], tn, lambda j: j, kb=kb, kidx=half)],
            [dict(ncols=tn, dtype=_F32, col=lambda j: j, total=d)],
            acc_ep, tm=tm_m, n_steps=d // tn, seq=seq,
            tile_extras=[dict(p=part, s=part_s, ncols=tn, col=lambda j: j)])
        scale = 1.0
    x2, x2b = _layernorm(part, lp["ln2_g"], lp["ln2_b"], "ln2_prompt")
    x2_s, x2b_s = _layernorm(part_s, lp["ln2_g"], lp["ln2_b"], "ln2_sample")

    caches = dict(
        k_p=k_f.reshape(n_batch, seq, KV_HEADS, HEAD_DIM), v_p=v_f.reshape(n_batch, seq, KV_HEADS, HEAD_DIM),
        ik_p=ki_f.reshape(n_batch, seq, IDX_DIM), wkv_p=wkv_p,
        sh_p=hc.reshape(n_batch, seq, C_COLS)[:, -1],
        k_s=k_fs[:n_dec].reshape(n_dec, 1, KV_HEADS, HEAD_DIM), v_s=v_fs[:n_dec].reshape(n_dec, 1, KV_HEADS, HEAD_DIM),
        ik_s=ki_fs[:n_dec].reshape(n_dec, 1, IDX_DIM), wkv_s=wkv_s, sh_s=hc_s[:n_dec],
        gv_s=vn_rows.reshape(n_dec, 1, GMLP_WIDTH))
    return x2, x2b, x2_s, x2b_s, caches


def kernel(x_prompt, x_sample, cache_k, cache_v, cache_idx_k, page_table, state_wkv, state_shift, w_in, attn_wo, gmlp_ln_g, gmlp_ln_b, gmlp_ws, gmlp_bs, gmlp_wo, rwkv_mu, rwkv_w0, rwkv_w2, rwkv_a0, rwkv_a2, rwkv_kk, rwkv_ka, rwkv_rk, rwkv_gn_g, rwkv_gn_b, rwkv_wo, w_out, ln1_g, ln1_b, ffn_w_in, ffn_w_out, ln2_g, ln2_b):
    n_batch, seq, d = x_prompt.shape
    n_dec = x_sample.shape[0]
    depth = w_in.shape[0]
    assert x_sample.shape[1] == 1 and n_dec <= SAMPLE_ROWS and n_dec % SCAN_BATCH == 0
    assert n_batch % SCAN_BATCH == 0 and d == D_MODEL and seq % CHUNK == 0
    xp = x_prompt.reshape(n_batch * seq, d)
    xs = _pad_rows(x_sample.reshape(n_dec, d))
    xpb, xsb = xp.astype(_BF16), xs.astype(_BF16)
    big = dict(w_in_t=jnp.swapaxes(w_in, 1, 2).reshape(depth * w_in.shape[2], d), depth=depth, attn_wo=attn_wo, gmlp_wo=gmlp_wo, rwkv_wo=rwkv_wo, w_out=w_out,
               ffn_w_in=ffn_w_in, ffn_w_out=ffn_w_out)
    per_layer = []
    for l in range(depth):
        lp = dict(gmlp_ln_g=gmlp_ln_g[l], gmlp_ln_b=gmlp_ln_b[l], gmlp_ws=gmlp_ws[l], gmlp_bs=gmlp_bs[l],
                  rwkv_mu=rwkv_mu[l], rwkv_w0=rwkv_w0[l], rwkv_w2=rwkv_w2[l], rwkv_a0=rwkv_a0[l],
                  rwkv_a2=rwkv_a2[l], rwkv_kk=rwkv_kk[l], rwkv_ka=rwkv_ka[l], rwkv_rk=rwkv_rk[l],
                  rwkv_gn_g=rwkv_gn_g[l], rwkv_gn_b=rwkv_gn_b[l], ln1_g=ln1_g[l], ln1_b=ln1_b[l],
                  ln2_g=ln2_g[l], ln2_b=ln2_b[l])
        xp, xpb, xs, xsb, c = _layer(xp, xpb, xs, xsb, seq, n_dec, l, lp, big, cache_k, cache_v, cache_idx_k,
                                     page_table, state_shift[l], state_wkv[l])
        per_layer.append(c)

    def stack(name):
        return jnp.stack([c[name] for c in per_layer])

    return (xp.reshape(n_batch, seq, d), xs[:n_dec].reshape(n_dec, 1, d),
            stack("k_p"), stack("v_p"), stack("ik_p"), stack("wkv_p"), stack("sh_p"),
            stack("k_s"), stack("v_s"), stack("ik_s"), stack("wkv_s"), stack("sh_s"), stack("gv_s"))
```

```python
import functools

import jax
import jax.numpy as jnp
from jax import lax
from jax.experimental import pallas as pl
from jax.experimental.pallas import tpu as pltpu

D_MODEL = 4096
PAST_LEN = 16384
PAGE_SIZE = 128
N_HEADS = 16
KV_HEADS = 4
HEAD_DIM = 128
ROT_DIM = HEAD_DIM // 4
ROPE_THETA = 500000.0
IDX_HEADS = 32
IDX_DIM = 128
TOPK_MAX = 256
A_Q = N_HEADS * HEAD_DIM
A_KV = KV_HEADS * HEAD_DIM
CHUNK = 128
GMLP_WIDTH = D_MODEL // 2
GMLP_GROUPS = 8
RWKV_HEAD = 64
RWKV_WIDTH = D_MODEL // 2
RWKV_HEADS = RWKV_WIDTH // RWKV_HEAD
D_DECAY = max(32, int(round(D_MODEL ** 0.5 * 1.8 / 32)) * 32)
D_AAA = D_DECAY
D_FF = ((8 * D_MODEL // 3 + 255) // 256) * 256
A_COLS = A_Q + 2 * A_KV + IDX_HEADS * IDX_DIM + IDX_DIM + IDX_HEADS
B_COLS = 2 * GMLP_WIDTH
C_COLS = 3 * RWKV_WIDTH + D_DECAY + D_AAA
G_COLS = 3 * D_MODEL
DEPTH = 2
ALPHA = (2 * DEPTH) ** 0.25
LN_EPS = 1e-5
GN_EPS = 64e-5

LANES = 128
SAMPLE_ROWS = 16
VMEM_LIMIT = 58 * 1024 * 1024
KEY_CHUNK = 512
INT_MIN = -(2 ** 31)
NEG_BIG = -1e30
LOG2E = 1.4426950408889634

_F32 = jnp.float32
_BF16 = jnp.bfloat16
_NT = (((1,), (1,)), ((), ()))


def _params(n_grid):
    return pltpu.CompilerParams(dimension_semantics=("arbitrary",) * n_grid,
                                vmem_limit_bytes=VMEM_LIMIT)


def _ws_matmul(name, lhs, pairs, outs, epilogue, *, tm, n_steps, seq, tile_extras=(), row_extras=()):
    mp = lhs[0][0].shape[0]
    n_i = mp // tm
    tpb = seq // tm
    n_batch = mp // seq

    in_specs, args = [], []
    for a, (xp, xs) in enumerate(lhs):
        kb, kidx = [(p["kb"], p["kidx"]) for p in pairs if p["lhs"] == a][0]
        in_specs += [pl.BlockSpec((tm, kb), lambda j, i, kidx=kidx: (i, kidx)),
                     pl.BlockSpec((SAMPLE_ROWS, kb), lambda j, i, kidx=kidx: (0, kidx))]
        args += [xp, xs]
    for p in pairs:
        if p.get("row0") is not None:
            in_specs.append(pl.BlockSpec((pl.Element(p["ncols"]), pl.Element(p["kb"])),
                                         lambda j, i, p=p: (p["row0"](j), 0)))
        else:
            in_specs.append(pl.BlockSpec((None, p["kb"], p["ncols"]),
                                         lambda j, i, p=p: (p["layer"], p["kidx"], p["wcol"](j))))
        args.append(p["w"])
    for e in tile_extras:
        in_specs += [pl.BlockSpec((tm, e["ncols"]), lambda j, i, e=e: (i, e["col"](j))),
                     pl.BlockSpec((SAMPLE_ROWS, e["ncols"]), lambda j, i, e=e: (0, e["col"](j)))]
        args += [e["p"], e["s"]]
    for e in row_extras:
        in_specs += [pl.BlockSpec((tm, LANES), lambda j, i: (i % tpb, 0)),
                     pl.BlockSpec((SAMPLE_ROWS, LANES), lambda j, i: (0, 0))]
        args += [e["p"], e["s"]]

    out_specs, out_shapes = [], []
    for o in outs:
        if o.get("hm"):
            hpt = o["ncols"] // LANES
            out_shapes.append(jax.ShapeDtypeStruct((n_batch, o["total"] // LANES, seq, LANES), o["dtype"]))
            out_specs.append(pl.BlockSpec((1, hpt, tm, LANES),
                                          lambda j, i, o=o: (i // tpb, o["col"](j), i % tpb, 0)))
        else:
            out_shapes.append(jax.ShapeDtypeStruct((mp, o["total"]), o["dtype"]))
            out_specs.append(pl.BlockSpec((tm, o["ncols"]), lambda j, i, o=o: (i, o["col"](j))))
        out_shapes.append(jax.ShapeDtypeStruct((SAMPLE_ROWS, o["total"]), o["dtype"]))
        out_specs.append(pl.BlockSpec((SAMPLE_ROWS, o["ncols"]), lambda j, i, o=o: (0, o["col"](j))))

    scratch = [pltpu.VMEM((p["kb"], p["ncols"]), _BF16) for p in pairs]
    n_lhs, n_pairs, n_te, n_re, n_out = len(lhs), len(pairs), len(tile_extras), len(row_extras), len(outs)

    def kernel(*refs):
        pos = 0
        x_refs = [(refs[pos + 2 * a], refs[pos + 2 * a + 1]) for a in range(n_lhs)]
        pos += 2 * n_lhs
        w_refs = refs[pos:pos + n_pairs]
        pos += n_pairs
        te_refs = [(refs[pos + 2 * a], refs[pos + 2 * a + 1]) for a in range(n_te)]
        pos += 2 * n_te
        re_refs = [(refs[pos + 2 * a], refs[pos + 2 * a + 1]) for a in range(n_re)]
        pos += 2 * n_re
        o_refs = [(refs[pos + 2 * a], refs[pos + 2 * a + 1]) for a in range(n_out)]
        pos += 2 * n_out
        wb_refs = refs[pos:pos + n_pairs]
        i = pl.program_id(1)

        def run(which):
            accs = [jnp.dot(x_refs[p["lhs"]][which][...], wb[...], preferred_element_type=_F32)
                    for p, wb in zip(pairs, wb_refs)]
            res = epilogue(accs, [t[which][...] for t in te_refs], [r[which][...] for r in re_refs])
            for o, (op_ref, os_ref), val in zip(outs, o_refs, res):
                if which == 0 and o.get("hm"):
                    for h in range(o["ncols"] // LANES):
                        op_ref[0, h] = val[:, h * LANES:(h + 1) * LANES].astype(o["dtype"])
                else:
                    (op_ref if which == 0 else os_ref)[...] = val.astype(o["dtype"])

        @pl.when(i == 0)
        def _():
            for p, w_ref, wb in zip(pairs, w_refs, wb_refs):
                if p.get("row0") is not None:
                    for kc in range(p["kb"] // LANES):
                        blk = w_ref[:, kc * LANES:(kc + 1) * LANES]
                        wb[kc * LANES:(kc + 1) * LANES, :] = blk.T.astype(_BF16)
                    continue
                rows = 256 if p["kb"] % 256 == 0 else 128

                def cast(c, carry, w_ref=w_ref, wb=wb, rows=rows):
                    r0 = pl.multiple_of(c * rows, rows)
                    wb[pl.ds(r0, rows), :] = w_ref[pl.ds(r0, rows), :].astype(_BF16)
                    return carry

                lax.fori_loop(0, p["kb"] // rows, cast, 0)
            run(1)

        run(0)

    res = pl.pallas_call(
        kernel, grid=(n_steps, n_i), in_specs=in_specs, out_specs=out_specs, out_shape=out_shapes,
        scratch_shapes=scratch, compiler_params=_params(2), name=name)(*args)
    return [(res[2 * a], res[2 * a + 1]) for a in range(n_out)]


def _rope_tiles(x, c, s1, s2):
    outs = []
    for h in range(x.shape[1] // LANES):
        xh = x[:, h * LANES:(h + 1) * LANES]
        outs.append(xh * c + pltpu.roll(xh, ROT_DIM // 2, 1) * s1 + pltpu.roll(xh, LANES - ROT_DIM // 2, 1) * s2)
    return outs[0] if len(outs) == 1 else jnp.concatenate(outs, axis=1)


def _rope_tables(pos):
    half = ROT_DIM // 2
    inv = ROPE_THETA ** (-2.0 * jnp.arange(half, dtype=_F32) / ROT_DIM)
    ang = pos.astype(_F32)[:, None] * inv[None, :]
    cos, sin = jnp.cos(ang), jnp.sin(ang)
    n = pos.shape[0]
    c = jnp.concatenate([cos, cos, jnp.ones((n, LANES - ROT_DIM), _F32)], axis=1)
    s1 = jnp.concatenate([jnp.zeros((n, half), _F32), sin, jnp.zeros((n, LANES - ROT_DIM), _F32)], axis=1)
    s2 = jnp.concatenate([-sin, jnp.zeros((n, LANES - half), _F32)], axis=1)
    return c, s1, s2


def _ln_kernel(x_ref, g_ref, b_ref, o_ref, ob_ref):
    x = x_ref[...]
    mu = jnp.mean(x, axis=1, keepdims=True)
    xc = x - mu
    var = jnp.mean(xc * xc, axis=1, keepdims=True)
    y = xc * lax.rsqrt(var + LN_EPS) * g_ref[...] + b_ref[...]
    o_ref[...] = y
    ob_ref[...] = y.astype(_BF16)


def _layernorm(x, g, b, name):
    m, d = x.shape
    tr = 256 if m % 256 == 0 else m
    return pl.pallas_call(
        _ln_kernel, grid=(m // tr,),
        in_specs=[pl.BlockSpec((tr, d), lambda i: (i, 0)), pl.BlockSpec((1, d), lambda i: (0, 0)),
                  pl.BlockSpec((1, d), lambda i: (0, 0))],
        out_specs=[pl.BlockSpec((tr, d), lambda i: (i, 0)), pl.BlockSpec((tr, d), lambda i: (i, 0))],
        out_shape=[jax.ShapeDtypeStruct((m, d), _F32), jax.ShapeDtypeStruct((m, d), _BF16)],
        compiler_params=_params(1), name=name)(x, g.reshape(1, d), b.reshape(1, d))


def _sortable(score):
    bits = pltpu.bitcast(score, jnp.int32)
    return bits ^ ((bits >> 31) & jnp.int32(0x7FFFFFFF))


def _topk_bias(n_ch, load_key, meta, store_bias, k_sel, shape, axes, index_bits, neg):
    kf = float(k_sel)

    def count(pred):
        if axes == (0,):
            def cb(c, part):
                x = jnp.where(pred(c), 1.0, 0.0)
                slabs = [x[i * 8:(i + 1) * 8] for i in range(x.shape[0] // 8)]
                while len(slabs) > 1:
                    slabs = [slabs[i] + slabs[i + 1] for i in range(0, len(slabs), 2)]
                return part + slabs[0]
            part = lax.fori_loop(0, n_ch, cb, jnp.zeros((8, shape[1]), _F32))
            return jnp.sum(part, axis=0, keepdims=True)

        def cb(c, cnt):
            return cnt + jnp.sum(jnp.where(pred(c), 1.0, 0.0), axis=axes, keepdims=True)
        return lax.fori_loop(0, n_ch, cb, jnp.zeros(shape, _F32))

    def bisect(it, t):
        cand = t + lax.shift_left(jnp.int32(1), jnp.int32(31) - it)
        return jnp.where(count(lambda c: load_key(c) >= cand) >= kf, cand, t)

    t = lax.fori_loop(0, 32, bisect, jnp.full(shape, INT_MIN, jnp.int32))
    n_ge = count(lambda c: (load_key(c) >= t) & meta(c)[0])

    def tie_limit():
        need = kf - count(lambda c: load_key(c) > t)

        def body(it, m):
            cand = m + lax.shift_left(jnp.int32(1), jnp.int32(index_bits - 1) - it)
            cnt = count(lambda c: (load_key(c) == t) & meta(c)[0] & (meta(c)[1] < cand))
            return jnp.where(cnt < need, cand, m)

        return lax.fori_loop(0, index_bits, body, jnp.zeros(shape, jnp.int32))

    no_limit = jnp.full(shape, 2 ** 30, jnp.int32)
    m = lax.cond(jnp.max(n_ge) > kf, tie_limit, lambda: no_limit)

    def emit(c, carry):
        keys = load_key(c)
        valid, idx = meta(c)
        store_bias(c, jnp.where((keys > t) | ((keys == t) & valid & (idx <= m)), 0.0, neg))
        return carry

    lax.fori_loop(0, n_ch, emit, 0)


def _attn_prompt_kernel(q_ref, qi_ref, ki_ref, wi_ref, k_ref, v_ref, o_ref,
                        acc_ref, key_ref, bias_ref, wb_ref, l_ref, oacc_ref, sacc_ref, m_ref,
                        *, tq, kc, k_sel, index_bits):
    q0 = pl.program_id(1) * tq
    n_ch = (q0 + tq + kc - 1) // kc
    grp = N_HEADS // KV_HEADS
    wi = wi_ref[...] * ((IDX_HEADS * IDX_DIM) ** -0.5)
    for h in range(IDX_HEADS):
        wb_ref[h] = jnp.broadcast_to(wi[:, h:h + 1], (tq, LANES))

    def meta(c):
        col = c * kc + lax.broadcasted_iota(jnp.int32, (tq, kc), 1)
        row = q0 + lax.broadcasted_iota(jnp.int32, (tq, kc), 0)
        return col <= row, col

    def meta_t(c):
        col = c * kc + lax.broadcasted_iota(jnp.int32, (kc, tq), 0)
        row = q0 + lax.broadcasted_iota(jnp.int32, (kc, tq), 1)
        return col <= row, col

    def score_chunk(c, carry):
        ki = ki_ref[pl.ds(pl.multiple_of(c * kc, kc), kc), :]
        acc_ref[...] = jnp.zeros((tq, kc), _F32)

        def idx_body(hc, carry2):
            qc = qi_ref[0, pl.ds(hc * grp, grp)].reshape(grp * tq, IDX_DIM)
            d = lax.dot_general(qc, ki, _NT, preferred_element_type=_F32)
            part = None
            for hh in range(grp):
                w = jnp.tile(wb_ref[hc * grp + hh], (1, kc // LANES))
                term = jnp.maximum(d[hh * tq:(hh + 1) * tq], 0.0) * w
                part = term if part is None else part + term
            acc_ref[...] += part
            return carry2

        lax.fori_loop(0, IDX_HEADS // grp, idx_body, 0, unroll=2)
        keys = jnp.where(meta(c)[0], _sortable(acc_ref[...]), INT_MIN)
        key_ref[c] = pltpu.bitcast(pltpu.bitcast(keys, _F32).T, jnp.int32)
        return carry

    lax.fori_loop(0, n_ch, score_chunk, 0)

    def store_bias(c, val):
        bias_ref[c] = val.T

    _topk_bias(n_ch, lambda c: key_ref[c], meta_t, store_bias, k_sel, (1, tq), (0,), index_bits, -jnp.inf)

    m_ref[...] = jnp.full(m_ref.shape, -jnp.inf, _F32)
    oacc_ref[...] = jnp.zeros(oacc_ref.shape, _F32)
    sacc_ref[...] = jnp.zeros(sacc_ref.shape, _F32)

    def logits_chunk(c, carry):
        k0 = pl.multiple_of(c * kc, kc)
        bias = jnp.tile(bias_ref[c], (grp, 1))
        for g in range(KV_HEADS):
            qg = q_ref[0, g * grp:(g + 1) * grp].reshape(grp * tq, HEAD_DIM)
            lg = lax.dot_general(qg, k_ref[0, g, pl.ds(k0, kc), :], _NT, preferred_element_type=_F32)
            lg = lg * (HEAD_DIM ** -0.5 * LOG2E) + bias
            l_ref[g, c] = lg
            m_ref[g] = jnp.maximum(m_ref[g], jnp.max(lg, axis=1, keepdims=True))
        return carry

    lax.fori_loop(0, n_ch, logits_chunk, 0)

    def pv_chunk(c, carry):
        k0 = pl.multiple_of(c * kc, kc)
        for g in range(KV_HEADS):
            p = jnp.exp2(l_ref[g, c] - m_ref[g])
            sacc_ref[g] += jnp.sum(p, axis=1, keepdims=True)
            oacc_ref[g] += jnp.dot(p.astype(_BF16), v_ref[0, g, pl.ds(k0, kc), :], preferred_element_type=_F32)
        return carry

    lax.fori_loop(0, n_ch, pv_chunk, 0)
    for g in range(KV_HEADS):
        out = oacc_ref[g] / sacc_ref[g]
        for hh in range(grp):
            h = g * grp + hh
            o_ref[:, h * HEAD_DIM:(h + 1) * HEAD_DIM] = out[hh * tq:(hh + 1) * tq].astype(_BF16)


def _attn_prompt(q_hm, qi_hm, ki_b, wi, k_hm, v_hm, n_batch, seq):
    tq = 128
    kc = KEY_CHUNK if seq % KEY_CHUNK == 0 else seq
    nqb = seq // tq
    grp = N_HEADS // KV_HEADS
    k_sel = min(TOPK_MAX, seq // 4)
    kern = functools.partial(_attn_prompt_kernel, tq=tq, kc=kc, k_sel=k_sel,
                             index_bits=(seq - 1).bit_length() + 1)
    return pl.pallas_call(
        kern, grid=(n_batch, nqb),
        in_specs=[
            pl.BlockSpec((1, N_HEADS, tq, HEAD_DIM), lambda b, q: (b, 0, q, 0)),
            pl.BlockSpec((1, IDX_HEADS, tq, IDX_DIM), lambda b, q: (b, 0, q, 0)),
            pl.BlockSpec((seq, IDX_DIM), lambda b, q: (b, 0)),
            pl.BlockSpec((tq, LANES), lambda b, q: (b * nqb + q, 0)),
            pl.BlockSpec((1, KV_HEADS, seq, HEAD_DIM), lambda b, q: (b, 0, 0, 0)),
            pl.BlockSpec((1, KV_HEADS, seq, HEAD_DIM), lambda b, q: (b, 0, 0, 0)),
        ],
        out_specs=pl.BlockSpec((tq, A_Q), lambda b, q: (b * nqb + q, 0)),
        out_shape=jax.ShapeDtypeStruct((n_batch * seq, A_Q), _BF16),
        scratch_shapes=[pltpu.VMEM((tq, kc), _F32), pltpu.VMEM((seq // kc, kc, tq), jnp.int32),
                        pltpu.VMEM((seq // kc, tq, kc), _F32), pltpu.VMEM((IDX_HEADS, tq, LANES), _F32),
                        pltpu.VMEM((KV_HEADS, seq // kc, grp * tq, kc), _F32),
                        pltpu.VMEM((KV_HEADS, grp * tq, HEAD_DIM), _F32),
                        pltpu.VMEM((KV_HEADS, grp * tq, 1), _F32), pltpu.VMEM((KV_HEADS, grp * tq, 1), _F32)],
        compiler_params=_params(2), name="attn_prompt")(q_hm, qi_hm, ki_b, wi, k_hm, v_hm)


SCORE_PAGES_PER_STEP = 32
ATTN_PAGES_PER_STEP = 16


def _attn_sample_score_kernel(pt_ref, qi_ref, wi_ref, kin_ref, exp_ref, *rest, n_pages, k_sel, pps):
    pages = rest[:pps]
    bias_ref, score_ref, key_ref, sel_ref = rest[pps:]
    del pt_ref
    s = pl.program_id(1)
    n_rows = n_pages + 8
    qi = qi_ref[0]
    wcol = wi_ref[0] * ((IDX_HEADS * IDX_DIM) ** -0.5)
    for r in range(pps):
        page = pages[r][0].astype(_BF16)
        d = lax.dot_general(qi, page, _NT, preferred_element_type=_F32)
        sc = jnp.sum(jnp.maximum(d, 0.0) * wcol, axis=0, keepdims=True)
        score_ref[pl.ds(s * pps + r, 1), :] = sc

    @pl.when(s == pl.num_programs(1) - 1)
    def _():
        kin = kin_ref[0].astype(_BF16).astype(_F32)
        dn = jnp.sum(qi.astype(_F32) * kin, axis=1, keepdims=True)
        sn = jnp.sum(jnp.maximum(dn, 0.0) * wcol, axis=0, keepdims=True)
        score_ref[pl.ds(n_pages, 8), :] = jnp.broadcast_to(sn, (8, LANES))
        rowi = lax.broadcasted_iota(jnp.int32, (n_rows, LANES), 0)
        lane = lax.broadcasted_iota(jnp.int32, (n_rows, LANES), 1)
        valid = (rowi < n_pages) | ((rowi == n_pages) & (lane == 0))
        pos = rowi * LANES + lane
        key_ref[...] = jnp.where(valid, _sortable(score_ref[...]), INT_MIN)

        def store_sel(c, val):
            sel_ref[...] = val

        _topk_bias(1, lambda c: key_ref[...], lambda c: (valid, pos), store_sel, k_sel, (1, 1), (0, 1),
                   (n_rows * LANES - 1).bit_length() + 1, 1.0)
        rep = jnp.dot(sel_ref[...].astype(_BF16), exp_ref[...], preferred_element_type=_F32)
        bias_ref[0] = jnp.where(rep < 0.5, 0.0, NEG_BIG)


def _attn_sample_kernel(pt_ref, q_ref, kn_ref, vn_ref, bias_ref, *rest, n_pages, pps):
    kpages = rest[:pps]
    vpages = rest[pps:2 * pps]
    o_ref, m_ref, s_ref, acc_ref = rest[2 * pps:]
    del pt_ref
    st = pl.program_id(1)
    scale = HEAD_DIM ** -0.5
    grp = N_HEADS // KV_HEADS
    n_cols = PAGE_SIZE * KV_HEADS
    q = q_ref[0]
    own = (lax.broadcasted_iota(jnp.int32, (N_HEADS, n_cols), 1) % KV_HEADS
           == lax.broadcasted_iota(jnp.int32, (N_HEADS, n_cols), 0) // grp)
    head_bias = jnp.where(own, 0.0, NEG_BIG)

    @pl.when(st == 0)
    def _():
        m_ref[...] = jnp.full(m_ref.shape, NEG_BIG, _F32)
        s_ref[...] = jnp.zeros(s_ref.shape, _F32)
        acc_ref[...] = jnp.zeros(acc_ref.shape, _F32)

    def update(logit_list, pv_fns):
        m_old = m_ref[...]
        m_new = m_old
        for lg in logit_list:
            m_new = jnp.maximum(m_new, jnp.max(lg, axis=1, keepdims=True))
        alpha = jnp.exp(m_old - m_new)
        s_new = s_ref[...] * alpha
        acc = acc_ref[...] * alpha
        for lg, pv_fn in zip(logit_list, pv_fns):
            p = jnp.exp(lg - m_new)
            s_new = s_new + jnp.sum(p, axis=1, keepdims=True)
            acc = acc + pv_fn(p)
        s_ref[...] = s_new
        acc_ref[...] = acc
        m_ref[...] = m_new

    logit_list, pv_fns = [], []
    for r in range(pps):
        kp = kpages[r][0].astype(_BF16)
        logits = lax.dot_general(q, kp, _NT, preferred_element_type=_F32) * scale
        logit_list.append(logits + head_bias + bias_ref[0, pl.ds(st * pps + r, 1), :])
        pv_fns.append(lambda p, r=r: jnp.dot(p.astype(_BF16), vpages[r][0].astype(_BF16),
                                             preferred_element_type=_F32))
    update(logit_list, pv_fns)

    @pl.when(st == pl.num_programs(1) - 1)
    def _():
        kn = kn_ref[0].astype(_BF16).astype(_F32)
        vn = vn_ref[0].astype(_BF16).astype(_F32)
        ln = jnp.sum(q.astype(_F32) * kn, axis=1, keepdims=True) * scale
        ln = ln + bias_ref[0, pl.ds(n_pages, 1), :][:, 0:1]
        update([ln], [lambda p: p * vn])
        o_ref[0] = acc_ref[...] / s_ref[...]


def _attn_sample(q_s, qi_s, wi_s, ki_s, k_s, v_s, cache_k, cache_v, cache_ik, page_table, layer):
    n_b, n_pages = page_table.shape
    depth, n_phys = cache_ik.shape[:2]
    k_sel = min(TOPK_MAX, (n_pages * PAGE_SIZE + 1) // 4)
    pps_s = min(SCORE_PAGES_PER_STEP, n_pages)
    pps_a = min(ATTN_PAGES_PER_STEP, n_pages)
    assert n_pages % pps_s == 0 and n_pages % pps_a == 0
    n_rows = n_pages + 8
    n_cols = PAGE_SIZE * KV_HEADS
    pt = page_table.reshape(-1)
    grp = N_HEADS // KV_HEADS

    def page_map(r, pps):
        return lambda b, s, pt_ref: (layer, pt_ref[b * n_pages + s * pps + r], 0, 0)

    qi3 = qi_s[:n_b].reshape(n_b, IDX_HEADS, IDX_DIM)
    wi3 = wi_s[:n_b, :IDX_HEADS].reshape(n_b, IDX_HEADS, 1)
    kin3 = ki_s[:n_b].reshape(n_b, 1, IDX_DIM)
    expand = (jnp.arange(LANES)[:, None] == jnp.arange(n_cols)[None, :] // KV_HEADS).astype(_BF16)
    bias = pl.pallas_call(
        functools.partial(_attn_sample_score_kernel, n_pages=n_pages, k_sel=k_sel, pps=pps_s),
        grid_spec=pltpu.PrefetchScalarGridSpec(
            num_scalar_prefetch=1, grid=(n_b, n_pages // pps_s),
            in_specs=[pl.BlockSpec((1, IDX_HEADS, IDX_DIM), lambda b, s, pt_ref: (b, 0, 0)),
                      pl.BlockSpec((1, IDX_HEADS, 1), lambda b, s, pt_ref: (b, 0, 0)),
                      pl.BlockSpec((1, 1, IDX_DIM), lambda b, s, pt_ref: (b, 0, 0)),
                      pl.BlockSpec((LANES, n_cols), lambda b, s, pt_ref: (0, 0))]
            + [pl.BlockSpec((None, 1, PAGE_SIZE, IDX_DIM), page_map(r, pps_s)) for r in range(pps_s)],
            out_specs=pl.BlockSpec((1, n_rows, n_cols), lambda b, s, pt_ref: (b, 0, 0)),
            scratch_shapes=[pltpu.VMEM((n_rows, LANES), _F32), pltpu.VMEM((n_rows, LANES), jnp.int32),
                            pltpu.VMEM((n_rows, LANES), _F32)]),
        out_shape=jax.ShapeDtypeStruct((n_b, n_rows, n_cols), _F32),
        compiler_params=_params(2), name="attn_sample_score")(
            pt, qi3, wi3, kin3, expand, *([cache_ik] * pps_s))

    pk = cache_k.reshape(depth, n_phys, n_cols, HEAD_DIM)
    pv = cache_v.reshape(depth, n_phys, n_cols, HEAD_DIM)
    q3 = q_s[:n_b].reshape(n_b, N_HEADS, HEAD_DIM)
    kn = jnp.repeat(k_s[:n_b].reshape(n_b, KV_HEADS, HEAD_DIM), grp, axis=1)
    vn = jnp.repeat(v_s[:n_b].reshape(n_b, KV_HEADS, HEAD_DIM), grp, axis=1)
    head_spec = pl.BlockSpec((1, N_HEADS, HEAD_DIM), lambda b, s, pt_ref: (b, 0, 0))
    out = pl.pallas_call(
        functools.partial(_attn_sample_kernel, n_pages=n_pages, pps=pps_a),
        grid_spec=pltpu.PrefetchScalarGridSpec(
            num_scalar_prefetch=1, grid=(n_b, n_pages // pps_a),
            in_specs=[head_spec, head_spec, head_spec,
                      pl.BlockSpec((1, n_rows, n_cols), lambda b, s, pt_ref: (b, 0, 0))]
            + [pl.BlockSpec((None, 1, n_cols, HEAD_DIM), page_map(r, pps_a)) for r in range(pps_a)]
            + [pl.BlockSpec((None, 1, n_cols, HEAD_DIM), page_map(r, pps_a)) for r in range(pps_a)],
            out_specs=head_spec,
            scratch_shapes=[pltpu.VMEM((N_HEADS, 1), _F32), pltpu.VMEM((N_HEADS, 1), _F32),
                            pltpu.VMEM((N_HEADS, HEAD_DIM), _F32)]),
        out_shape=jax.ShapeDtypeStruct((n_b, N_HEADS, HEAD_DIM), _F32),
        compiler_params=_params(2), name="attn_sample")(
            pt, q3, kn, vn, bias, *([pk] * pps_a), *([pv] * pps_a))
    return out.reshape(n_b, A_Q)


def _gmlp_kernel(u_ref, v_ref, g_ref, b_ref, ws_ref, bs_ref, o_ref, vn_ref=None):
    v = v_ref[...]
    mu = jnp.mean(v, axis=1, keepdims=True)
    vc = v - mu
    var = jnp.mean(vc * vc, axis=1, keepdims=True)
    vn = vc * lax.rsqrt(var + LN_EPS) * g_ref[...] + b_ref[...]
    if vn_ref is not None:
        vn_ref[...] = vn
    gw = GMLP_WIDTH // GMLP_GROUPS
    tril = (lax.broadcasted_iota(jnp.int32, (CHUNK, CHUNK), 0)
            >= lax.broadcasted_iota(jnp.int32, (CHUNK, CHUNK), 1))
    bs = bs_ref[...]
    for g in range(GMLP_GROUPS):
        wm = jnp.where(tril, ws_ref[g], 0.0).astype(_BF16)
        mixed = jnp.dot(wm, vn[:, g * gw:(g + 1) * gw].astype(_BF16), preferred_element_type=_F32)
        mixed = mixed + bs[:, g:g + 1]
        o_ref[:, g * gw:(g + 1) * gw] = (u_ref[:, g * gw:(g + 1) * gw] * mixed).astype(_BF16)


def _gmlp(hb, ln_g, ln_b, ws, bs, want_vn, name):
    m = hb.shape[0]
    w = GMLP_WIDTH
    out_shape = [jax.ShapeDtypeStruct((m, w), _BF16)]
    out_specs = [pl.BlockSpec((CHUNK, w), lambda c: (c, 0))]
    if want_vn:
        out_shape.append(jax.ShapeDtypeStruct((m, w), _F32))
        out_specs.append(pl.BlockSpec((CHUNK, w), lambda c: (c, 0)))
    return pl.pallas_call(
        _gmlp_kernel, grid=(m // CHUNK,),
        in_specs=[pl.BlockSpec((CHUNK, w), lambda c: (c, 0)), pl.BlockSpec((CHUNK, w), lambda c: (c, 1)),
                  pl.BlockSpec((1, w), lambda c: (0, 0)), pl.BlockSpec((1, w), lambda c: (0, 0)),
                  pl.BlockSpec((GMLP_GROUPS, CHUNK, CHUNK), lambda c: (0, 0, 0)),
                  pl.BlockSpec((CHUNK, GMLP_GROUPS), lambda c: (0, 0))],
        out_specs=out_specs, out_shape=out_shape, compiler_params=_params(1), name=name)(
            hb, hb, ln_g.reshape(1, w), ln_b.reshape(1, w), ws, jnp.transpose(bs))


def _head_sums(x, blk):
    outs = []
    for c in range(x.shape[1] // LANES):
        xc = x[:, c * LANES:(c + 1) * LANES]
        hi = xc.astype(_BF16)
        lo = (xc - hi.astype(_F32)).astype(_BF16)
        outs.append(jnp.dot(hi, blk, preferred_element_type=_F32) + jnp.dot(lo, blk, preferred_element_type=_F32))
    return jnp.concatenate(outs, axis=1)


def _rwkv_pre_kernel(pc_ref, sh_ref, prev_ref, mu_ref, w0_ref, w2_ref, a0_ref, a2_ref,
                     r_ref, w_ref, k_ref, v_ref, a_ref, *, tr, tiles_per_batch, roll_shift):
    pc = pc_ref[...]
    if roll_shift:
        first = pl.program_id(0) % tiles_per_batch == 0
        before = jnp.where(first, prev_ref[0], sh_ref[7:8, :])
        rolled = pltpu.roll(pc, 1, 0)
        rowi = lax.broadcasted_iota(jnp.int32, pc.shape, 0)
        shifted = jnp.where(rowi == 0, before, rolled)
    else:
        shifted = sh_ref[...]
    y = pc + mu_ref[...] * (shifted - pc)
    wdt = RWKV_WIDTH
    r = y[:, 0:wdt]
    k = y[:, wdt:2 * wdt]
    v = y[:, 2 * wdt:3 * wdt]
    wd = y[:, 3 * wdt:3 * wdt + D_DECAY]
    ad = y[:, 3 * wdt + D_DECAY:3 * wdt + D_DECAY + D_AAA]
    z = -(w0_ref[...] + jnp.dot(jnp.tanh(wd).astype(_BF16), w2_ref[...].astype(_BF16),
                                preferred_element_type=_F32))
    softplus = jnp.maximum(z, 0.0) + jnp.log(1.0 + jnp.exp(-jnp.abs(z)))
    decay = jnp.exp(-jnp.exp(-softplus - 0.5))
    a = jax.nn.sigmoid(a0_ref[...] + jnp.dot(ad.astype(_BF16), a2_ref[...].astype(_BF16),
                                             preferred_element_type=_F32))
    r_ref[...] = r
    w_ref[...] = decay
    k_ref[...] = k
    v_ref[...] = v
    a_ref[...] = a


def _seg_blk():
    seg = jnp.arange(LANES) // RWKV_HEAD
    return (seg[:, None] == seg[None, :]).astype(_BF16)


def _rwkv_pre(hc, shift_src, lp, seq, roll_shift, name):
    m = hc.shape[0]
    tr = 256 if (roll_shift and seq % 256 == 0) else (seq if roll_shift else m)
    tpb = seq // tr if roll_shift else 1
    wdt = RWKV_WIDTH
    if roll_shift:
        n_b = m // seq
        sh_spec = pl.BlockSpec((8, C_COLS), lambda i: (jnp.maximum(i * (tr // 8) - 1, 0), 0))
        prev = shift_src.reshape(n_b, 1, C_COLS)
        prev_spec = pl.BlockSpec((1, 1, C_COLS), lambda i: (i // tpb, 0, 0))
        sh_arg = hc
    else:
        sh_spec = pl.BlockSpec((tr, C_COLS), lambda i: (i, 0))
        prev = jnp.zeros((1, 1, C_COLS), _F32)
        prev_spec = pl.BlockSpec((1, 1, C_COLS), lambda i: (0, 0, 0))
        sh_arg = shift_src

    def vec(n):
        return pl.BlockSpec((1, n), lambda i: (0, 0))

    kern = functools.partial(_rwkv_pre_kernel, tr=tr, tiles_per_batch=tpb, roll_shift=roll_shift)
    return pl.pallas_call(
        kern, grid=(m // tr,),
        in_specs=[pl.BlockSpec((tr, C_COLS), lambda i: (i, 0)), sh_spec, prev_spec, vec(C_COLS), vec(wdt),
                  pl.BlockSpec((D_DECAY, wdt), lambda i: (0, 0)), vec(wdt),
                  pl.BlockSpec((D_AAA, wdt), lambda i: (0, 0))],
        out_specs=[pl.BlockSpec((tr, wdt), lambda i: (i, 0))] * 5,
        out_shape=[jax.ShapeDtypeStruct((m, wdt), _F32)] * 5,
        compiler_params=_params(1), name=name)(
            hc, sh_arg, prev, lp["rwkv_mu"].reshape(1, C_COLS), lp["rwkv_w0"].reshape(1, wdt), lp["rwkv_w2"],
            lp["rwkv_a0"].reshape(1, wdt), lp["rwkv_a2"])


SCAN_BATCH = LANES // RWKV_HEADS


def _rwkv_scan_kernel(*refs, tt, halves):
    cur = refs[0:5]
    nxt = refs[5:10]
    kkp_ref, ka_ref, s0_ref, o_ref, st_ref = refs[10:15]
    z = (refs[15:21], refs[21:27])
    zo = refs[27]
    n = RWKV_HEAD
    step_idx = pl.program_id(0)

    def to_lanes(srcs, off, dsts, t):
        def lanes(src):
            x = jnp.concatenate([src[bb, off + t] for bb in range(SCAN_BATCH)], axis=0)
            return x.T
        r, w, k, v, a = [lanes(src) for src in srcs]
        kk = k * kkp_ref[...]
        kk = kk / jnp.maximum(jnp.sqrt(jnp.sum(kk * kk, axis=0, keepdims=True)), 1e-12)
        zr, zw, zk, zv, zkk, zb = dsts
        zr[t] = r
        zw[t] = w
        zk[t] = k * (1.0 + (a - 1.0) * ka_ref[...])
        zv[t] = v
        zkk[t] = kk
        zb[t] = kk * a

    def from_lanes(t):
        y = zo[t].T
        for bb in range(SCAN_BATCH):
            o_ref[bb, t] = y[bb * RWKV_HEADS:(bb + 1) * RWKV_HEADS]

    def update(zs, t, t_out):
        zr, zw, zk, zv, zkk, zb = zs
        acc = st_ref[0] * zkk[t, pl.ds(0, 1), :]
        for j in range(1, n):
            acc = acc + st_ref[j] * zkk[t, pl.ds(j, 1), :]
        sa = -acc
        vt = zv[t]
        out = None
        for j in range(n):
            sj = st_ref[j] * zw[t, pl.ds(j, 1), :] + sa * zb[t, pl.ds(j, 1), :] + vt * zk[t, pl.ds(j, 1), :]
            st_ref[j] = sj
            term = sj * zr[t, pl.ds(j, 1), :]
            out = term if out is None else out + term
        zo[t_out] = out

    def fill_first():
        def body(t, carry):
            to_lanes(cur, 0, z[0], t)
            return carry
        lax.fori_loop(0, tt, body, 0)

    if halves == 1:
        @pl.when(step_idx == 0)
        def _():
            st_ref[...] = s0_ref[...]

        fill_first()

        def body(t, carry):
            update(z[0], t, t)
            return carry

        lax.fori_loop(0, tt, body, 0)

        def out_body(t, carry):
            from_lanes(t)
            return carry

        lax.fori_loop(0, tt, out_body, 0)
        return

    @pl.when(step_idx == 0)
    def _():
        st_ref[...] = s0_ref[...]
        zo[...] = jnp.zeros(zo.shape, _F32)
        fill_first()

    def first_half(t, carry):
        to_lanes(cur, tt, z[1], t)
        from_lanes(jnp.maximum(t - 1, 0))
        update(z[0], t, t)
        return carry

    lax.fori_loop(0, tt, first_half, 0)

    def second_half(t, carry):
        to_lanes(nxt, 0, z[0], t)
        from_lanes(tt + t - 1)
        update(z[1], t, tt + t)
        return carry

    lax.fori_loop(0, tt, second_half, 0)
    from_lanes(2 * tt - 1)


def _head_param_to_lanes(p):
    return jnp.tile(jnp.transpose(p.reshape(RWKV_HEADS, RWKV_HEAD)), (1, SCAN_BATCH))


def _rwkv_scan(vecs, s0, lp, seq, name):
    n = RWKV_HEAD
    halves, tt = (2, 8) if seq % 16 == 0 else (1, seq)
    blk = halves * tt
    n_half = seq // tt
    shape4 = (SCAN_BATCH, seq, RWKV_HEADS, n)
    vspec = pl.BlockSpec((SCAN_BATCH, blk, RWKV_HEADS, n), lambda s: (0, s, 0, 0))
    nspec = pl.BlockSpec((SCAN_BATCH, tt, RWKV_HEADS, n),
                         lambda s: (0, jnp.minimum(halves * (s + 1), n_half - 1), 0, 0))
    sspec = pl.BlockSpec((n, n, LANES), lambda s: (0, 0, 0))
    pspec = pl.BlockSpec((n, LANES), lambda s: (0, 0))
    args = [x.reshape(shape4) for x in vecs]
    o, st = pl.pallas_call(
        functools.partial(_rwkv_scan_kernel, tt=tt, halves=halves), grid=(seq // blk,),
        in_specs=[vspec] * 5 + [nspec] * 5 + [pspec, pspec, sspec], out_specs=[vspec, sspec],
        out_shape=[jax.ShapeDtypeStruct(shape4, _F32), jax.ShapeDtypeStruct((n, n, LANES), _F32)],
        scratch_shapes=[pltpu.VMEM((tt, n, LANES), _F32)] * 12 + [pltpu.VMEM((blk, n, LANES), _F32)],
        compiler_params=_params(1), name=name)(
            *args, *args, _head_param_to_lanes(lp["rwkv_kk"]), _head_param_to_lanes(lp["rwkv_ka"]), s0)
    return o.reshape(SCAN_BATCH * seq, RWKV_WIDTH), st


def _state_to_lanes(s):
    return jnp.transpose(s, (3, 2, 0, 1)).reshape(RWKV_HEAD, RWKV_HEAD, LANES)


def _state_from_lanes(s):
    return jnp.transpose(s.reshape(RWKV_HEAD, RWKV_HEAD, SCAN_BATCH, RWKV_HEADS), (2, 3, 1, 0))


def _rwkv_post_kernel(o_ref, r_ref, k_ref, v_ref, a_ref, g_ref, b_ref, rk_ref, ka_ref, blk_ref, out_ref):
    blk = blk_ref[...]
    o = o_ref[...]
    inv = 1.0 / RWKV_HEAD
    m = _head_sums(o, blk) * inv
    oc = o - m
    var = _head_sums(oc * oc, blk) * inv
    on = oc * lax.rsqrt(var + GN_EPS) * g_ref[...] + b_ref[...]
    k = k_ref[...] * (1.0 + (a_ref[...] - 1.0) * ka_ref[...])
    bonus = _head_sums(r_ref[...] * k * rk_ref[...], blk) * v_ref[...]
    out_ref[...] = (on + bonus).astype(_BF16)


def _rwkv_post(o, r, k, v, a, lp, name):
    m = o.shape[0]
    tr = 256 if m % 256 == 0 else m
    wdt = RWKV_WIDTH
    row = pl.BlockSpec((tr, wdt), lambda i: (i, 0))
    vec = pl.BlockSpec((1, wdt), lambda i: (0, 0))
    return pl.pallas_call(
        _rwkv_post_kernel, grid=(m // tr,),
        in_specs=[row] * 5 + [vec] * 4 + [pl.BlockSpec((LANES, LANES), lambda i: (0, 0))],
        out_specs=row, out_shape=jax.ShapeDtypeStruct((m, wdt), _BF16),
        compiler_params=_params(1), name=name)(
            o, r, k, v, a, lp["rwkv_gn_g"].reshape(1, wdt), lp["rwkv_gn_b"].reshape(1, wdt),
            lp["rwkv_rk"].reshape(1, wdt), lp["rwkv_ka"].reshape(1, wdt), _seg_blk())


def _pad_rows(x):
    return jnp.zeros((SAMPLE_ROWS,) + x.shape[1:], x.dtype).at[:x.shape[0]].set(x)


def _layer(xp, xpb, xs, xsb, seq, n_dec, layer, lp, big, cache_k, cache_v, cache_ik, page_table,
           st_shift, st_wkv):
    mp = xp.shape[0]
    n_batch = mp // seq
    tm = 1024 if seq % 1024 == 0 else seq
    d = D_MODEL
    rope_p = _rope_tables(jnp.arange(seq, dtype=jnp.int32))
    rope_s = _rope_tables(jnp.full((SAMPLE_ROWS,), PAST_LEN, jnp.int32))
    rope_extras = [dict(p=a, s=b) for a, b in zip(rope_p, rope_s)]
    lhs_x = [(xpb, xsb)]

    def rope_ep(accs, tex, rex):
        return [_rope_tiles(accs[0], *rex)]

    def pair(w, ncols, wcol, kb=None, kidx=0, lhs=0):
        return dict(lhs=lhs, w=w, layer=layer, kb=w.shape[1] if kb is None else kb, kidx=kidx, ncols=ncols,
                    wcol=wcol)

    def pair_in(ncols, row0):
        n_in = big["w_in_t"].shape[0] // big["depth"]
        return dict(lhs=0, w=big["w_in_t"], layer=layer, kb=d, kidx=0, ncols=ncols,
                    row0=lambda j: pl.multiple_of(layer * n_in + row0(j) + 0 * j, 8))

    tn = 512
    tn_w = 1024
    tm_w = 512 if seq % 512 == 0 else seq
    (q_hm, q_s), = _ws_matmul(
        "proj_q", lhs_x, [pair_in(tn_w, lambda j: j * tn_w)],
        [dict(ncols=tn_w, dtype=_BF16, col=lambda j: j, total=A_Q, hm=True)],
        rope_ep, tm=tm_w, n_steps=A_Q // tn_w, seq=seq, row_extras=rope_extras)
    qi_start = A_Q + 2 * A_KV
    (qi_hm, qi_s), = _ws_matmul(
        "proj_qi", lhs_x, [pair_in(tn_w, lambda j: qi_start + j * tn_w)],
        [dict(ncols=tn_w, dtype=_BF16, col=lambda j: j, total=IDX_HEADS * IDX_DIM, hm=True)],
        rope_ep, tm=tm_w, n_steps=IDX_HEADS * IDX_DIM // tn_w, seq=seq, row_extras=rope_extras)

    ki_start = qi_start + IDX_HEADS * IDX_DIM

    def k_ep(accs, tex, rex):
        kr = _rope_tiles(accs[0], *rex)
        kir = _rope_tiles(accs[1], *rex)
        return [kr, kr, kir, kir]

    zero = lambda j: 0
    (k_f, k_fs), (k_hm, _), (ki_f, ki_fs), (ki_b, _) = _ws_matmul(
        "proj_k", lhs_x,
        [pair_in(A_KV, lambda j: A_Q), pair_in(IDX_DIM, lambda j: ki_start)],
        [dict(ncols=A_KV, dtype=_F32, col=zero, total=A_KV),
         dict(ncols=A_KV, dtype=_BF16, col=zero, total=A_KV, hm=True),
         dict(ncols=IDX_DIM, dtype=_F32, col=zero, total=IDX_DIM),
         dict(ncols=IDX_DIM, dtype=_BF16, col=zero, total=IDX_DIM)],
        k_ep, tm=tm, n_steps=1, seq=seq, row_extras=rope_extras)

    def v_ep(accs, tex, rex):
        return [accs[0], accs[0], accs[1]]

    (v_f, v_fs), (v_hm, _), (wi_f, wi_fs) = _ws_matmul(
        "proj_v", lhs_x,
        [pair_in(A_KV, lambda j: A_Q + A_KV), pair_in(LANES, lambda j: ki_start + IDX_DIM)],
        [dict(ncols=A_KV, dtype=_F32, col=zero, total=A_KV),
         dict(ncols=A_KV, dtype=_BF16, col=zero, total=A_KV, hm=True),
         dict(ncols=LANES, dtype=_F32, col=zero, total=LANES)],
        v_ep, tm=tm, n_steps=1, seq=seq)

    ident = lambda accs, tex, rex: [accs[0]]
    (hb, hb_s), = _ws_matmul(
        "proj_b", lhs_x, [pair_in(tn_w, lambda j: A_COLS + j * tn_w)],
        [dict(ncols=tn_w, dtype=_F32, col=lambda j: j, total=B_COLS)],
        ident, tm=tm_w, n_steps=B_COLS // tn_w, seq=seq)
    tn_c = 640
    (hc, hc_s), = _ws_matmul(
        "proj_c", lhs_x, [pair_in(tn_c, lambda j: A_COLS + B_COLS + j * tn_c)],
        [dict(ncols=tn_c, dtype=_F32, col=lambda j: j, total=C_COLS)],
        ident, tm=tm, n_steps=C_COLS // tn_c, seq=seq)
    (gate, gate_s), = _ws_matmul(
        "proj_g", lhs_x, [pair_in(tn_w, lambda j: A_COLS + B_COLS + C_COLS + j * tn_w)],
        [dict(ncols=tn_w, dtype=_BF16, col=lambda j: j, total=G_COLS)],
        lambda accs, tex, rex: [jax.nn.sigmoid(accs[0])], tm=tm_w, n_steps=G_COLS // tn_w, seq=seq)

    out_a = _attn_prompt(q_hm, qi_hm, ki_b, wi_f, k_hm, v_hm, n_batch, seq)
    out_a_s = _attn_sample(q_s, qi_s, wi_fs, ki_fs, k_fs, v_fs, cache_k, cache_v, cache_ik, page_table, layer)
    out_a_s = _pad_rows(out_a_s.astype(_BF16))

    out_b, = _gmlp(hb, lp["gmlp_ln_g"], lp["gmlp_ln_b"], lp["gmlp_ws"], lp["gmlp_bs"], False, "gmlp_prompt")
    hb_chunks = jnp.zeros((n_dec, CHUNK, B_COLS), _F32).at[:, 0].set(hb_s[:n_dec]).reshape(n_dec * CHUNK, B_COLS)
    ob_s, vn_s = _gmlp(hb_chunks, lp["gmlp_ln_g"], lp["gmlp_ln_b"], lp["gmlp_ws"], lp["gmlp_bs"], True,
                       "gmlp_sample")
    out_b_s = _pad_rows(ob_s.reshape(n_dec, CHUNK, GMLP_WIDTH)[:, 0])
    vn_rows = vn_s.reshape(n_dec, CHUNK, GMLP_WIDTH)[:, 0]

    pre_p = _rwkv_pre(hc, jnp.zeros((n_batch, C_COLS), _F32), lp, seq, True, "rwkv_pre_prompt")
    o_parts, st_parts = [], []
    for b0 in range(0, n_batch, SCAN_BATCH):
        rows = slice(b0 * seq, (b0 + SCAN_BATCH) * seq)
        o_c, st_c = _rwkv_scan([x[rows] for x in pre_p], jnp.zeros((RWKV_HEAD, RWKV_HEAD, LANES), _F32),
                               lp, seq, "rwkv_scan_prompt")
        o_parts.append(o_c)
        st_parts.append(_state_from_lanes(st_c))
    o_p = o_parts[0] if len(o_parts) == 1 else jnp.concatenate(o_parts, axis=0)
    wkv_p = st_parts[0] if len(st_parts) == 1 else jnp.concatenate(st_parts, axis=0)
    out_c = _rwkv_post(o_p, pre_p[0], pre_p[2], pre_p[3], pre_p[4], lp, "rwkv_post_prompt")

    pre_s = _rwkv_pre(hc_s, _pad_rows(st_shift), lp, 1, False, "rwkv_pre_sample")
    o_parts, st_parts = [], []
    for b0 in range(0, n_dec, SCAN_BATCH):
        o_c, st_c = _rwkv_scan([x[b0:b0 + SCAN_BATCH] for x in pre_s],
                               _state_to_lanes(st_wkv[b0:b0 + SCAN_BATCH]), lp, 1, "rwkv_scan_sample")
        o_parts.append(o_c)
        st_parts.append(_state_from_lanes(st_c))
    o_s = _pad_rows(jnp.concatenate(o_parts, axis=0))
    wkv_s = jnp.concatenate(st_parts, axis=0)
    out_c_s = _rwkv_post(o_s, pre_s[0], pre_s[2], pre_s[3], pre_s[4], lp, "rwkv_post_sample")

    tm_m = 512 if seq % 512 == 0 else seq
    n_g = d // tn

    def merge_ep(accs, tex, rex):
        return [tex[0].astype(_F32) * accs[0] + tex[1].astype(_F32) * accs[1] + tex[2].astype(_F32) * accs[2]]

    (merged, merged_s), = _ws_matmul(
        "merge", [(out_a, out_a_s), (out_b, out_b_s), (out_c, out_c_s)],
        [pair(big["attn_wo"], tn, lambda j: j, lhs=0), pair(big["gmlp_wo"], tn, lambda j: j, lhs=1),
         pair(big["rwkv_wo"], tn, lambda j: j, lhs=2)],
        [dict(ncols=tn, dtype=_BF16, col=lambda j: j, total=d)],
        merge_ep, tm=tm_m, n_steps=d // tn, seq=seq,
        tile_extras=[dict(p=gate, s=gate_s, ncols=tn, col=lambda j, a=a: j + a * n_g) for a in range(3)])

    def resid_ep(accs, tex, rex):
        return [ALPHA * tex[0] + accs[0]]

    (pre1, pre1_s), = _ws_matmul(
        "out_proj", [(merged, merged_s)], [pair(big["w_out"], tn, lambda j: j)],
        [dict(ncols=tn, dtype=_F32, col=lambda j: j, total=d)],
        resid_ep, tm=tm, n_steps=d // tn, seq=seq,
        tile_extras=[dict(p=xp, s=xs, ncols=tn, col=lambda j: j)])
    x1, x1b = _layernorm(pre1, lp["ln1_g"], lp["ln1_b"], "ln1_prompt")
    x1_s, x1b_s = _layernorm(pre1_s, lp["ln1_g"], lp["ln1_b"], "ln1_sample")

    tn_f = 256
    n_f = D_FF // tn_f

    def swiglu_ep(accs, tex, rex):
        return [jax.nn.silu(accs[0]) * accs[1]]

    (act, act_s), = _ws_matmul(
        "ffn_in", [(x1b, x1b_s)],
        [pair(big["ffn_w_in"], tn_f, lambda j: j), pair(big["ffn_w_in"], tn_f, lambda j: j + n_f)],
        [dict(ncols=tn_f, dtype=_BF16, col=lambda j: j, total=D_FF)],
        swiglu_ep, tm=tm, n_steps=n_f, seq=seq)

    kb = D_FF // 2
    part, part_s, scale = x1, x1_s, ALPHA
    for half in range(2):
        def acc_ep(accs, tex, rex, scale=scale):
            return [scale * tex[0] + accs[0]]

        (part, part_s), = _ws_matmul(
            "ffn_out%d" % half, [(act, act_s)],
            [pair(big["ffn_w_out"], tn, lambda j: j, kb=kb, kidx=half)],
            [dict(ncols=tn, dtype=_F32, col=lambda j: j, total=d)],
            acc_ep, tm=tm_m, n_steps=d // tn, seq=seq,
            tile_extras=[dict(p=part, s=part_s, ncols=tn, col=lambda j: j)])
        scale = 1.0
    x2, x2b = _layernorm(part, lp["ln2_g"], lp["ln2_b"], "ln2_prompt")
    x2_s, x2b_s = _layernorm(part_s, lp["ln2_g"], lp["ln2_b"], "ln2_sample")

    caches = dict(
        k_p=k_f.reshape(n_batch, seq, KV_HEADS, HEAD_DIM), v_p=v_f.reshape(n_batch, seq, KV_HEADS, HEAD_DIM),
        ik_p=ki_f.reshape(n_batch, seq, IDX_DIM), wkv_p=wkv_p,
        sh_p=hc.reshape(n_batch, seq, C_COLS)[:, -1],
        k_s=k_fs[:n_dec].reshape(n_dec, 1, KV_HEADS, HEAD_DIM), v_s=v_fs[:n_dec].reshape(n_dec, 1, KV_HEADS, HEAD_DIM),
        ik_s=ki_fs[:n_dec].reshape(n_dec, 1, IDX_DIM), wkv_s=wkv_s, sh_s=hc_s[:n_dec],
        gv_s=vn_rows.reshape(n_dec, 1, GMLP_WIDTH))
    return x2, x2b, x2_s, x2b_s, caches


def kernel(x_prompt, x_sample, cache_k, cache_v, cache_idx_k, page_table, state_wkv, state_shift, w_in, attn_wo, gmlp_ln_g, gmlp_ln_b, gmlp_ws, gmlp_bs, gmlp_wo, rwkv_mu, rwkv_w0, rwkv_w2, rwkv_a0, rwkv_a2, rwkv_kk, rwkv_ka, rwkv_rk, rwkv_gn_g, rwkv_gn_b, rwkv_wo, w_out, ln1_g, ln1_b, ffn_w_in, ffn_w_out, ln2_g, ln2_b):
    n_batch, seq, d = x_prompt.shape
    n_dec = x_sample.shape[0]
    depth = w_in.shape[0]
    assert x_sample.shape[1] == 1 and n_dec <= SAMPLE_ROWS and n_dec % SCAN_BATCH == 0
    assert n_batch % SCAN_BATCH == 0 and d == D_MODEL and seq % CHUNK == 0
    xp = x_prompt.reshape(n_batch * seq, d)
    xs = _pad_rows(x_sample.reshape(n_dec, d))
    xpb, xsb = xp.astype(_BF16), xs.astype(_BF16)
    big = dict(w_in_t=jnp.swapaxes(w_in, 1, 2).reshape(depth * w_in.shape[2], d), depth=depth, attn_wo=attn_wo, gmlp_wo=gmlp_wo, rwkv_wo=rwkv_wo, w_out=w_out,
               ffn_w_in=ffn_w_in, ffn_w_out=ffn_w_out)
    per_layer = []
    for l in range(depth):
        lp = dict(gmlp_ln_g=gmlp_ln_g[l], gmlp_ln_b=gmlp_ln_b[l], gmlp_ws=gmlp_ws[l], gmlp_bs=gmlp_bs[l],
                  rwkv_mu=rwkv_mu[l], rwkv_w0=rwkv_w0[l], rwkv_w2=rwkv_w2[l], rwkv_a0=rwkv_a0[l],
                  rwkv_a2=rwkv_a2[l], rwkv_kk=rwkv_kk[l], rwkv_ka=rwkv_ka[l], rwkv_rk=rwkv_rk[l],
                  rwkv_gn_g=rwkv_gn_g[l], rwkv_gn_b=rwkv_gn_b[l], ln1_g=ln1_g[l], ln1_b=ln1_b[l],
                  ln2_g=ln2_g[l], ln2_b=ln2_b[l])
        xp, xpb, xs, xsb, c = _layer(xp, xpb, xs, xsb, seq, n_dec, l, lp, big, cache_k, cache_v, cache_idx_k,
                                     page_table, state_shift[l], state_wkv[l])
        per_layer.append(c)

    def stack(name):
        return jnp.stack([c[name] for c in per_layer])

    return (xp.reshape(n_batch, seq, d), xs[:n_dec].reshape(n_dec, 1, d),
            stack("k_p"), stack("v_p"), stack("ik_p"), stack("wkv_p"), stack("sh_p"),
            stack("k_s"), stack("v_s"), stack("ik_s"), stack("wkv_s"), stack("sh_s"), stack("gv_s"))
```

```python
import functools

import jax
import jax.numpy as jnp
from jax import lax
from jax.experimental import pallas as pl
from jax.experimental.pallas import tpu as pltpu

D_MODEL = 4096
PAST_LEN = 16384
PAGE_SIZE = 128
N_HEADS = 16
KV_HEADS = 4
HEAD_DIM = 128
ROT_DIM = HEAD_DIM // 4
ROPE_THETA = 500000.0
IDX_HEADS = 32
IDX_DIM = 128
TOPK_MAX = 256
A_Q = N_HEADS * HEAD_DIM
A_KV = KV_HEADS * HEAD_DIM
CHUNK = 128
GMLP_WIDTH = D_MODEL // 2
GMLP_GROUPS = 8
RWKV_HEAD = 64
RWKV_WIDTH = D_MODEL // 2
RWKV_HEADS = RWKV_WIDTH // RWKV_HEAD
D_DECAY = max(32, int(round(D_MODEL ** 0.5 * 1.8 / 32)) * 32)
D_AAA = D_DECAY
D_FF = ((8 * D_MODEL // 3 + 255) // 256) * 256
A_COLS = A_Q + 2 * A_KV + IDX_HEADS * IDX_DIM + IDX_DIM + IDX_HEADS
B_COLS = 2 * GMLP_WIDTH
C_COLS = 3 * RWKV_WIDTH + D_DECAY + D_AAA
G_COLS = 3 * D_MODEL
DEPTH = 2
ALPHA = (2 * DEPTH) ** 0.25
LN_EPS = 1e-5
GN_EPS = 64e-5

LANES = 128
SAMPLE_ROWS = 16
VMEM_LIMIT = 58 * 1024 * 1024
KEY_CHUNK = 512
INT_MIN = -(2 ** 31)
NEG_BIG = -1e30
LOG2E = 1.4426950408889634

_F32 = jnp.float32
_BF16 = jnp.bfloat16
_NT = (((1,), (1,)), ((), ()))


def _params(n_grid):
    return pltpu.CompilerParams(dimension_semantics=("arbitrary",) * n_grid,
                                vmem_limit_bytes=VMEM_LIMIT)


def _ws_matmul(name, lhs, pairs, outs, epilogue, *, tm, n_steps, seq, tile_extras=(), row_extras=()):
    mp = lhs[0][0].shape[0]
    n_i = mp // tm
    tpb = seq // tm
    n_batch = mp // seq

    in_specs, args = [], []
    for a, (xp, xs) in enumerate(lhs):
        kb, kidx = [(p["kb"], p["kidx"]) for p in pairs if p["lhs"] == a][0]
        in_specs += [pl.BlockSpec((tm, kb), lambda j, i, kidx=kidx: (i, kidx)),
                     pl.BlockSpec((SAMPLE_ROWS, kb), lambda j, i, kidx=kidx: (0, kidx))]
        args += [xp, xs]
    for p in pairs:
        if p.get("row0") is not None:
            in_specs.append(pl.BlockSpec((pl.Element(p["ncols"]), pl.Element(p["kb"])),
                                         lambda j, i, p=p: (p["row0"](j), 0)))
        else:
            in_specs.append(pl.BlockSpec((None, p["kb"], p["ncols"]),
                                         lambda j, i, p=p: (p["layer"], p["kidx"], p["wcol"](j))))
        args.append(p["w"])
    for e in tile_extras:
        in_specs += [pl.BlockSpec((tm, e["ncols"]), lambda j, i, e=e: (i, e["col"](j))),
                     pl.BlockSpec((SAMPLE_ROWS, e["ncols"]), lambda j, i, e=e: (0, e["col"](j)))]
        args += [e["p"], e["s"]]
    for e in row_extras:
        in_specs += [pl.BlockSpec((tm, LANES), lambda j, i: (i % tpb, 0)),
                     pl.BlockSpec((SAMPLE_ROWS, LANES), lambda j, i: (0, 0))]
        args += [e["p"], e["s"]]

    out_specs, out_shapes = [], []
    for o in outs:
        if o.get("hm"):
            hpt = o["ncols"] // LANES
            out_shapes.append(jax.ShapeDtypeStruct((n_batch, o["total"] // LANES, seq, LANES), o["dtype"]))
            out_specs.append(pl.BlockSpec((1, hpt, tm, LANES),
                                          lambda j, i, o=o: (i // tpb, o["col"](j), i % tpb, 0)))
        else:
            out_shapes.append(jax.ShapeDtypeStruct((mp, o["total"]), o["dtype"]))
            out_specs.append(pl.BlockSpec((tm, o["ncols"]), lambda j, i, o=o: (i, o["col"](j))))
        out_shapes.append(jax.ShapeDtypeStruct((SAMPLE_ROWS, o["total"]), o["dtype"]))
        out_specs.append(pl.BlockSpec((SAMPLE_ROWS, o["ncols"]), lambda j, i, o=o: (0, o["col"](j))))

    scratch = [pltpu.VMEM((p["kb"], p["ncols"]), _BF16) for p in pairs]
    n_lhs, n_pairs, n_te, n_re, n_out = len(lhs), len(pairs), len(tile_extras), len(row_extras), len(outs)

    def kernel(*refs):
        pos = 0
        x_refs = [(refs[pos + 2 * a], refs[pos + 2 * a + 1]) for a in range(n_lhs)]
        pos += 2 * n_lhs
        w_refs = refs[pos:pos + n_pairs]
        pos += n_pairs
        te_refs = [(refs[pos + 2 * a], refs[pos + 2 * a + 1]) for a in range(n_te)]
        pos += 2 * n_te
        re_refs = [(refs[pos + 2 * a], refs[pos + 2 * a + 1]) for a in range(n_re)]
        pos += 2 * n_re
        o_refs = [(refs[pos + 2 * a], refs[pos + 2 * a + 1]) for a in range(n_out)]
        pos += 2 * n_out
        wb_refs = refs[pos:pos + n_pairs]
        i = pl.program_id(1)

        def run(which):
            accs = [jnp.dot(x_refs[p["lhs"]][which][...], wb[...], preferred_element_type=_F32)
                    for p, wb in zip(pairs, wb_refs)]
            res = epilogue(accs, [t[which][...] for t in te_refs], [r[which][...] for r in re_refs])
            for o, (op_ref, os_ref), val in zip(outs, o_refs, res):
                if which == 0 and o.get("hm"):
                    for h in range(o["ncols"] // LANES):
                        op_ref[0, h] = val[:, h * LANES:(h + 1) * LANES].astype(o["dtype"])
                else:
                    (op_ref if which == 0 else os_ref)[...] = val.astype(o["dtype"])

        @pl.when(i == 0)
        def _():
            for p, w_ref, wb in zip(pairs, w_refs, wb_refs):
                if p.get("row0") is not None:
                    for kc in range(p["kb"] // LANES):
                        blk = w_ref[:, kc * LANES:(kc + 1) * LANES]
                        wb[kc * LANES:(kc + 1) * LANES, :] = blk.T.astype(_BF16)
                    continue
                rows = 256 if p["kb"] % 256 == 0 else 128

                def cast(c, carry, w_ref=w_ref, wb=wb, rows=rows):
                    r0 = pl.multiple_of(c * rows, rows)
                    wb[pl.ds(r0, rows), :] = w_ref[pl.ds(r0, rows), :].astype(_BF16)
                    return carry

                lax.fori_loop(0, p["kb"] // rows, cast, 0)
            run(1)

        run(0)

    res = pl.pallas_call(
        kernel, grid=(n_steps, n_i), in_specs=in_specs, out_specs=out_specs, out_shape=out_shapes,
        scratch_shapes=scratch, compiler_params=_params(2), name=name)(*args)
    return [(res[2 * a], res[2 * a + 1]) for a in range(n_out)]


def _rope_tiles(x, c, s1, s2):
    outs = []
    for h in range(x.shape[1] // LANES):
        xh = x[:, h * LANES:(h + 1) * LANES]
        outs.append(xh * c + pltpu.roll(xh, ROT_DIM // 2, 1) * s1 + pltpu.roll(xh, LANES - ROT_DIM // 2, 1) * s2)
    return outs[0] if len(outs) == 1 else jnp.concatenate(outs, axis=1)


def _rope_tables(pos):
    half = ROT_DIM // 2
    inv = ROPE_THETA ** (-2.0 * jnp.arange(half, dtype=_F32) / ROT_DIM)
    ang = pos.astype(_F32)[:, None] * inv[None, :]
    cos, sin = jnp.cos(ang), jnp.sin(ang)
    n = pos.shape[0]
    c = jnp.concatenate([cos, cos, jnp.ones((n, LANES - ROT_DIM), _F32)], axis=1)
    s1 = jnp.concatenate([jnp.zeros((n, half), _F32), sin, jnp.zeros((n, LANES - ROT_DIM), _F32)], axis=1)
    s2 = jnp.concatenate([-sin, jnp.zeros((n, LANES - half), _F32)], axis=1)
    return c, s1, s2


def _ln_kernel(x_ref, g_ref, b_ref, o_ref, ob_ref):
    x = x_ref[...]
    mu = jnp.mean(x, axis=1, keepdims=True)
    xc = x - mu
    var = jnp.mean(xc * xc, axis=1, keepdims=True)
    y = xc * lax.rsqrt(var + LN_EPS) * g_ref[...] + b_ref[...]
    o_ref[...] = y
    ob_ref[...] = y.astype(_BF16)


def _layernorm(x, g, b, name):
    m, d = x.shape
    tr = 256 if m % 256 == 0 else m
    return pl.pallas_call(
        _ln_kernel, grid=(m // tr,),
        in_specs=[pl.BlockSpec((tr, d), lambda i: (i, 0)), pl.BlockSpec((1, d), lambda i: (0, 0)),
                  pl.BlockSpec((1, d), lambda i: (0, 0))],
        out_specs=[pl.BlockSpec((tr, d), lambda i: (i, 0)), pl.BlockSpec((tr, d), lambda i: (i, 0))],
        out_shape=[jax.ShapeDtypeStruct((m, d), _F32), jax.ShapeDtypeStruct((m, d), _BF16)],
        compiler_params=_params(1), name=name)(x, g.reshape(1, d), b.reshape(1, d))


def _sortable(score):
    bits = pltpu.bitcast(score, jnp.int32)
    return bits ^ ((bits >> 31) & jnp.int32(0x7FFFFFFF))


def _topk_bias(n_ch, load_key, meta, store_bias, k_sel, shape, axes, index_bits, neg):
    kf = float(k_sel)

    def count(pred):
        if axes == (0,):
            def cb(c, part):
                x = jnp.where(pred(c), 1.0, 0.0)
                slabs = [x[i * 8:(i + 1) * 8] for i in range(x.shape[0] // 8)]
                while len(slabs) > 1:
                    slabs = [slabs[i] + slabs[i + 1] for i in range(0, len(slabs), 2)]
                return part + slabs[0]
            part = lax.fori_loop(0, n_ch, cb, jnp.zeros((8, shape[1]), _F32))
            return jnp.sum(part, axis=0, keepdims=True)

        def cb(c, cnt):
            return cnt + jnp.sum(jnp.where(pred(c), 1.0, 0.0), axis=axes, keepdims=True)
        return lax.fori_loop(0, n_ch, cb, jnp.zeros(shape, _F32))

    def bisect(it, t):
        cand = t + lax.shift_left(jnp.int32(1), jnp.int32(31) - it)
        return jnp.where(count(lambda c: load_key(c) >= cand) >= kf, cand, t)

    t = lax.fori_loop(0, 32, bisect, jnp.full(shape, INT_MIN, jnp.int32))
    n_ge = count(lambda c: (load_key(c) >= t) & meta(c)[0])

    def tie_limit():
        need = kf - count(lambda c: load_key(c) > t)

        def body(it, m):
            cand = m + lax.shift_left(jnp.int32(1), jnp.int32(index_bits - 1) - it)
            cnt = count(lambda c: (load_key(c) == t) & meta(c)[0] & (meta(c)[1] < cand))
            return jnp.where(cnt < need, cand, m)

        return lax.fori_loop(0, index_bits, body, jnp.zeros(shape, jnp.int32))

    no_limit = jnp.full(shape, 2 ** 30, jnp.int32)
    m = lax.cond(jnp.max(n_ge) > kf, tie_limit, lambda: no_limit)

    def emit(c, carry):
        keys = load_key(c)
        valid, idx = meta(c)
        store_bias(c, jnp.where((keys > t) | ((keys == t) & valid & (idx <= m)), 0.0, neg))
        return carry

    lax.fori_loop(0, n_ch, emit, 0)


def _attn_prompt_kernel(q_ref, qi_ref, ki_ref, wi_ref, k_ref, v_ref, o_ref,
                        acc_ref, key_ref, bias_ref, wb_ref, l_ref, oacc_ref, sacc_ref, m_ref,
                        *, tq, kc, k_sel, index_bits):
    q0 = pl.program_id(1) * tq
    n_ch = (q0 + tq + kc - 1) // kc
    grp = N_HEADS // KV_HEADS
    wi = wi_ref[...] * ((IDX_HEADS * IDX_DIM) ** -0.5)
    for h in range(IDX_HEADS):
        wb_ref[h] = jnp.broadcast_to(wi[:, h:h + 1], (tq, LANES))

    def meta(c):
        col = c * kc + lax.broadcasted_iota(jnp.int32, (tq, kc), 1)
        row = q0 + lax.broadcasted_iota(jnp.int32, (tq, kc), 0)
        return col <= row, col

    def meta_t(c):
        col = c * kc + lax.broadcasted_iota(jnp.int32, (kc, tq), 0)
        row = q0 + lax.broadcasted_iota(jnp.int32, (kc, tq), 1)
        return col <= row, col

    def score_chunk(c, carry):
        ki = ki_ref[pl.ds(pl.multiple_of(c * kc, kc), kc), :]
        acc_ref[...] = jnp.zeros((tq, kc), _F32)

        def idx_body(hc, carry2):
            qc = qi_ref[0, pl.ds(hc * grp, grp)].reshape(grp * tq, IDX_DIM)
            d = lax.dot_general(qc, ki, _NT, preferred_element_type=_F32)
            part = None
            for hh in range(grp):
                w = jnp.tile(wb_ref[hc * grp + hh], (1, kc // LANES))
                term = jnp.maximum(d[hh * tq:(hh + 1) * tq], 0.0) * w
                part = term if part is None else part + term
            acc_ref[...] += part
            return carry2

        lax.fori_loop(0, IDX_HEADS // grp, idx_body, 0, unroll=2)
        keys = jnp.where(meta(c)[0], _sortable(acc_ref[...]), INT_MIN)
        key_ref[c] = pltpu.bitcast(pltpu.bitcast(keys, _F32).T, jnp.int32)
        return carry

    lax.fori_loop(0, n_ch, score_chunk, 0)

    def store_bias(c, val):
        bias_ref[c] = val.T

    _topk_bias(n_ch, lambda c: key_ref[c], meta_t, store_bias, k_sel, (1, tq), (0,), index_bits, -jnp.inf)

    m_ref[...] = jnp.full(m_ref.shape, -jnp.inf, _F32)
    oacc_ref[...] = jnp.zeros(oacc_ref.shape, _F32)
    sacc_ref[...] = jnp.zeros(sacc_ref.shape, _F32)

    def logits_chunk(c, carry):
        k0 = pl.multiple_of(c * kc, kc)
        bias = jnp.tile(bias_ref[c], (grp, 1))
        for g in range(KV_HEADS):
            qg = q_ref[0, g * grp:(g + 1) * grp].reshape(grp * tq, HEAD_DIM)
            lg = lax.dot_general(qg, k_ref[0, g, pl.ds(k0, kc), :], _NT, preferred_element_type=_F32)
            lg = lg * (HEAD_DIM ** -0.5 * LOG2E) + bias
            l_ref[g, c] = lg
            m_ref[g] = jnp.maximum(m_ref[g], jnp.max(lg, axis=1, keepdims=True))
        return carry

    lax.fori_loop(0, n_ch, logits_chunk, 0)

    def pv_chunk(c, carry):
        k0 = pl.multiple_of(c * kc, kc)
        for g in range(KV_HEADS):
            p = jnp.exp2(l_ref[g, c] - m_ref[g])
            sacc_ref[g] += jnp.sum(p, axis=1, keepdims=True)
            oacc_ref[g] += jnp.dot(p.astype(_BF16), v_ref[0, g, pl.ds(k0, kc), :], preferred_element_type=_F32)
        return carry

    lax.fori_loop(0, n_ch, pv_chunk, 0)
    for g in range(KV_HEADS):
        out = oacc_ref[g] / sacc_ref[g]
        for hh in range(grp):
            h = g * grp + hh
            o_ref[:, h * HEAD_DIM:(h + 1) * HEAD_DIM] = out[hh * tq:(hh + 1) * tq].astype(_BF16)


def _attn_prompt(q_hm, qi_hm, ki_b, wi, k_hm, v_hm, n_batch, seq):
    tq = 128
    kc = KEY_CHUNK if seq % KEY_CHUNK == 0 else seq
    nqb = seq // tq
    grp = N_HEADS // KV_HEADS
    k_sel = min(TOPK_MAX, seq // 4)
    kern = functools.partial(_attn_prompt_kernel, tq=tq, kc=kc, k_sel=k_sel,
                             index_bits=(seq - 1).bit_length() + 1)
    return pl.pallas_call(
        kern, grid=(n_batch, nqb),
        in_specs=[
            pl.BlockSpec((1, N_HEADS, tq, HEAD_DIM), lambda b, q: (b, 0, q, 0)),
            pl.BlockSpec((1, IDX_HEADS, tq, IDX_DIM), lambda b, q: (b, 0, q, 0)),
            pl.BlockSpec((seq, IDX_DIM), lambda b, q: (b, 0)),
            pl.BlockSpec((tq, LANES), lambda b, q: (b * nqb + q, 0)),
            pl.BlockSpec((1, KV_HEADS, seq, HEAD_DIM), lambda b, q: (b, 0, 0, 0)),
            pl.BlockSpec((1, KV_HEADS, seq, HEAD_DIM), lambda b, q: (b, 0, 0, 0)),
        ],
        out_specs=pl.BlockSpec((tq, A_Q), lambda b, q: (b * nqb + q, 0)),
        out_shape=jax.ShapeDtypeStruct((n_batch * seq, A_Q), _BF16),
        scratch_shapes=[pltpu.VMEM((tq, kc), _F32), pltpu.VMEM((seq // kc, kc, tq), jnp.int32),
                        pltpu.VMEM((seq // kc, tq, kc), _F32), pltpu.VMEM((IDX_HEADS, tq, LANES), _F32),
                        pltpu.VMEM((KV_HEADS, seq // kc, grp * tq, kc), _F32),
                        pltpu.VMEM((KV_HEADS, grp * tq, HEAD_DIM), _F32),
                        pltpu.VMEM((KV_HEADS, grp * tq, 1), _F32), pltpu.VMEM((KV_HEADS, grp * tq, 1), _F32)],
        compiler_params=_params(2), name="attn_prompt")(q_hm, qi_hm, ki_b, wi, k_hm, v_hm)


SCORE_PAGES_PER_STEP = 32
ATTN_PAGES_PER_STEP = 16


def _attn_sample_score_kernel(pt_ref, qi_ref, wi_ref, kin_ref, exp_ref, *rest, n_pages, k_sel, pps):
    pages = rest[:pps]
    bias_ref, score_ref, key_ref, sel_ref = rest[pps:]
    del pt_ref
    s = pl.program_id(1)
    n_rows = n_pages + 8
    qi = qi_ref[0]
    wcol = wi_ref[0] * ((IDX_HEADS * IDX_DIM) ** -0.5)
    for r in range(pps):
        page = pages[r][0].astype(_BF16)
        d = lax.dot_general(qi, page, _NT, preferred_element_type=_F32)
        sc = jnp.sum(jnp.maximum(d, 0.0) * wcol, axis=0, keepdims=True)
        score_ref[pl.ds(s * pps + r, 1), :] = sc

    @pl.when(s == pl.num_programs(1) - 1)
    def _():
        kin = kin_ref[0].astype(_BF16).astype(_F32)
        dn = jnp.sum(qi.astype(_F32) * kin, axis=1, keepdims=True)
        sn = jnp.sum(jnp.maximum(dn, 0.0) * wcol, axis=0, keepdims=True)
        score_ref[pl.ds(n_pages, 8), :] = jnp.broadcast_to(sn, (8, LANES))
        rowi = lax.broadcasted_iota(jnp.int32, (n_rows, LANES), 0)
        lane = lax.broadcasted_iota(jnp.int32, (n_rows, LANES), 1)
        valid = (rowi < n_pages) | ((rowi == n_pages) & (lane == 0))
        pos = rowi * LANES + lane
        key_ref[...] = jnp.where(valid, _sortable(score_ref[...]), INT_MIN)

        def store_sel(c, val):
            sel_ref[...] = val

        _topk_bias(1, lambda c: key_ref[...], lambda c: (valid, pos), store_sel, k_sel, (1, 1), (0, 1),
                   (n_rows * LANES - 1).bit_length() + 1, 1.0)
        rep = jnp.dot(sel_ref[...].astype(_BF16), exp_ref[...], preferred_element_type=_F32)
        bias_ref[0] = jnp.where(rep < 0.5, 0.0, NEG_BIG)


def _attn_sample_kernel(pt_ref, q_ref, kn_ref, vn_ref, bias_ref, *rest, n_pages, pps):
    kpages = rest[:pps]
    vpages = rest[pps:2 * pps]
    o_ref, m_ref, s_ref, acc_ref = rest[2 * pps:]
    del pt_ref
    st = pl.program_id(1)
    scale = HEAD_DIM ** -0.5
    grp = N_HEADS // KV_HEADS
    n_cols = PAGE_SIZE * KV_HEADS
    q = q_ref[0]
    own = (lax.broadcasted_iota(jnp.int32, (N_HEADS, n_cols), 1) % KV_HEADS
           == lax.broadcasted_iota(jnp.int32, (N_HEADS, n_cols), 0) // grp)
    head_bias = jnp.where(own, 0.0, NEG_BIG)

    @pl.when(st == 0)
    def _():
        m_ref[...] = jnp.full(m_ref.shape, NEG_BIG, _F32)
        s_ref[...] = jnp.zeros(s_ref.shape, _F32)
        acc_ref[...] = jnp.zeros(acc_ref.shape, _F32)

    def update(logit_list, pv_fns):
        m_old = m_ref[...]
        m_new = m_old
        for lg in logit_list:
            m_new = jnp.maximum(m_new, jnp.max(lg, axis=1, keepdims=True))
        alpha = jnp.exp(m_old - m_new)
        s_new = s_ref[...] * alpha
        acc = acc_ref[...] * alpha
        for lg, pv_fn in zip(logit_list, pv_fns):
            p = jnp.exp(lg - m_new)
            s_new = s_new + jnp.sum(p, axis=1, keepdims=True)
            acc = acc + pv_fn(p)
        s_ref[...] = s_new
        acc_ref[...] = acc
        m_ref[...] = m_new

    logit_list, pv_fns = [], []
    for r in range(pps):
        kp = kpages[r][0].astype(_BF16)
        logits = lax.dot_general(q, kp, _NT, preferred_element_type=_F32) * scale
        logit_list.append(logits + head_bias + bias_ref[0, pl.ds(st * pps + r, 1), :])
        pv_fns.append(lambda p, r=r: jnp.dot(p.astype(_BF16), vpages[r][0].astype(_BF16),
                                             preferred_element_type=_F32))
    update(logit_list, pv_fns)

    @pl.when(st == pl.num_programs(1) - 1)
    def _():
        kn = kn_ref[0].astype(_BF16).astype(_F32)
        vn = vn_ref[0].astype(_BF16).astype(_F32)
        ln = jnp.sum(q.astype(_F32) * kn, axis=1, keepdims=True) * scale
        ln = ln + bias_ref[0, pl.ds(n_pages, 1), :][:, 0:1]
        update([ln], [lambda p: p * vn])
        o_ref[0] = acc_ref[...] / s_ref[...]


def _attn_sample(q_s, qi_s, wi_s, ki_s, k_s, v_s, cache_k, cache_v, cache_ik, page_table, layer):
    n_b, n_pages = page_table.shape
    depth, n_phys = cache_ik.shape[:2]
    k_sel = min(TOPK_MAX, (n_pages * PAGE_SIZE + 1) // 4)
    pps_s = min(SCORE_PAGES_PER_STEP, n_pages)
    pps_a = min(ATTN_PAGES_PER_STEP, n_pages)
    assert n_pages % pps_s == 0 and n_pages % pps_a == 0
    n_rows = n_pages + 8
    n_cols = PAGE_SIZE * KV_HEADS
    pt = page_table.reshape(-1)
    grp = N_HEADS // KV_HEADS

    def page_map(r, pps):
        return lambda b, s, pt_ref: (layer, pt_ref[b * n_pages + s * pps + r], 0, 0)

    qi3 = qi_s[:n_b].reshape(n_b, IDX_HEADS, IDX_DIM)
    wi3 = wi_s[:n_b, :IDX_HEADS].reshape(n_b, IDX_HEADS, 1)
    kin3 = ki_s[:n_b].reshape(n_b, 1, IDX_DIM)
    expand = (jnp.arange(LANES)[:, None] == jnp.arange(n_cols)[None, :] // KV_HEADS).astype(_BF16)
    bias = pl.pallas_call(
        functools.partial(_attn_sample_score_kernel, n_pages=n_pages, k_sel=k_sel, pps=pps_s),
        grid_spec=pltpu.PrefetchScalarGridSpec(
            num_scalar_prefetch=1, grid=(n_b, n_pages // pps_s),
            in_specs=[pl.BlockSpec((1, IDX_HEADS, IDX_DIM), lambda b, s, pt_ref: (b, 0, 0)),
                      pl.BlockSpec((1, IDX_HEADS, 1), lambda b, s, pt_ref: (b, 0, 0)),
                      pl.BlockSpec((1, 1, IDX_DIM), lambda b, s, pt_ref: (b, 0, 0)),
                      pl.BlockSpec((LANES, n_cols), lambda b, s, pt_ref: (0, 0))]
            + [pl.BlockSpec((None, 1, PAGE_SIZE, IDX_DIM), page_map(r, pps_s)) for r in range(pps_s)],
            out_specs=pl.BlockSpec((1, n_rows, n_cols), lambda b, s, pt_ref: (b, 0, 0)),
            scratch_shapes=[pltpu.VMEM((n_rows, LANES), _F32), pltpu.VMEM((n_rows, LANES), jnp.int32),
                            pltpu.VMEM((n_rows, LANES), _F32)]),
        out_shape=jax.ShapeDtypeStruct((n_b, n_rows, n_cols), _F32),
        compiler_params=_params(2), name="attn_sample_score")(
            pt, qi3, wi3, kin3, expand, *([cache_ik] * pps_s))

    pk = cache_k.reshape(depth, n_phys, n_cols, HEAD_DIM)
    pv = cache_v.reshape(depth, n_phys, n_cols, HEAD_DIM)
    q3 = q_s[:n_b].reshape(n_b, N_HEADS, HEAD_DIM)
    kn = jnp.repeat(k_s[:n_b].reshape(n_b, KV_HEADS, HEAD_DIM), grp, axis=1)
    vn = jnp.repeat(v_s[:n_b].reshape(n_b, KV_HEADS, HEAD_DIM), grp, axis=1)
    head_spec = pl.BlockSpec((1, N_HEADS, HEAD_DIM), lambda b, s, pt_ref: (b, 0, 0))
    out = pl.pallas_call(
        functools.partial(_attn_sample_kernel, n_pages=n_pages, pps=pps_a),
        grid_spec=pltpu.PrefetchScalarGridSpec(
            num_scalar_prefetch=1, grid=(n_b, n_pages // pps_a),
            in_specs=[head_spec, head_spec, head_spec,
                      pl.BlockSpec((1, n_rows, n_cols), lambda b, s, pt_ref: (b, 0, 0))]
            + [pl.BlockSpec((None, 1, n_cols, HEAD_DIM), page_map(r, pps_a)) for r in range(pps_a)]
            + [pl.BlockSpec((None, 1, n_cols, HEAD_DIM), page_map(r, pps_a)) for r in range(pps_a)],
            out_specs=head_spec,
            scratch_shapes=[pltpu.VMEM((N_HEADS, 1), _F32), pltpu.VMEM((N_HEADS, 1), _F32),
                            pltpu.VMEM((N_HEADS, HEAD_DIM), _F32)]),
        out_shape=jax.ShapeDtypeStruct((n_b, N_HEADS, HEAD_DIM), _F32),
        compiler_params=_params(2), name="attn_sample")(
            pt, q3, kn, vn, bias, *([pk] * pps_a), *([pv] * pps_a))
    return out.reshape(n_b, A_Q)


def _gmlp_kernel(u_ref, v_ref, g_ref, b_ref, ws_ref, bs_ref, o_ref, vn_ref=None):
    v = v_ref[...]
    mu = jnp.mean(v, axis=1, keepdims=True)
    vc = v - mu
    var = jnp.mean(vc * vc, axis=1, keepdims=True)
    vn = vc * lax.rsqrt(var + LN_EPS) * g_ref[...] + b_ref[...]
    if vn_ref is not None:
        vn_ref[...] = vn
    gw = GMLP_WIDTH // GMLP_GROUPS
    tril = (lax.broadcasted_iota(jnp.int32, (CHUNK, CHUNK), 0)
            >= lax.broadcasted_iota(jnp.int32, (CHUNK, CHUNK), 1))
    bs = bs_ref[...]
    for g in range(GMLP_GROUPS):
        wm = jnp.where(tril, ws_ref[g], 0.0).astype(_BF16)
        mixed = jnp.dot(wm, vn[:, g * gw:(g + 1) * gw].astype(_BF16), preferred_element_type=_F32)
        mixed = mixed + bs[:, g:g + 1]
        o_ref[:, g * gw:(g + 1) * gw] = (u_ref[:, g * gw:(g + 1) * gw] * mixed).astype(_BF16)


def _gmlp(hb, ln_g, ln_b, ws, bs, want_vn, name):
    m = hb.shape[0]
    w = GMLP_WIDTH
    out_shape = [jax.ShapeDtypeStruct((m, w), _BF16)]
    out_specs = [pl.BlockSpec((CHUNK, w), lambda c: (c, 0))]
    if want_vn:
        out_shape.append(jax.ShapeDtypeStruct((m, w), _F32))
        out_specs.append(pl.BlockSpec((CHUNK, w), lambda c: (c, 0)))
    return pl.pallas_call(
        _gmlp_kernel, grid=(m // CHUNK,),
        in_specs=[pl.BlockSpec((CHUNK, w), lambda c: (c, 0)), pl.BlockSpec((CHUNK, w), lambda c: (c, 1)),
                  pl.BlockSpec((1, w), lambda c: (0, 0)), pl.BlockSpec((1, w), lambda c: (0, 0)),
                  pl.BlockSpec((GMLP_GROUPS, CHUNK, CHUNK), lambda c: (0, 0, 0)),
                  pl.BlockSpec((CHUNK, GMLP_GROUPS), lambda c: (0, 0))],
        out_specs=out_specs, out_shape=out_shape, compiler_params=_params(1), name=name)(
            hb, hb, ln_g.reshape(1, w), ln_b.reshape(1, w), ws, jnp.transpose(bs))


def _head_sums(x, blk):
    outs = []
    for c in range(x.shape[1] // LANES):
        xc = x[:, c * LANES:(c + 1) * LANES]
        hi = xc.astype(_BF16)
        lo = (xc - hi.astype(_F32)).astype(_BF16)
        outs.append(jnp.dot(hi, blk, preferred_element_type=_F32) + jnp.dot(lo, blk, preferred_element_type=_F32))
    return jnp.concatenate(outs, axis=1)


def _rwkv_pre_kernel(pc_ref, sh_ref, prev_ref, mu_ref, w0_ref, w2_ref, a0_ref, a2_ref,
                     r_ref, w_ref, k_ref, v_ref, a_ref, *, tr, tiles_per_batch, roll_shift):
    pc = pc_ref[...]
    if roll_shift:
        first = pl.program_id(0) % tiles_per_batch == 0
        before = jnp.where(first, prev_ref[0], sh_ref[7:8, :])
        rolled = pltpu.roll(pc, 1, 0)
        rowi = lax.broadcasted_iota(jnp.int32, pc.shape, 0)
        shifted = jnp.where(rowi == 0, before, rolled)
    else:
        shifted = sh_ref[...]
    y = pc + mu_ref[...] * (shifted - pc)
    wdt = RWKV_WIDTH
    r = y[:, 0:wdt]
    k = y[:, wdt:2 * wdt]
    v = y[:, 2 * wdt:3 * wdt]
    wd = y[:, 3 * wdt:3 * wdt + D_DECAY]
    ad = y[:, 3 * wdt + D_DECAY:3 * wdt + D_DECAY + D_AAA]
    z = -(w0_ref[...] + jnp.dot(jnp.tanh(wd).astype(_BF16), w2_ref[...].astype(_BF16),
                                preferred_element_type=_F32))
    softplus = jnp.maximum(z, 0.0) + jnp.log(1.0 + jnp.exp(-jnp.abs(z)))
    decay = jnp.exp(-jnp.exp(-softplus - 0.5))
    a = jax.nn.sigmoid(a0_ref[...] + jnp.dot(ad.astype(_BF16), a2_ref[...].astype(_BF16),
                                             preferred_element_type=_F32))
    r_ref[...] = r
    w_ref[...] = decay
    k_ref[...] = k
    v_ref[...] = v
    a_ref[...] = a


def _seg_blk():
    seg = jnp.arange(LANES) // RWKV_HEAD
    return (seg[:, None] == seg[None, :]).astype(_BF16)


def _rwkv_pre(hc, shift_src, lp, seq, roll_shift, name):
    m = hc.shape[0]
    tr = 256 if (roll_shift and seq % 256 == 0) else (seq if roll_shift else m)
    tpb = seq // tr if roll_shift else 1
    wdt = RWKV_WIDTH
    if roll_shift:
        n_b = m // seq
        sh_spec = pl.BlockSpec((8, C_COLS), lambda i: (jnp.maximum(i * (tr // 8) - 1, 0), 0))
        prev = shift_src.reshape(n_b, 1, C_COLS)
        prev_spec = pl.BlockSpec((1, 1, C_COLS), lambda i: (i // tpb, 0, 0))
        sh_arg = hc
    else:
        sh_spec = pl.BlockSpec((tr, C_COLS), lambda i: (i, 0))
        prev = jnp.zeros((1, 1, C_COLS), _F32)
        prev_spec = pl.BlockSpec((1, 1, C_COLS), lambda i: (0, 0, 0))
        sh_arg = shift_src

    def vec(n):
        return pl.BlockSpec((1, n), lambda i: (0, 0))

    kern = functools.partial(_rwkv_pre_kernel, tr=tr, tiles_per_batch=tpb, roll_shift=roll_shift)
    return pl.pallas_call(
        kern, grid=(m // tr,),
        in_specs=[pl.BlockSpec((tr, C_COLS), lambda i: (i, 0)), sh_spec, prev_spec, vec(C_COLS), vec(wdt),
                  pl.BlockSpec((D_DECAY, wdt), lambda i: (0, 0)), vec(wdt),
                  pl.BlockSpec((D_AAA, wdt), lambda i: (0, 0))],
        out_specs=[pl.BlockSpec((tr, wdt), lambda i: (i, 0))] * 5,
        out_shape=[jax.ShapeDtypeStruct((m, wdt), _F32)] * 5,
        compiler_params=_params(1), name=name)(
            hc, sh_arg, prev, lp["rwkv_mu"].reshape(1, C_COLS), lp["rwkv_w0"].reshape(1, wdt), lp["rwkv_w2"],
            lp["rwkv_a0"].reshape(1, wdt), lp["rwkv_a2"])


SCAN_BATCH = LANES // RWKV_HEADS
HEAD_PAIRS = RWKV_HEADS // 2


def _rwkv_scan_kernel(*refs, tt, halves):
    cur = refs[0:5]
    nxt = refs[5:10]
    kkp_ref, ka_ref, s0_ref, o_ref, st_ref = refs[10:15]
    z = (refs[15:21], refs[21:27])
    zo = refs[27]
    _scan_block(cur, nxt, kkp_ref, ka_ref, s0_ref, o_ref, st_ref, z, zo, pl.program_id(0), tt, halves)


def _scan_block(cur, nxt, kkp_ref, ka_ref, s0_ref, o_ref, st_ref, z, zo, step_idx, tt, halves):
    n = RWKV_HEAD

    def to_lanes(srcs, off, dsts, t):
        def lanes(src):
            x = jnp.concatenate([src[bb, off + t] for bb in range(SCAN_BATCH)], axis=0)
            xt = x.T
            return jnp.concatenate([xt[:n], xt[n:]], axis=1)
        r, w, k, v, a = [lanes(src) for src in srcs]
        kk = k * kkp_ref[...]
        kk = kk / jnp.maximum(jnp.sqrt(jnp.sum(kk * kk, axis=0, keepdims=True)), 1e-12)
        zr, zw, zk, zv, zkk, zb = dsts
        zr[t] = r
        zw[t] = w
        zk[t] = k * (1.0 + (a - 1.0) * ka_ref[...])
        zv[t] = v
        zkk[t] = kk
        zb[t] = kk * a

    def from_lanes(t):
        y = zo[t]
        yt = jnp.concatenate([y[:, :LANES // 2], y[:, LANES // 2:]], axis=0).T
        for bb in range(SCAN_BATCH):
            o_ref[bb, t] = yt[bb * HEAD_PAIRS:(bb + 1) * HEAD_PAIRS]

    def update(zs, t, t_out):
        zr, zw, zk, zv, zkk, zb = zs
        acc = st_ref[0] * zkk[t, pl.ds(0, 1), :]
        for j in range(1, n):
            acc = acc + st_ref[j] * zkk[t, pl.ds(j, 1), :]
        sa = -acc
        vt = zv[t]
        out = None
        for j in range(n):
            sj = st_ref[j] * zw[t, pl.ds(j, 1), :] + sa * zb[t, pl.ds(j, 1), :] + vt * zk[t, pl.ds(j, 1), :]
            st_ref[j] = sj
            term = sj * zr[t, pl.ds(j, 1), :]
            out = term if out is None else out + term
        zo[t_out] = out

    def fill_first():
        def body(t, carry):
            to_lanes(cur, 0, z[0], t)
            return carry
        lax.fori_loop(0, tt, body, 0)

    if halves == 1:
        @pl.when(step_idx == 0)
        def _():
            st_ref[...] = s0_ref[...]

        fill_first()

        def body(t, carry):
            update(z[0], t, t)
            return carry

        lax.fori_loop(0, tt, body, 0)

        def out_body(t, carry):
            from_lanes(t)
            return carry

        lax.fori_loop(0, tt, out_body, 0)
        return

    @pl.when(step_idx == 0)
    def _():
        st_ref[...] = s0_ref[...]
        zo[...] = jnp.zeros(zo.shape, _F32)
        fill_first()

    def first_half(t, carry):
        to_lanes(cur, tt, z[1], t)
        from_lanes(jnp.maximum(t - 1, 0))
        update(z[0], t, t)
        return carry

    def second_half(t, carry):
        to_lanes(nxt, 0, z[0], t)
        from_lanes(tt + t - 1)
        update(z[1], t, tt + t)
        return carry

    lax.fori_loop(0, tt, first_half, 0)
    lax.fori_loop(0, tt, second_half, 0)
    from_lanes(2 * tt - 1)


def _head_param_to_lanes(p):
    q = jnp.transpose(p.reshape(HEAD_PAIRS, 2, RWKV_HEAD), (2, 1, 0))
    return jnp.broadcast_to(q[:, :, None, :], (RWKV_HEAD, 2, SCAN_BATCH, HEAD_PAIRS)).reshape(RWKV_HEAD, LANES)


def _rwkv_scan(vecs, s0, lp, seq, name):
    n = RWKV_HEAD
    halves, tt = (2, 8) if seq % 16 == 0 else (1, seq)
    blk = halves * tt
    n_half = seq // tt
    shape4 = (SCAN_BATCH, seq, HEAD_PAIRS, LANES)
    vspec = pl.BlockSpec((SCAN_BATCH, blk, HEAD_PAIRS, LANES), lambda s: (0, s, 0, 0))
    nspec = pl.BlockSpec((SCAN_BATCH, tt, HEAD_PAIRS, LANES),
                         lambda s: (0, jnp.minimum(halves * (s + 1), n_half - 1), 0, 0))
    sspec = pl.BlockSpec((n, n, LANES), lambda s: (0, 0, 0))
    pspec = pl.BlockSpec((n, LANES), lambda s: (0, 0))
    args = [x.reshape(shape4) for x in vecs]
    o, st = pl.pallas_call(
        functools.partial(_rwkv_scan_kernel, tt=tt, halves=halves), grid=(seq // blk,),
        in_specs=[vspec] * 5 + [nspec] * 5 + [pspec, pspec, sspec], out_specs=[vspec, sspec],
        out_shape=[jax.ShapeDtypeStruct(shape4, _F32), jax.ShapeDtypeStruct((n, n, LANES), _F32)],
        scratch_shapes=[pltpu.VMEM((tt, n, LANES), _F32)] * 12 + [pltpu.VMEM((blk, n, LANES), _F32)],
        compiler_params=_params(1), name=name)(
            *args, *args, _head_param_to_lanes(lp["rwkv_kk"]), _head_param_to_lanes(lp["rwkv_ka"]), s0)
    return o.reshape(SCAN_BATCH * seq, RWKV_WIDTH), st


def _state_to_lanes(s):
    s5 = s.reshape(SCAN_BATCH, HEAD_PAIRS, 2, RWKV_HEAD, RWKV_HEAD)
    return jnp.transpose(s5, (4, 3, 2, 0, 1)).reshape(RWKV_HEAD, RWKV_HEAD, LANES)


def _state_from_lanes(s):
    s5 = s.reshape(RWKV_HEAD, RWKV_HEAD, 2, SCAN_BATCH, HEAD_PAIRS)
    return jnp.transpose(s5, (3, 4, 2, 1, 0)).reshape(SCAN_BATCH, RWKV_HEADS, RWKV_HEAD, RWKV_HEAD)


def _rwkv_post_kernel(o_ref, r_ref, k_ref, v_ref, a_ref, g_ref, b_ref, rk_ref, ka_ref, blk_ref, out_ref):
    blk = blk_ref[...]
    o = o_ref[...]
    inv = 1.0 / RWKV_HEAD
    m = _head_sums(o, blk) * inv
    oc = o - m
    var = _head_sums(oc * oc, blk) * inv
    on = oc * lax.rsqrt(var + GN_EPS) * g_ref[...] + b_ref[...]
    k = k_ref[...] * (1.0 + (a_ref[...] - 1.0) * ka_ref[...])
    bonus = _head_sums(r_ref[...] * k * rk_ref[...], blk) * v_ref[...]
    out_ref[...] = (on + bonus).astype(_BF16)


def _rwkv_post(o, r, k, v, a, lp, name):
    m = o.shape[0]
    tr = 256 if m % 256 == 0 else m
    wdt = RWKV_WIDTH
    row = pl.BlockSpec((tr, wdt), lambda i: (i, 0))
    vec = pl.BlockSpec((1, wdt), lambda i: (0, 0))
    return pl.pallas_call(
        _rwkv_post_kernel, grid=(m // tr,),
        in_specs=[row] * 5 + [vec] * 4 + [pl.BlockSpec((LANES, LANES), lambda i: (0, 0))],
        out_specs=row, out_shape=jax.ShapeDtypeStruct((m, wdt), _BF16),
        compiler_params=_params(1), name=name)(
            o, r, k, v, a, lp["rwkv_gn_g"].reshape(1, wdt), lp["rwkv_gn_b"].reshape(1, wdt),
            lp["rwkv_rk"].reshape(1, wdt), lp["rwkv_ka"].reshape(1, wdt), _seg_blk())


def _pad_rows(x):
    return jnp.zeros((SAMPLE_ROWS,) + x.shape[1:], x.dtype).at[:x.shape[0]].set(x)


def _layer(xp, xpb, xs, xsb, seq, n_dec, layer, lp, big, cache_k, cache_v, cache_ik, page_table,
           st_shift, st_wkv):
    mp = xp.shape[0]
    n_batch = mp // seq
    tm = 1024 if seq % 1024 == 0 else seq
    d = D_MODEL
    rope_p = _rope_tables(jnp.arange(seq, dtype=jnp.int32))
    rope_s = _rope_tables(jnp.full((SAMPLE_ROWS,), PAST_LEN, jnp.int32))
    rope_extras = [dict(p=a, s=b) for a, b in zip(rope_p, rope_s)]
    lhs_x = [(xpb, xsb)]

    def rope_ep(accs, tex, rex):
        return [_rope_tiles(accs[0], *rex)]

    def pair(w, ncols, wcol, kb=None, kidx=0, lhs=0):
        return dict(lhs=lhs, w=w, layer=layer, kb=w.shape[1] if kb is None else kb, kidx=kidx, ncols=ncols,
                    wcol=wcol)

    def pair_in(ncols, row0):
        n_in = big["w_in_t"].shape[0] // big["depth"]
        return dict(lhs=0, w=big["w_in_t"], layer=layer, kb=d, kidx=0, ncols=ncols,
                    row0=lambda j: pl.multiple_of(layer * n_in + row0(j) + 0 * j, 8))

    tn = 512
    tn_w = 1024
    tm_w = 512 if seq % 512 == 0 else seq
    (q_hm, q_s), = _ws_matmul(
        "proj_q", lhs_x, [pair_in(tn_w, lambda j: j * tn_w)],
        [dict(ncols=tn_w, dtype=_BF16, col=lambda j: j, total=A_Q, hm=True)],
        rope_ep, tm=tm_w, n_steps=A_Q // tn_w, seq=seq, row_extras=rope_extras)
    qi_start = A_Q + 2 * A_KV
    (qi_hm, qi_s), = _ws_matmul(
        "proj_qi", lhs_x, [pair_in(tn_w, lambda j: qi_start + j * tn_w)],
        [dict(ncols=tn_w, dtype=_BF16, col=lambda j: j, total=IDX_HEADS * IDX_DIM, hm=True)],
        rope_ep, tm=tm_w, n_steps=IDX_HEADS * IDX_DIM // tn_w, seq=seq, row_extras=rope_extras)

    ki_start = qi_start + IDX_HEADS * IDX_DIM

    def k_ep(accs, tex, rex):
        kr = _rope_tiles(accs[0], *rex)
        kir = _rope_tiles(accs[1], *rex)
        return [kr, kr, kir, kir]

    zero = lambda j: 0
    (k_f, k_fs), (k_hm, _), (ki_f, ki_fs), (ki_b, _) = _ws_matmul(
        "proj_k", lhs_x,
        [pair_in(A_KV, lambda j: A_Q), pair_in(IDX_DIM, lambda j: ki_start)],
        [dict(ncols=A_KV, dtype=_F32, col=zero, total=A_KV),
         dict(ncols=A_KV, dtype=_BF16, col=zero, total=A_KV, hm=True),
         dict(ncols=IDX_DIM, dtype=_F32, col=zero, total=IDX_DIM),
         dict(ncols=IDX_DIM, dtype=_BF16, col=zero, total=IDX_DIM)],
        k_ep, tm=tm, n_steps=1, seq=seq, row_extras=rope_extras)

    def v_ep(accs, tex, rex):
        return [accs[0], accs[0], accs[1]]

    (v_f, v_fs), (v_hm, _), (wi_f, wi_fs) = _ws_matmul(
        "proj_v", lhs_x,
        [pair_in(A_KV, lambda j: A_Q + A_KV), pair_in(LANES, lambda j: ki_start + IDX_DIM)],
        [dict(ncols=A_KV, dtype=_F32, col=zero, total=A_KV),
         dict(ncols=A_KV, dtype=_BF16, col=zero, total=A_KV, hm=True),
         dict(ncols=LANES, dtype=_F32, col=zero, total=LANES)],
        v_ep, tm=tm, n_steps=1, seq=seq)

    ident = lambda accs, tex, rex: [accs[0]]
    (hb, hb_s), = _ws_matmul(
        "proj_b", lhs_x, [pair_in(tn_w, lambda j: A_COLS + j * tn_w)],
        [dict(ncols=tn_w, dtype=_F32, col=lambda j: j, total=B_COLS)],
        ident, tm=tm_w, n_steps=B_COLS // tn_w, seq=seq)
    tn_c = 640
    (hc, hc_s), = _ws_matmul(
        "proj_c", lhs_x, [pair_in(tn_c, lambda j: A_COLS + B_COLS + j * tn_c)],
        [dict(ncols=tn_c, dtype=_F32, col=lambda j: j, total=C_COLS)],
        ident, tm=tm, n_steps=C_COLS // tn_c, seq=seq)
    (gate, gate_s), = _ws_matmul(
        "proj_g", lhs_x, [pair_in(tn_w, lambda j: A_COLS + B_COLS + C_COLS + j * tn_w)],
        [dict(ncols=tn_w, dtype=_BF16, col=lambda j: j, total=G_COLS)],
        lambda accs, tex, rex: [jax.nn.sigmoid(accs[0])], tm=tm_w, n_steps=G_COLS // tn_w, seq=seq)
    out_a = _attn_prompt(q_hm, qi_hm, ki_b, wi_f, k_hm, v_hm, n_batch, seq)
    out_a_s = _attn_sample(q_s, qi_s, wi_fs, ki_fs, k_fs, v_fs, cache_k, cache_v, cache_ik, page_table, layer)
    out_a_s = _pad_rows(out_a_s.astype(_BF16))

    out_b, = _gmlp(hb, lp["gmlp_ln_g"], lp["gmlp_ln_b"], lp["gmlp_ws"], lp["gmlp_bs"], False, "gmlp_prompt")
    hb_chunks = jnp.zeros((n_dec, CHUNK, B_COLS), _F32).at[:, 0].set(hb_s[:n_dec]).reshape(n_dec * CHUNK, B_COLS)
    ob_s, vn_s = _gmlp(hb_chunks, lp["gmlp_ln_g"], lp["gmlp_ln_b"], lp["gmlp_ws"], lp["gmlp_bs"], True,
                       "gmlp_sample")
    out_b_s = _pad_rows(ob_s.reshape(n_dec, CHUNK, GMLP_WIDTH)[:, 0])
    vn_rows = vn_s.reshape(n_dec, CHUNK, GMLP_WIDTH)[:, 0]

    pre_p = _rwkv_pre(hc, jnp.zeros((n_batch, C_COLS), _F32), lp, seq, True, "rwkv_pre_prompt")
    zero_state = jnp.zeros((RWKV_HEAD, RWKV_HEAD, LANES), _F32)
    o_parts, st_parts = [], []
    for b0 in range(0, n_batch, SCAN_BATCH):
        rows = slice(b0 * seq, (b0 + SCAN_BATCH) * seq)
        o_c, st_c = _rwkv_scan([x[rows] for x in pre_p], zero_state, lp, seq, "rwkv_scan_prompt")
        o_parts.append(o_c)
        st_parts.append(_state_from_lanes(st_c))
    o_p = o_parts[0] if len(o_parts) == 1 else jnp.concatenate(o_parts, axis=0)
    wkv_p = st_parts[0] if len(st_parts) == 1 else jnp.concatenate(st_parts, axis=0)
    out_c = _rwkv_post(o_p, pre_p[0], pre_p[2], pre_p[3], pre_p[4], lp, "rwkv_post_prompt")

    pre_s = _rwkv_pre(hc_s, _pad_rows(st_shift), lp, 1, False, "rwkv_pre_sample")
    o_parts, st_parts = [], []
    for b0 in range(0, n_dec, SCAN_BATCH):
        o_c, st_c = _rwkv_scan([x[b0:b0 + SCAN_BATCH] for x in pre_s],
                               _state_to_lanes(st_wkv[b0:b0 + SCAN_BATCH]), lp, 1, "rwkv_scan_sample")
        o_parts.append(o_c)
        st_parts.append(_state_from_lanes(st_c))
    o_s = _pad_rows(jnp.concatenate(o_parts, axis=0))
    wkv_s = jnp.concatenate(st_parts, axis=0)
    out_c_s = _rwkv_post(o_s, pre_s[0], pre_s[2], pre_s[3], pre_s[4], lp, "rwkv_post_sample")

    tm_m = 512 if seq % 512 == 0 else seq
    n_g = d // tn

    def merge_ep(accs, tex, rex):
        return [tex[0].astype(_F32) * accs[0] + tex[1].astype(_F32) * accs[1] + tex[2].astype(_F32) * accs[2]]

    (merged, merged_s), = _ws_matmul(
        "merge", [(out_a, out_a_s), (out_b, out_b_s), (out_c, out_c_s)],
        [pair(big["attn_wo"], tn, lambda j: j, lhs=0), pair(big["gmlp_wo"], tn, lambda j: j, lhs=1),
         pair(big["rwkv_wo"], tn, lambda j: j, lhs=2)],
        [dict(ncols=tn, dtype=_BF16, col=lambda j: j, total=d)],
        merge_ep, tm=tm_m, n_steps=d // tn, seq=seq,
        tile_extras=[dict(p=gate, s=gate_s, ncols=tn, col=lambda j, a=a: j + a * n_g) for a in range(3)])

    def resid_ep(accs, tex, rex):
        return [ALPHA * tex[0] + accs[0]]

    (pre1, pre1_s), = _ws_matmul(
        "out_proj", [(merged, merged_s)], [pair(big["w_out"], tn, lambda j: j)],
        [dict(ncols=tn, dtype=_F32, col=lambda j: j, total=d)],
        resid_ep, tm=tm, n_steps=d // tn, seq=seq,
        tile_extras=[dict(p=xp, s=xs, ncols=tn, col=lambda j: j)])
    x1, x1b = _layernorm(pre1, lp["ln1_g"], lp["ln1_b"], "ln1_prompt")
    x1_s, x1b_s = _layernorm(pre1_s, lp["ln1_g"], lp["ln1_b"], "ln1_sample")

    tn_f = 256
    n_f = D_FF // tn_f

    def swiglu_ep(accs, tex, rex):
        return [jax.nn.silu(accs[0]) * accs[1]]

    (act, act_s), = _ws_matmul(
        "ffn_in", [(x1b, x1b_s)],
        [pair(big["ffn_w_in"], tn_f, lambda j: j), pair(big["ffn_w_in"], tn_f, lambda j: j + n_f)],
        [dict(ncols=tn_f, dtype=_BF16, col=lambda j: j, total=D_FF)],
        swiglu_ep, tm=tm, n_steps=n_f, seq=seq)

    kb = D_FF // 2
    part, part_s, scale = x1, x1_s, ALPHA
    for half in range(2):
        def acc_ep(accs, tex, rex, scale=scale):
            return [scale * tex[0] + accs[0]]

        (part, part_s), = _ws_matmul(
            "ffn_out%d" % half, [(act, act_s)],
            [pair(big["ffn_w_out"], tn, lambda j: j, kb=kb, kidx=half)],
            [dict(ncols=tn, dtype=_F32, col=lambda j: j, total=d)],
            acc_ep, tm=tm_m, n_steps=d // tn, seq=seq,
            tile_extras=[dict(p=part, s=part_s, ncols=tn, col=lambda j: j)])
        scale = 1.0
    x2, x2b = _layernorm(part, lp["ln2_g"], lp["ln2_b"], "ln2_prompt")
    x2_s, x2b_s = _layernorm(part_s, lp["ln2_g"], lp["ln2_b"], "ln2_sample")

    caches = dict(
        k_p=k_f.reshape(n_batch, seq, KV_HEADS, HEAD_DIM), v_p=v_f.reshape(n_batch, seq, KV_HEADS, HEAD_DIM),
        ik_p=ki_f.reshape(n_batch, seq, IDX_DIM), wkv_p=wkv_p,
        sh_p=hc.reshape(n_batch, seq, C_COLS)[:, -1],
        k_s=k_fs[:n_dec].reshape(n_dec, 1, KV_HEADS, HEAD_DIM), v_s=v_fs[:n_dec].reshape(n_dec, 1, KV_HEADS, HEAD_DIM),
        ik_s=ki_fs[:n_dec].reshape(n_dec, 1, IDX_DIM), wkv_s=wkv_s, sh_s=hc_s[:n_dec],
        gv_s=vn_rows.reshape(n_dec, 1, GMLP_WIDTH))
    return x2, x2b, x2_s, x2b_s, caches


def kernel(x_prompt, x_sample, cache_k, cache_v, cache_idx_k, page_table, state_wkv, state_shift, w_in, attn_wo, gmlp_ln_g, gmlp_ln_b, gmlp_ws, gmlp_bs, gmlp_wo, rwkv_mu, rwkv_w0, rwkv_w2, rwkv_a0, rwkv_a2, rwkv_kk, rwkv_ka, rwkv_rk, rwkv_gn_g, rwkv_gn_b, rwkv_wo, w_out, ln1_g, ln1_b, ffn_w_in, ffn_w_out, ln2_g, ln2_b):
    n_batch, seq, d = x_prompt.shape
    n_dec = x_sample.shape[0]
    depth = w_in.shape[0]
    assert x_sample.shape[1] == 1 and n_dec <= SAMPLE_ROWS and n_dec % SCAN_BATCH == 0
    assert n_batch % SCAN_BATCH == 0 and d == D_MODEL and seq % CHUNK == 0
    xp = x_prompt.reshape(n_batch * seq, d)
    xs = _pad_rows(x_sample.reshape(n_dec, d))
    xpb, xsb = xp.astype(_BF16), xs.astype(_BF16)
    big = dict(w_in_t=jnp.swapaxes(w_in, 1, 2).reshape(depth * w_in.shape[2], d), depth=depth, attn_wo=attn_wo, gmlp_wo=gmlp_wo, rwkv_wo=rwkv_wo, w_out=w_out,
               ffn_w_in=ffn_w_in, ffn_w_out=ffn_w_out)
    per_layer = []
    for l in range(depth):
        lp = dict(gmlp_ln_g=gmlp_ln_g[l], gmlp_ln_b=gmlp_ln_b[l], gmlp_ws=gmlp_ws[l], gmlp_bs=gmlp_bs[l],
                  rwkv_mu=rwkv_mu[l], rwkv_w0=rwkv_w0[l], rwkv_w2=rwkv_w2[l], rwkv_a0=rwkv_a0[l],
                  rwkv_a2=rwkv_a2[l], rwkv_kk=rwkv_kk[l], rwkv_ka=rwkv_ka[l], rwkv_rk=rwkv_rk[l],
                  rwkv_gn_g=rwkv_gn_g[l], rwkv_gn_b=rwkv_gn_b[l], ln1_g=ln1_g[l], ln1_b=ln1_b[l],
                  ln2_g=ln2_g[l], ln2_b=ln2_b[l])
        xp, xpb, xs, xsb, c = _layer(xp, xpb, xs, xsb, seq, n_dec, l, lp, big, cache_k, cache_v, cache_idx_k,
                                     page_table, state_shift[l], state_wkv[l])
        per_layer.append(c)

    def stack(name):
        return jnp.stack([c[name] for c in per_layer])

    return (xp.reshape(n_batch, seq, d), xs[:n_dec].reshape(n_dec, 1, d),
            stack("k_p"), stack("v_p"), stack("ik_p"), stack("wkv_p"), stack("sh_p"),
            stack("k_s"), stack("v_s"), stack("ik_s"), stack("wkv_s"), stack("sh_s"), stack("gv_s"))
```

```python
import functools

import jax
import jax.numpy as jnp
from jax import lax
from jax.experimental import pallas as pl
from jax.experimental.pallas import tpu as pltpu

D_MODEL = 4096
PAST_LEN = 16384
PAGE_SIZE = 128
N_HEADS = 16
KV_HEADS = 4
HEAD_DIM = 128
ROT_DIM = HEAD_DIM // 4
ROPE_THETA = 500000.0
IDX_HEADS = 32
IDX_DIM = 128
TOPK_MAX = 256
A_Q = N_HEADS * HEAD_DIM
A_KV = KV_HEADS * HEAD_DIM
CHUNK = 128
GMLP_WIDTH = D_MODEL // 2
GMLP_GROUPS = 8
RWKV_HEAD = 64
RWKV_WIDTH = D_MODEL // 2
RWKV_HEADS = RWKV_WIDTH // RWKV_HEAD
D_DECAY = max(32, int(round(D_MODEL ** 0.5 * 1.8 / 32)) * 32)
D_AAA = D_DECAY
D_FF = ((8 * D_MODEL // 3 + 255) // 256) * 256
A_COLS = A_Q + 2 * A_KV + IDX_HEADS * IDX_DIM + IDX_DIM + IDX_HEADS
B_COLS = 2 * GMLP_WIDTH
C_COLS = 3 * RWKV_WIDTH + D_DECAY + D_AAA
G_COLS = 3 * D_MODEL
DEPTH = 2
ALPHA = (2 * DEPTH) ** 0.25
LN_EPS = 1e-5
GN_EPS = 64e-5

LANES = 128
SAMPLE_ROWS = 16
VMEM_LIMIT = 58 * 1024 * 1024
KEY_CHUNK = 512
INT_MIN = -(2 ** 31)
NEG_BIG = -1e30
LOG2E = 1.4426950408889634

_F32 = jnp.float32
_BF16 = jnp.bfloat16
_NT = (((1,), (1,)), ((), ()))


def _params(n_grid):
    return pltpu.CompilerParams(dimension_semantics=("arbitrary",) * n_grid,
                                vmem_limit_bytes=VMEM_LIMIT)


def _ws_matmul(name, lhs, pairs, outs, epilogue, *, tm, n_steps, seq, tile_extras=(), row_extras=()):
    mp = lhs[0][0].shape[0]
    n_i = mp // tm
    tpb = seq // tm
    n_batch = mp // seq

    in_specs, args = [], []
    for a, (xp, xs) in enumerate(lhs):
        kb, kidx = [(p["kb"], p["kidx"]) for p in pairs if p["lhs"] == a][0]
        in_specs += [pl.BlockSpec((tm, kb), lambda j, i, kidx=kidx: (i, kidx)),
                     pl.BlockSpec((SAMPLE_ROWS, kb), lambda j, i, kidx=kidx: (0, kidx))]
        args += [xp, xs]
    for p in pairs:
        if p.get("row0") is not None:
            in_specs.append(pl.BlockSpec((pl.Element(p["ncols"]), pl.Element(p["kb"])),
                                         lambda j, i, p=p: (p["row0"](j), 0)))
        else:
            in_specs.append(pl.BlockSpec((None, p["kb"], p["ncols"]),
                                         lambda j, i, p=p: (p["layer"], p["kidx"], p["wcol"](j))))
        args.append(p["w"])
    for e in tile_extras:
        in_specs += [pl.BlockSpec((tm, e["ncols"]), lambda j, i, e=e: (i, e["col"](j))),
                     pl.BlockSpec((SAMPLE_ROWS, e["ncols"]), lambda j, i, e=e: (0, e["col"](j)))]
        args += [e["p"], e["s"]]
    for e in row_extras:
        in_specs += [pl.BlockSpec((tm, LANES), lambda j, i: (i % tpb, 0)),
                     pl.BlockSpec((SAMPLE_ROWS, LANES), lambda j, i: (0, 0))]
        args += [e["p"], e["s"]]

    out_specs, out_shapes = [], []
    for o in outs:
        if o.get("hm"):
            hpt = o["ncols"] // LANES
            out_shapes.append(jax.ShapeDtypeStruct((n_batch, o["total"] // LANES, seq, LANES), o["dtype"]))
            out_specs.append(pl.BlockSpec((1, hpt, tm, LANES),
                                          lambda j, i, o=o: (i // tpb, o["col"](j), i % tpb, 0)))
        else:
            out_shapes.append(jax.ShapeDtypeStruct((mp, o["total"]), o["dtype"]))
            out_specs.append(pl.BlockSpec((tm, o["ncols"]), lambda j, i, o=o: (i, o["col"](j))))
        out_shapes.append(jax.ShapeDtypeStruct((SAMPLE_ROWS, o["total"]), o["dtype"]))
        out_specs.append(pl.BlockSpec((SAMPLE_ROWS, o["ncols"]), lambda j, i, o=o: (0, o["col"](j))))

    scratch = [pltpu.VMEM((p["kb"], p["ncols"]), _BF16) for p in pairs]
    n_lhs, n_pairs, n_te, n_re, n_out = len(lhs), len(pairs), len(tile_extras), len(row_extras), len(outs)

    def kernel(*refs):
        pos = 0
        x_refs = [(refs[pos + 2 * a], refs[pos + 2 * a + 1]) for a in range(n_lhs)]
        pos += 2 * n_lhs
        w_refs = refs[pos:pos + n_pairs]
        pos += n_pairs
        te_refs = [(refs[pos + 2 * a], refs[pos + 2 * a + 1]) for a in range(n_te)]
        pos += 2 * n_te
        re_refs = [(refs[pos + 2 * a], refs[pos + 2 * a + 1]) for a in range(n_re)]
        pos += 2 * n_re
        o_refs = [(refs[pos + 2 * a], refs[pos + 2 * a + 1]) for a in range(n_out)]
        pos += 2 * n_out
        wb_refs = refs[pos:pos + n_pairs]
        i = pl.program_id(1)

        def run(which):
            accs = [jnp.dot(x_refs[p["lhs"]][which][...], wb[...], preferred_element_type=_F32)
                    for p, wb in zip(pairs, wb_refs)]
            res = epilogue(accs, [t[which][...] for t in te_refs], [r[which][...] for r in re_refs])
            for o, (op_ref, os_ref), val in zip(outs, o_refs, res):
                if which == 0 and o.get("hm"):
                    for h in range(o["ncols"] // LANES):
                        op_ref[0, h] = val[:, h * LANES:(h + 1) * LANES].astype(o["dtype"])
                else:
                    (op_ref if which == 0 else os_ref)[...] = val.astype(o["dtype"])

        @pl.when(i == 0)
        def _():
            for p, w_ref, wb in zip(pairs, w_refs, wb_refs):
                if p.get("row0") is not None:
                    for kc in range(p["kb"] // LANES):
                        blk = w_ref[:, kc * LANES:(kc + 1) * LANES]
                        wb[kc * LANES:(kc + 1) * LANES, :] = blk.T.astype(_BF16)
                    continue
                rows = 256 if p["kb"] % 256 == 0 else 128

                def cast(c, carry, w_ref=w_ref, wb=wb, rows=rows):
                    r0 = pl.multiple_of(c * rows, rows)
                    wb[pl.ds(r0, rows), :] = w_ref[pl.ds(r0, rows), :].astype(_BF16)
                    return carry

                lax.fori_loop(0, p["kb"] // rows, cast, 0)
            run(1)

        run(0)

    res = pl.pallas_call(
        kernel, grid=(n_steps, n_i), in_specs=in_specs, out_specs=out_specs, out_shape=out_shapes,
        scratch_shapes=scratch, compiler_params=_params(2), name=name)(*args)
    return [(res[2 * a], res[2 * a + 1]) for a in range(n_out)]


def _rope_tiles(x, c, s1, s2):
    outs = []
    for h in range(x.shape[1] // LANES):
        xh = x[:, h * LANES:(h + 1) * LANES]
        outs.append(xh * c + pltpu.roll(xh, ROT_DIM // 2, 1) * s1 + pltpu.roll(xh, LANES - ROT_DIM // 2, 1) * s2)
    return outs[0] if len(outs) == 1 else jnp.concatenate(outs, axis=1)


def _rope_tables(pos):
    half = ROT_DIM // 2
    inv = ROPE_THETA ** (-2.0 * jnp.arange(half, dtype=_F32) / ROT_DIM)
    ang = pos.astype(_F32)[:, None] * inv[None, :]
    cos, sin = jnp.cos(ang), jnp.sin(ang)
    n = pos.shape[0]
    c = jnp.concatenate([cos, cos, jnp.ones((n, LANES - ROT_DIM), _F32)], axis=1)
    s1 = jnp.concatenate([jnp.zeros((n, half), _F32), sin, jnp.zeros((n, LANES - ROT_DIM), _F32)], axis=1)
    s2 = jnp.concatenate([-sin, jnp.zeros((n, LANES - half), _F32)], axis=1)
    return c, s1, s2


def _ln_kernel(x_ref, g_ref, b_ref, o_ref, ob_ref):
    x = x_ref[...]
    mu = jnp.mean(x, axis=1, keepdims=True)
    xc = x - mu
    var = jnp.mean(xc * xc, axis=1, keepdims=True)
    y = xc * lax.rsqrt(var + LN_EPS) * g_ref[...] + b_ref[...]
    o_ref[...] = y
    ob_ref[...] = y.astype(_BF16)


def _layernorm(x, g, b, name):
    m, d = x.shape
    tr = 256 if m % 256 == 0 else m
    return pl.pallas_call(
        _ln_kernel, grid=(m // tr,),
        in_specs=[pl.BlockSpec((tr, d), lambda i: (i, 0)), pl.BlockSpec((1, d), lambda i: (0, 0)),
                  pl.BlockSpec((1, d), lambda i: (0, 0))],
        out_specs=[pl.BlockSpec((tr, d), lambda i: (i, 0)), pl.BlockSpec((tr, d), lambda i: (i, 0))],
        out_shape=[jax.ShapeDtypeStruct((m, d), _F32), jax.ShapeDtypeStruct((m, d), _BF16)],
        compiler_params=_params(1), name=name)(x, g.reshape(1, d), b.reshape(1, d))


def _sortable(score):
    bits = pltpu.bitcast(score, jnp.int32)
    return bits ^ ((bits >> 31) & jnp.int32(0x7FFFFFFF))


def _topk_bias(n_ch, load_key, meta, store_bias, k_sel, shape, axes, index_bits, neg):
    kf = float(k_sel)

    def count(pred):
        if axes == (0,):
            def cb(c, part):
                x = jnp.where(pred(c), 1.0, 0.0)
                slabs = [x[i * 8:(i + 1) * 8] for i in range(x.shape[0] // 8)]
                while len(slabs) > 1:
                    slabs = [slabs[i] + slabs[i + 1] for i in range(0, len(slabs), 2)]
                return part + slabs[0]
            part = lax.fori_loop(0, n_ch, cb, jnp.zeros((8, shape[1]), _F32))
            return jnp.sum(part, axis=0, keepdims=True)

        def cb(c, cnt):
            return cnt + jnp.sum(jnp.where(pred(c), 1.0, 0.0), axis=axes, keepdims=True)
        return lax.fori_loop(0, n_ch, cb, jnp.zeros(shape, _F32))

    def bisect(it, t):
        cand = t + lax.shift_left(jnp.int32(1), jnp.int32(31) - it)
        return jnp.where(count(lambda c: load_key(c) >= cand) >= kf, cand, t)

    t = lax.fori_loop(0, 32, bisect, jnp.full(shape, INT_MIN, jnp.int32))
    n_ge = count(lambda c: (load_key(c) >= t) & meta(c)[0])

    def tie_limit():
        need = kf - count(lambda c: load_key(c) > t)

        def body(it, m):
            cand = m + lax.shift_left(jnp.int32(1), jnp.int32(index_bits - 1) - it)
            cnt = count(lambda c: (load_key(c) == t) & meta(c)[0] & (meta(c)[1] < cand))
            return jnp.where(cnt < need, cand, m)

        return lax.fori_loop(0, index_bits, body, jnp.zeros(shape, jnp.int32))

    no_limit = jnp.full(shape, 2 ** 30, jnp.int32)
    m = lax.cond(jnp.max(n_ge) > kf, tie_limit, lambda: no_limit)

    def emit(c, carry):
        keys = load_key(c)
        valid, idx = meta(c)
        store_bias(c, jnp.where((keys > t) | ((keys == t) & valid & (idx <= m)), 0.0, neg))
        return carry

    lax.fori_loop(0, n_ch, emit, 0)


def _attn_prompt_kernel(q_ref, qi_ref, ki_ref, wi_ref, k_ref, v_ref, o_ref,
                        acc_ref, key_ref, bias_ref, wb_ref, l_ref, oacc_ref, sacc_ref, m_ref,
                        *, tq, kc, k_sel, index_bits):
    q0 = pl.program_id(1) * tq
    n_ch = (q0 + tq + kc - 1) // kc
    grp = N_HEADS // KV_HEADS
    wi = wi_ref[...] * ((IDX_HEADS * IDX_DIM) ** -0.5)
    for h in range(IDX_HEADS):
        wb_ref[h] = jnp.broadcast_to(wi[:, h:h + 1], (tq, LANES))

    def meta(c):
        col = c * kc + lax.broadcasted_iota(jnp.int32, (tq, kc), 1)
        row = q0 + lax.broadcasted_iota(jnp.int32, (tq, kc), 0)
        return col <= row, col

    def meta_t(c):
        col = c * kc + lax.broadcasted_iota(jnp.int32, (kc, tq), 0)
        row = q0 + lax.broadcasted_iota(jnp.int32, (kc, tq), 1)
        return col <= row, col

    def score_chunk(c, carry):
        ki = ki_ref[pl.ds(pl.multiple_of(c * kc, kc), kc), :]
        acc_ref[...] = jnp.zeros((tq, kc), _F32)

        def idx_body(hc, carry2):
            qc = qi_ref[0, pl.ds(hc * grp, grp)].reshape(grp * tq, IDX_DIM)
            d = lax.dot_general(qc, ki, _NT, preferred_element_type=_F32)
            part = None
            for hh in range(grp):
                w = jnp.tile(wb_ref[hc * grp + hh], (1, kc // LANES))
                term = jnp.maximum(d[hh * tq:(hh + 1) * tq], 0.0) * w
                part = term if part is None else part + term
            acc_ref[...] += part
            return carry2

        lax.fori_loop(0, IDX_HEADS // grp, idx_body, 0, unroll=True)
        keys = jnp.where(meta(c)[0], _sortable(acc_ref[...]), INT_MIN)
        key_ref[c] = pltpu.bitcast(pltpu.bitcast(keys, _F32).T, jnp.int32)
        return carry

    lax.fori_loop(0, n_ch, score_chunk, 0)

    def store_bias(c, val):
        bias_ref[c] = val.T

    _topk_bias(n_ch, lambda c: key_ref[c], meta_t, store_bias, k_sel, (1, tq), (0,), index_bits, -jnp.inf)

    m_ref[...] = jnp.full(m_ref.shape, -jnp.inf, _F32)
    oacc_ref[...] = jnp.zeros(oacc_ref.shape, _F32)
    sacc_ref[...] = jnp.zeros(sacc_ref.shape, _F32)

    def logits_chunk(c, carry):
        k0 = pl.multiple_of(c * kc, kc)
        bias = jnp.tile(bias_ref[c], (grp, 1))
        for g in range(KV_HEADS):
            qg = q_ref[0, g * grp:(g + 1) * grp].reshape(grp * tq, HEAD_DIM)
            lg = lax.dot_general(qg, k_ref[0, g, pl.ds(k0, kc), :], _NT, preferred_element_type=_F32)
            lg = lg * (HEAD_DIM ** -0.5 * LOG2E) + bias
            l_ref[g, c] = lg
            m_ref[g] = jnp.maximum(m_ref[g], jnp.max(lg, axis=1, keepdims=True))
        return carry

    lax.fori_loop(0, n_ch, logits_chunk, 0)

    def pv_chunk(c, carry):
        k0 = pl.multiple_of(c * kc, kc)
        for g in range(KV_HEADS):
            p = jnp.exp2(l_ref[g, c] - m_ref[g])
            sacc_ref[g] += jnp.sum(p, axis=1, keepdims=True)
            oacc_ref[g] += jnp.dot(p.astype(_BF16), v_ref[0, g, pl.ds(k0, kc), :], preferred_element_type=_F32)
        return carry

    lax.fori_loop(0, n_ch, pv_chunk, 0)
    for g in range(KV_HEADS):
        out = oacc_ref[g] / sacc_ref[g]
        for hh in range(grp):
            h = g * grp + hh
            o_ref[:, h * HEAD_DIM:(h + 1) * HEAD_DIM] = out[hh * tq:(hh + 1) * tq].astype(_BF16)


def _attn_prompt(q_hm, qi_hm, ki_b, wi, k_hm, v_hm, n_batch, seq):
    tq = 128
    kc = KEY_CHUNK if seq % KEY_CHUNK == 0 else seq
    nqb = seq // tq
    grp = N_HEADS // KV_HEADS
    k_sel = min(TOPK_MAX, seq // 4)
    kern = functools.partial(_attn_prompt_kernel, tq=tq, kc=kc, k_sel=k_sel,
                             index_bits=(seq - 1).bit_length() + 1)
    return pl.pallas_call(
        kern, grid=(n_batch, nqb),
        in_specs=[
            pl.BlockSpec((1, N_HEADS, tq, HEAD_DIM), lambda b, q: (b, 0, q, 0)),
            pl.BlockSpec((1, IDX_HEADS, tq, IDX_DIM), lambda b, q: (b, 0, q, 0)),
            pl.BlockSpec((seq, IDX_DIM), lambda b, q: (b, 0)),
            pl.BlockSpec((tq, LANES), lambda b, q: (b * nqb + q, 0)),
            pl.BlockSpec((1, KV_HEADS, seq, HEAD_DIM), lambda b, q: (b, 0, 0, 0)),
            pl.BlockSpec((1, KV_HEADS, seq, HEAD_DIM), lambda b, q: (b, 0, 0, 0)),
        ],
        out_specs=pl.BlockSpec((tq, A_Q), lambda b, q: (b * nqb + q, 0)),
        out_shape=jax.ShapeDtypeStruct((n_batch * seq, A_Q), _BF16),
        scratch_shapes=[pltpu.VMEM((tq, kc), _F32), pltpu.VMEM((seq // kc, kc, tq), jnp.int32),
                        pltpu.VMEM((seq // kc, tq, kc), _F32), pltpu.VMEM((IDX_HEADS, tq, LANES), _F32),
                        pltpu.VMEM((KV_HEADS, seq // kc, grp * tq, kc), _F32),
                        pltpu.VMEM((KV_HEADS, grp * tq, HEAD_DIM), _F32),
                        pltpu.VMEM((KV_HEADS, grp * tq, 1), _F32), pltpu.VMEM((KV_HEADS, grp * tq, 1), _F32)],
        compiler_params=_params(2), name="attn_prompt")(q_hm, qi_hm, ki_b, wi, k_hm, v_hm)


SCORE_PAGES_PER_STEP = 32
ATTN_PAGES_PER_STEP = 16


def _attn_sample_score_kernel(pt_ref, qi_ref, wi_ref, kin_ref, exp_ref, *rest, n_pages, k_sel, pps):
    pages = rest[:pps]
    bias_ref, score_ref, key_ref, sel_ref = rest[pps:]
    del pt_ref
    s = pl.program_id(1)
    n_rows = n_pages + 8
    qi = qi_ref[0]
    wcol = wi_ref[0] * ((IDX_HEADS * IDX_DIM) ** -0.5)
    for r in range(pps):
        page = pages[r][0].astype(_BF16)
        d = lax.dot_general(qi, page, _NT, preferred_element_type=_F32)
        sc = jnp.sum(jnp.maximum(d, 0.0) * wcol, axis=0, keepdims=True)
        score_ref[pl.ds(s * pps + r, 1), :] = sc

    @pl.when(s == pl.num_programs(1) - 1)
    def _():
        kin = kin_ref[0].astype(_BF16).astype(_F32)
        dn = jnp.sum(qi.astype(_F32) * kin, axis=1, keepdims=True)
        sn = jnp.sum(jnp.maximum(dn, 0.0) * wcol, axis=0, keepdims=True)
        score_ref[pl.ds(n_pages, 8), :] = jnp.broadcast_to(sn, (8, LANES))
        rowi = lax.broadcasted_iota(jnp.int32, (n_rows, LANES), 0)
        lane = lax.broadcasted_iota(jnp.int32, (n_rows, LANES), 1)
        valid = (rowi < n_pages) | ((rowi == n_pages) & (lane == 0))
        pos = rowi * LANES + lane
        key_ref[...] = jnp.where(valid, _sortable(score_ref[...]), INT_MIN)

        def store_sel(c, val):
            sel_ref[...] = val

        _topk_bias(1, lambda c: key_ref[...], lambda c: (valid, pos), store_sel, k_sel, (1, 1), (0, 1),
                   (n_rows * LANES - 1).bit_length() + 1, 1.0)
        rep = jnp.dot(sel_ref[...].astype(_BF16), exp_ref[...], preferred_element_type=_F32)
        bias_ref[0] = jnp.where(rep < 0.5, 0.0, NEG_BIG)


def _attn_sample_kernel(pt_ref, q_ref, kn_ref, vn_ref, bias_ref, *rest, n_pages, pps):
    kpages = rest[:pps]
    vpages = rest[pps:2 * pps]
    o_ref, m_ref, s_ref, acc_ref = rest[2 * pps:]
    del pt_ref
    st = pl.program_id(1)
    scale = HEAD_DIM ** -0.5
    grp = N_HEADS // KV_HEADS
    n_cols = PAGE_SIZE * KV_HEADS
    q = q_ref[0]
    own = (lax.broadcasted_iota(jnp.int32, (N_HEADS, n_cols), 1) % KV_HEADS
           == lax.broadcasted_iota(jnp.int32, (N_HEADS, n_cols), 0) // grp)
    head_bias = jnp.where(own, 0.0, NEG_BIG)

    @pl.when(st == 0)
    def _():
        m_ref[...] = jnp.full(m_ref.shape, NEG_BIG, _F32)
        s_ref[...] = jnp.zeros(s_ref.shape, _F32)
        acc_ref[...] = jnp.zeros(acc_ref.shape, _F32)

    def update(logit_list, pv_fns):
        m_old = m_ref[...]
        m_new = m_old
        for lg in logit_list:
            m_new = jnp.maximum(m_new, jnp.max(lg, axis=1, keepdims=True))
        alpha = jnp.exp(m_old - m_new)
        s_new = s_ref[...] * alpha
        acc = acc_ref[...] * alpha
        for lg, pv_fn in zip(logit_list, pv_fns):
            p = jnp.exp(lg - m_new)
            s_new = s_new + jnp.sum(p, axis=1, keepdims=True)
            acc = acc + pv_fn(p)
        s_ref[...] = s_new
        acc_ref[...] = acc
        m_ref[...] = m_new

    logit_list, pv_fns = [], []
    for r in range(pps):
        kp = kpages[r][0].astype(_BF16)
        logits = lax.dot_general(q, kp, _NT, preferred_element_type=_F32) * scale
        logit_list.append(logits + head_bias + bias_ref[0, pl.ds(st * pps + r, 1), :])
        pv_fns.append(lambda p, r=r: jnp.dot(p.astype(_BF16), vpages[r][0].astype(_BF16),
                                             preferred_element_type=_F32))
    update(logit_list, pv_fns)

    @pl.when(st == pl.num_programs(1) - 1)
    def _():
        kn = kn_ref[0].astype(_BF16).astype(_F32)
        vn = vn_ref[0].astype(_BF16).astype(_F32)
        ln = jnp.sum(q.astype(_F32) * kn, axis=1, keepdims=True) * scale
        ln = ln + bias_ref[0, pl.ds(n_pages, 1), :][:, 0:1]
        update([ln], [lambda p: p * vn])
        o_ref[0] = acc_ref[...] / s_ref[...]


def _attn_sample(q_s, qi_s, wi_s, ki_s, k_s, v_s, cache_k, cache_v, cache_ik, page_table, layer):
    n_b, n_pages = page_table.shape
    depth, n_phys = cache_ik.shape[:2]
    k_sel = min(TOPK_MAX, (n_pages * PAGE_SIZE + 1) // 4)
    pps_s = min(SCORE_PAGES_PER_STEP, n_pages)
    pps_a = min(ATTN_PAGES_PER_STEP, n_pages)
    assert n_pages % pps_s == 0 and n_pages % pps_a == 0
    n_rows = n_pages + 8
    n_cols = PAGE_SIZE * KV_HEADS
    pt = page_table.reshape(-1)
    grp = N_HEADS // KV_HEADS

    def page_map(r, pps):
        return lambda b, s, pt_ref: (layer, pt_ref[b * n_pages + s * pps + r], 0, 0)

    qi3 = qi_s[:n_b].reshape(n_b, IDX_HEADS, IDX_DIM)
    wi3 = wi_s[:n_b, :IDX_HEADS].reshape(n_b, IDX_HEADS, 1)
    kin3 = ki_s[:n_b].reshape(n_b, 1, IDX_DIM)
    expand = (jnp.arange(LANES)[:, None] == jnp.arange(n_cols)[None, :] // KV_HEADS).astype(_BF16)
    bias = pl.pallas_call(
        functools.partial(_attn_sample_score_kernel, n_pages=n_pages, k_sel=k_sel, pps=pps_s),
        grid_spec=pltpu.PrefetchScalarGridSpec(
            num_scalar_prefetch=1, grid=(n_b, n_pages // pps_s),
            in_specs=[pl.BlockSpec((1, IDX_HEADS, IDX_DIM), lambda b, s, pt_ref: (b, 0, 0)),
                      pl.BlockSpec((1, IDX_HEADS, 1), lambda b, s, pt_ref: (b, 0, 0)),
                      pl.BlockSpec((1, 1, IDX_DIM), lambda b, s, pt_ref: (b, 0, 0)),
                      pl.BlockSpec((LANES, n_cols), lambda b, s, pt_ref: (0, 0))]
            + [pl.BlockSpec((None, 1, PAGE_SIZE, IDX_DIM), page_map(r, pps_s)) for r in range(pps_s)],
            out_specs=pl.BlockSpec((1, n_rows, n_cols), lambda b, s, pt_ref: (b, 0, 0)),
            scratch_shapes=[pltpu.VMEM((n_rows, LANES), _F32), pltpu.VMEM((n_rows, LANES), jnp.int32),
                            pltpu.VMEM((n_rows, LANES), _F32)]),
        out_shape=jax.ShapeDtypeStruct((n_b, n_rows, n_cols), _F32),
        compiler_params=_params(2), name="attn_sample_score")(
            pt, qi3, wi3, kin3, expand, *([cache_ik] * pps_s))

    pk = cache_k.reshape(depth, n_phys, n_cols, HEAD_DIM)
    pv = cache_v.reshape(depth, n_phys, n_cols, HEAD_DIM)
    q3 = q_s[:n_b].reshape(n_b, N_HEADS, HEAD_DIM)
    kn = jnp.repeat(k_s[:n_b].reshape(n_b, KV_HEADS, HEAD_DIM), grp, axis=1)
    vn = jnp.repeat(v_s[:n_b].reshape(n_b, KV_HEADS, HEAD_DIM), grp, axis=1)
    head_spec = pl.BlockSpec((1, N_HEADS, HEAD_DIM), lambda b, s, pt_ref: (b, 0, 0))
    out = pl.pallas_call(
        functools.partial(_attn_sample_kernel, n_pages=n_pages, pps=pps_a),
        grid_spec=pltpu.PrefetchScalarGridSpec(
            num_scalar_prefetch=1, grid=(n_b, n_pages // pps_a),
            in_specs=[head_spec, head_spec, head_spec,
                      pl.BlockSpec((1, n_rows, n_cols), lambda b, s, pt_ref: (b, 0, 0))]
            + [pl.BlockSpec((None, 1, n_cols, HEAD_DIM), page_map(r, pps_a)) for r in range(pps_a)]
            + [pl.BlockSpec((None, 1, n_cols, HEAD_DIM), page_map(r, pps_a)) for r in range(pps_a)],
            out_specs=head_spec,
            scratch_shapes=[pltpu.VMEM((N_HEADS, 1), _F32), pltpu.VMEM((N_HEADS, 1), _F32),
                            pltpu.VMEM((N_HEADS, HEAD_DIM), _F32)]),
        out_shape=jax.ShapeDtypeStruct((n_b, N_HEADS, HEAD_DIM), _F32),
        compiler_params=_params(2), name="attn_sample")(
            pt, q3, kn, vn, bias, *([pk] * pps_a), *([pv] * pps_a))
    return out.reshape(n_b, A_Q)


def _gmlp_kernel(u_ref, v_ref, g_ref, b_ref, ws_ref, bs_ref, o_ref, vn_ref=None):
    v = v_ref[...]
    mu = jnp.mean(v, axis=1, keepdims=True)
    vc = v - mu
    var = jnp.mean(vc * vc, axis=1, keepdims=True)
    vn = vc * lax.rsqrt(var + LN_EPS) * g_ref[...] + b_ref[...]
    if vn_ref is not None:
        vn_ref[...] = vn
    gw = GMLP_WIDTH // GMLP_GROUPS
    tril = (lax.broadcasted_iota(jnp.int32, (CHUNK, CHUNK), 0)
            >= lax.broadcasted_iota(jnp.int32, (CHUNK, CHUNK), 1))
    bs = bs_ref[...]
    for g in range(GMLP_GROUPS):
        wm = jnp.where(tril, ws_ref[g], 0.0).astype(_BF16)
        mixed = jnp.dot(wm, vn[:, g * gw:(g + 1) * gw].astype(_BF16), preferred_element_type=_F32)
        mixed = mixed + bs[:, g:g + 1]
        o_ref[:, g * gw:(g + 1) * gw] = (u_ref[:, g * gw:(g + 1) * gw] * mixed).astype(_BF16)


def _gmlp(hb, ln_g, ln_b, ws, bs, want_vn, name):
    m = hb.shape[0]
    w = GMLP_WIDTH
    out_shape = [jax.ShapeDtypeStruct((m, w), _BF16)]
    out_specs = [pl.BlockSpec((CHUNK, w), lambda c: (c, 0))]
    if want_vn:
        out_shape.append(jax.ShapeDtypeStruct((m, w), _F32))
        out_specs.append(pl.BlockSpec((CHUNK, w), lambda c: (c, 0)))
    return pl.pallas_call(
        _gmlp_kernel, grid=(m // CHUNK,),
        in_specs=[pl.BlockSpec((CHUNK, w), lambda c: (c, 0)), pl.BlockSpec((CHUNK, w), lambda c: (c, 1)),
                  pl.BlockSpec((1, w), lambda c: (0, 0)), pl.BlockSpec((1, w), lambda c: (0, 0)),
                  pl.BlockSpec((GMLP_GROUPS, CHUNK, CHUNK), lambda c: (0, 0, 0)),
                  pl.BlockSpec((CHUNK, GMLP_GROUPS), lambda c: (0, 0))],
        out_specs=out_specs, out_shape=out_shape, compiler_params=_params(1), name=name)(
            hb, hb, ln_g.reshape(1, w), ln_b.reshape(1, w), ws, jnp.transpose(bs))


def _head_sums(x, blk):
    outs = []
    for c in range(x.shape[1] // LANES):
        xc = x[:, c * LANES:(c + 1) * LANES]
        hi = xc.astype(_BF16)
        lo = (xc - hi.astype(_F32)).astype(_BF16)
        outs.append(jnp.dot(hi, blk, preferred_element_type=_F32) + jnp.dot(lo, blk, preferred_element_type=_F32))
    return jnp.concatenate(outs, axis=1)


def _rwkv_pre_kernel(pc_ref, sh_ref, prev_ref, mu_ref, w0_ref, w2_ref, a0_ref, a2_ref,
                     r_ref, w_ref, k_ref, v_ref, a_ref, *, tr, tiles_per_batch, roll_shift):
    pc = pc_ref[...]
    if roll_shift:
        first = pl.program_id(0) % tiles_per_batch == 0
        before = jnp.where(first, prev_ref[0], sh_ref[7:8, :])
        rolled = pltpu.roll(pc, 1, 0)
        rowi = lax.broadcasted_iota(jnp.int32, pc.shape, 0)
        shifted = jnp.where(rowi == 0, before, rolled)
    else:
        shifted = sh_ref[...]
    y = pc + mu_ref[...] * (shifted - pc)
    wdt = RWKV_WIDTH
    r = y[:, 0:wdt]
    k = y[:, wdt:2 * wdt]
    v = y[:, 2 * wdt:3 * wdt]
    wd = y[:, 3 * wdt:3 * wdt + D_DECAY]
    ad = y[:, 3 * wdt + D_DECAY:3 * wdt + D_DECAY + D_AAA]
    z = -(w0_ref[...] + jnp.dot(jnp.tanh(wd).astype(_BF16), w2_ref[...].astype(_BF16),
                                preferred_element_type=_F32))
    softplus = jnp.maximum(z, 0.0) + jnp.log(1.0 + jnp.exp(-jnp.abs(z)))
    decay = jnp.exp(-jnp.exp(-softplus - 0.5))
    a = jax.nn.sigmoid(a0_ref[...] + jnp.dot(ad.astype(_BF16), a2_ref[...].astype(_BF16),
                                             preferred_element_type=_F32))
    r_ref[...] = r
    w_ref[...] = decay
    k_ref[...] = k
    v_ref[...] = v
    a_ref[...] = a


def _seg_blk():
    seg = jnp.arange(LANES) // RWKV_HEAD
    return (seg[:, None] == seg[None, :]).astype(_BF16)


def _rwkv_pre(hc, shift_src, lp, seq, roll_shift, name):
    m = hc.shape[0]
    tr = 256 if (roll_shift and seq % 256 == 0) else (seq if roll_shift else m)
    tpb = seq // tr if roll_shift else 1
    wdt = RWKV_WIDTH
    if roll_shift:
        n_b = m // seq
        sh_spec = pl.BlockSpec((8, C_COLS), lambda i: (jnp.maximum(i * (tr // 8) - 1, 0), 0))
        prev = shift_src.reshape(n_b, 1, C_COLS)
        prev_spec = pl.BlockSpec((1, 1, C_COLS), lambda i: (i // tpb, 0, 0))
        sh_arg = hc
    else:
        sh_spec = pl.BlockSpec((tr, C_COLS), lambda i: (i, 0))
        prev = jnp.zeros((1, 1, C_COLS), _F32)
        prev_spec = pl.BlockSpec((1, 1, C_COLS), lambda i: (0, 0, 0))
        sh_arg = shift_src

    def vec(n):
        return pl.BlockSpec((1, n), lambda i: (0, 0))

    kern = functools.partial(_rwkv_pre_kernel, tr=tr, tiles_per_batch=tpb, roll_shift=roll_shift)
    return pl.pallas_call(
        kern, grid=(m // tr,),
        in_specs=[pl.BlockSpec((tr, C_COLS), lambda i: (i, 0)), sh_spec, prev_spec, vec(C_COLS), vec(wdt),
                  pl.BlockSpec((D_DECAY, wdt), lambda i: (0, 0)), vec(wdt),
                  pl.BlockSpec((D_AAA, wdt), lambda i: (0, 0))],
        out_specs=[pl.BlockSpec((tr, wdt), lambda i: (i, 0))] * 5,
        out_shape=[jax.ShapeDtypeStruct((m, wdt), _F32)] * 5,
        compiler_params=_params(1), name=name)(
            hc, sh_arg, prev, lp["rwkv_mu"].reshape(1, C_COLS), lp["rwkv_w0"].reshape(1, wdt), lp["rwkv_w2"],
            lp["rwkv_a0"].reshape(1, wdt), lp["rwkv_a2"])


SCAN_BATCH = LANES // RWKV_HEADS
HEAD_PAIRS = RWKV_HEADS // 2


def _rwkv_scan_kernel(*refs, tt, halves):
    cur = refs[0:5]
    nxt = refs[5:10]
    kkp_ref, ka_ref, s0_ref, o_ref, st_ref = refs[10:15]
    z = (refs[15:21], refs[21:27])
    zo = refs[27]
    _scan_block(cur, nxt, kkp_ref, ka_ref, s0_ref, o_ref, st_ref, z, zo, pl.program_id(0), tt, halves)


def _scan_block(cur, nxt, kkp_ref, ka_ref, s0_ref, o_ref, st_ref, z, zo, step_idx, tt, halves):
    n = RWKV_HEAD

    def to_lanes(srcs, off, dsts, t):
        def lanes(src):
            x = jnp.concatenate([src[bb, off + t] for bb in range(SCAN_BATCH)], axis=0)
            x2 = jnp.concatenate([x, pltpu.roll(x, LANES // 2, 1)], axis=0)
            return x2.T[:n]
        r, w, k, v, a = [lanes(src) for src in srcs]
        kk = k * kkp_ref[...]
        kk = kk / jnp.maximum(jnp.sqrt(jnp.sum(kk * kk, axis=0, keepdims=True)), 1e-12)
        zr, zw, zk, zv, zkk, zb = dsts
        zr[t] = r
        zw[t] = w
        zk[t] = k * (1.0 + (a - 1.0) * ka_ref[...])
        zv[t] = v
        zkk[t] = kk
        zb[t] = kk * a

    def from_lanes(t):
        y = zo[t]
        yt = jnp.concatenate([y, pltpu.roll(y, LANES // 2, 1)], axis=0).T[:LANES // 2]
        for bb in range(SCAN_BATCH):
            o_ref[bb, t] = yt[bb * HEAD_PAIRS:(bb + 1) * HEAD_PAIRS]

    def update(zs, t, t_out):
        zr, zw, zk, zv, zkk, zb = zs
        acc = st_ref[0] * zkk[t, pl.ds(0, 1), :]
        for j in range(1, n):
            acc = acc + st_ref[j] * zkk[t, pl.ds(j, 1), :]
        sa = -acc
        vt = zv[t]
        out = None
        for j in range(n):
            sj = st_ref[j] * zw[t, pl.ds(j, 1), :] + sa * zb[t, pl.ds(j, 1), :] + vt * zk[t, pl.ds(j, 1), :]
            st_ref[j] = sj
            term = sj * zr[t, pl.ds(j, 1), :]
            out = term if out is None else out + term
        zo[t_out] = out

    def fill_first():
        def body(t, carry):
            to_lanes(cur, 0, z[0], t)
            return carry
        lax.fori_loop(0, tt, body, 0)

    if halves == 1:
        @pl.when(step_idx == 0)
        def _():
            st_ref[...] = s0_ref[...]

        fill_first()

        def body(t, carry):
            update(z[0], t, t)
            return carry

        lax.fori_loop(0, tt, body, 0)

        def out_body(t, carry):
            from_lanes(t)
            return carry

        lax.fori_loop(0, tt, out_body, 0)
        return

    @pl.when(step_idx == 0)
    def _():
        st_ref[...] = s0_ref[...]
        zo[...] = jnp.zeros(zo.shape, _F32)
        fill_first()

    def first_half(t, carry):
        to_lanes(cur, tt, z[1], t)
        from_lanes(jnp.maximum(t - 1, 0))
        update(z[0], t, t)
        return carry

    def second_half(t, carry):
        to_lanes(nxt, 0, z[0], t)
        from_lanes(tt + t - 1)
        update(z[1], t, tt + t)
        return carry

    lax.fori_loop(0, tt, first_half, 0)
    lax.fori_loop(0, tt, second_half, 0)
    from_lanes(2 * tt - 1)


def _head_param_to_lanes(p):
    q = jnp.transpose(p.reshape(HEAD_PAIRS, 2, RWKV_HEAD), (2, 1, 0))
    return jnp.broadcast_to(q[:, :, None, :], (RWKV_HEAD, 2, SCAN_BATCH, HEAD_PAIRS)).reshape(RWKV_HEAD, LANES)


def _rwkv_scan(vecs, s0, lp, seq, name):
    n = RWKV_HEAD
    halves, tt = (2, 8) if seq % 16 == 0 else (1, seq)
    blk = halves * tt
    n_half = seq // tt
    shape4 = (SCAN_BATCH, seq, HEAD_PAIRS, LANES)
    vspec = pl.BlockSpec((SCAN_BATCH, blk, HEAD_PAIRS, LANES), lambda s: (0, s, 0, 0))
    nspec = pl.BlockSpec((SCAN_BATCH, tt, HEAD_PAIRS, LANES),
                         lambda s: (0, jnp.minimum(halves * (s + 1), n_half - 1), 0, 0))
    sspec = pl.BlockSpec((n, n, LANES), lambda s: (0, 0, 0))
    pspec = pl.BlockSpec((n, LANES), lambda s: (0, 0))
    args = [x.reshape(shape4) for x in vecs]
    o, st = pl.pallas_call(
        functools.partial(_rwkv_scan_kernel, tt=tt, halves=halves), grid=(seq // blk,),
        in_specs=[vspec] * 5 + [nspec] * 5 + [pspec, pspec, sspec], out_specs=[vspec, sspec],
        out_shape=[jax.ShapeDtypeStruct(shape4, _F32), jax.ShapeDtypeStruct((n, n, LANES), _F32)],
        scratch_shapes=[pltpu.VMEM((tt, n, LANES), _F32)] * 12 + [pltpu.VMEM((blk, n, LANES), _F32)],
        compiler_params=_params(1), name=name)(
            *args, *args, _head_param_to_lanes(lp["rwkv_kk"]), _head_param_to_lanes(lp["rwkv_ka"]), s0)
    return o.reshape(SCAN_BATCH * seq, RWKV_WIDTH), st


def _state_to_lanes(s):
    s5 = s.reshape(SCAN_BATCH, HEAD_PAIRS, 2, RWKV_HEAD, RWKV_HEAD)
    return jnp.transpose(s5, (4, 3, 2, 0, 1)).reshape(RWKV_HEAD, RWKV_HEAD, LANES)


def _state_from_lanes(s):
    s5 = s.reshape(RWKV_HEAD, RWKV_HEAD, 2, SCAN_BATCH, HEAD_PAIRS)
    return jnp.transpose(s5, (3, 4, 2, 1, 0)).reshape(SCAN_BATCH, RWKV_HEADS, RWKV_HEAD, RWKV_HEAD)


def _rwkv_post_kernel(o_ref, r_ref, k_ref, v_ref, a_ref, g_ref, b_ref, rk_ref, ka_ref, blk_ref, out_ref):
    blk = blk_ref[...]
    o = o_ref[...]
    inv = 1.0 / RWKV_HEAD
    m = _head_sums(o, blk) * inv
    oc = o - m
    var = _head_sums(oc * oc, blk) * inv
    on = oc * lax.rsqrt(var + GN_EPS) * g_ref[...] + b_ref[...]
    k = k_ref[...] * (1.0 + (a_ref[...] - 1.0) * ka_ref[...])
    bonus = _head_sums(r_ref[...] * k * rk_ref[...], blk) * v_ref[...]
    out_ref[...] = (on + bonus).astype(_BF16)


def _rwkv_post(o, r, k, v, a, lp, name):
    m = o.shape[0]
    tr = 256 if m % 256 == 0 else m
    wdt = RWKV_WIDTH
    row = pl.BlockSpec((tr, wdt), lambda i: (i, 0))
    vec = pl.BlockSpec((1, wdt), lambda i: (0, 0))
    return pl.pallas_call(
        _rwkv_post_kernel, grid=(m // tr,),
        in_specs=[row] * 5 + [vec] * 4 + [pl.BlockSpec((LANES, LANES), lambda i: (0, 0))],
        out_specs=row, out_shape=jax.ShapeDtypeStruct((m, wdt), _BF16),
        compiler_params=_params(1), name=name)(
            o, r, k, v, a, lp["rwkv_gn_g"].reshape(1, wdt), lp["rwkv_gn_b"].reshape(1, wdt),
            lp["rwkv_rk"].reshape(1, wdt), lp["rwkv_ka"].reshape(1, wdt), _seg_blk())


def _pad_rows(x):
    return jnp.zeros((SAMPLE_ROWS,) + x.shape[1:], x.dtype).at[:x.shape[0]].set(x)


def _layer(xp, xpb, xs, xsb, seq, n_dec, layer, lp, big, cache_k, cache_v, cache_ik, page_table,
           st_shift, st_wkv):
    mp = xp.shape[0]
    n_batch = mp // seq
    tm = 1024 if seq % 1024 == 0 else seq
    d = D_MODEL
    rope_p = _rope_tables(jnp.arange(seq, dtype=jnp.int32))
    rope_s = _rope_tables(jnp.full((SAMPLE_ROWS,), PAST_LEN, jnp.int32))
    rope_extras = [dict(p=a, s=b) for a, b in zip(rope_p, rope_s)]
    lhs_x = [(xpb, xsb)]

    def rope_ep(accs, tex, rex):
        return [_rope_tiles(accs[0], *rex)]

    def pair(w, ncols, wcol, kb=None, kidx=0, lhs=0):
        return dict(lhs=lhs, w=w, layer=layer, kb=w.shape[1] if kb is None else kb, kidx=kidx, ncols=ncols,
                    wcol=wcol)

    def pair_in(ncols, row0):
        n_in = big["w_in_t"].shape[0] // big["depth"]
        return dict(lhs=0, w=big["w_in_t"], layer=layer, kb=d, kidx=0, ncols=ncols,
                    row0=lambda j: pl.multiple_of(layer * n_in + row0(j) + 0 * j, 8))

    tn = 512
    tn_w = 1024
    tm_w = 512 if seq % 512 == 0 else seq
    (q_hm, q_s), = _ws_matmul(
        "proj_q", lhs_x, [pair_in(tn_w, lambda j: j * tn_w)],
        [dict(ncols=tn_w, dtype=_BF16, col=lambda j: j, total=A_Q, hm=True)],
        rope_ep, tm=tm_w, n_steps=A_Q // tn_w, seq=seq, row_extras=rope_extras)
    qi_start = A_Q + 2 * A_KV
    (qi_hm, qi_s), = _ws_matmul(
        "proj_qi", lhs_x, [pair_in(tn_w, lambda j: qi_start + j * tn_w)],
        [dict(ncols=tn_w, dtype=_BF16, col=lambda j: j, total=IDX_HEADS * IDX_DIM, hm=True)],
        rope_ep, tm=tm_w, n_steps=IDX_HEADS * IDX_DIM // tn_w, seq=seq, row_extras=rope_extras)

    ki_start = qi_start + IDX_HEADS * IDX_DIM

    def k_ep(accs, tex, rex):
        kr = _rope_tiles(accs[0], *rex)
        kir = _rope_tiles(accs[1], *rex)
        return [kr, kr, kir, kir]

    zero = lambda j: 0
    (k_f, k_fs), (k_hm, _), (ki_f, ki_fs), (ki_b, _) = _ws_matmul(
        "proj_k", lhs_x,
        [pair_in(A_KV, lambda j: A_Q), pair_in(IDX_DIM, lambda j: ki_start)],
        [dict(ncols=A_KV, dtype=_F32, col=zero, total=A_KV),
         dict(ncols=A_KV, dtype=_BF16, col=zero, total=A_KV, hm=True),
         dict(ncols=IDX_DIM, dtype=_F32, col=zero, total=IDX_DIM),
         dict(ncols=IDX_DIM, dtype=_BF16, col=zero, total=IDX_DIM)],
        k_ep, tm=tm, n_steps=1, seq=seq, row_extras=rope_extras)

    def v_ep(accs, tex, rex):
        return [accs[0], accs[0], accs[1]]

    (v_f, v_fs), (v_hm, _), (wi_f, wi_fs) = _ws_matmul(
        "proj_v", lhs_x,
        [pair_in(A_KV, lambda j: A_Q + A_KV), pair_in(LANES, lambda j: ki_start + IDX_DIM)],
        [dict(ncols=A_KV, dtype=_F32, col=zero, total=A_KV),
         dict(ncols=A_KV, dtype=_BF16, col=zero, total=A_KV, hm=True),
         dict(ncols=LANES, dtype=_F32, col=zero, total=LANES)],
        v_ep, tm=tm, n_steps=1, seq=seq)

    ident = lambda accs, tex, rex: [accs[0]]
    (hb, hb_s), = _ws_matmul(
        "proj_b", lhs_x, [pair_in(tn_w, lambda j: A_COLS + j * tn_w)],
        [dict(ncols=tn_w, dtype=_F32, col=lambda j: j, total=B_COLS)],
        ident, tm=tm_w, n_steps=B_COLS // tn_w, seq=seq)
    tn_c = 640
    (hc, hc_s), = _ws_matmul(
        "proj_c", lhs_x, [pair_in(tn_c, lambda j: A_COLS + B_COLS + j * tn_c)],
        [dict(ncols=tn_c, dtype=_F32, col=lambda j: j, total=C_COLS)],
        ident, tm=tm, n_steps=C_COLS // tn_c, seq=seq)
    (gate, gate_s), = _ws_matmul(
        "proj_g", lhs_x, [pair_in(tn_w, lambda j: A_COLS + B_COLS + C_COLS + j * tn_w)],
        [dict(ncols=tn_w, dtype=_BF16, col=lambda j: j, total=G_COLS)],
        lambda accs, tex, rex: [jax.nn.sigmoid(accs[0])], tm=tm_w, n_steps=G_COLS // tn_w, seq=seq)
    out_a = _attn_prompt(q_hm, qi_hm, ki_b, wi_f, k_hm, v_hm, n_batch, seq)
    out_a_s = _attn_sample(q_s, qi_s, wi_fs, ki_fs, k_fs, v_fs, cache_k, cache_v, cache_ik, page_table, layer)
    out_a_s = _pad_rows(out_a_s.astype(_BF16))

    out_b, = _gmlp(hb, lp["gmlp_ln_g"], lp["gmlp_ln_b"], lp["gmlp_ws"], lp["gmlp_bs"], False, "gmlp_prompt")
    hb_chunks = jnp.zeros((n_dec, CHUNK, B_COLS), _F32).at[:, 0].set(hb_s[:n_dec]).reshape(n_dec * CHUNK, B_COLS)
    ob_s, vn_s = _gmlp(hb_chunks, lp["gmlp_ln_g"], lp["gmlp_ln_b"], lp["gmlp_ws"], lp["gmlp_bs"], True,
                       "gmlp_sample")
    out_b_s = _pad_rows(ob_s.reshape(n_dec, CHUNK, GMLP_WIDTH)[:, 0])
    vn_rows = vn_s.reshape(n_dec, CHUNK, GMLP_WIDTH)[:, 0]

    pre_p = _rwkv_pre(hc, jnp.zeros((n_batch, C_COLS), _F32), lp, seq, True, "rwkv_pre_prompt")
    zero_state = jnp.zeros((RWKV_HEAD, RWKV_HEAD, LANES), _F32)
    o_parts, st_parts = [], []
    for b0 in range(0, n_batch, SCAN_BATCH):
        rows = slice(b0 * seq, (b0 + SCAN_BATCH) * seq)
        o_c, st_c = _rwkv_scan([x[rows] for x in pre_p], zero_state, lp, seq, "rwkv_scan_prompt")
        o_parts.append(o_c)
        st_parts.append(_state_from_lanes(st_c))
    o_p = o_parts[0] if len(o_parts) == 1 else jnp.concatenate(o_parts, axis=0)
    wkv_p = st_parts[0] if len(st_parts) == 1 else jnp.concatenate(st_parts, axis=0)
    out_c = _rwkv_post(o_p, pre_p[0], pre_p[2], pre_p[3], pre_p[4], lp, "rwkv_post_prompt")

    pre_s = _rwkv_pre(hc_s, _pad_rows(st_shift), lp, 1, False, "rwkv_pre_sample")
    o_parts, st_parts = [], []
    for b0 in range(0, n_dec, SCAN_BATCH):
        o_c, st_c = _rwkv_scan([x[b0:b0 + SCAN_BATCH] for x in pre_s],
                               _state_to_lanes(st_wkv[b0:b0 + SCAN_BATCH]), lp, 1, "rwkv_scan_sample")
        o_parts.append(o_c)
        st_parts.append(_state_from_lanes(st_c))
    o_s = _pad_rows(jnp.concatenate(o_parts, axis=0))
    wkv_s = jnp.concatenate(st_parts, axis=0)
    out_c_s = _rwkv_post(o_s, pre_s[0], pre_s[2], pre_s[3], pre_s[4], lp, "rwkv_post_sample")

    tm_m = 512 if seq % 512 == 0 else seq
    n_g = d // tn

    def merge_ep(accs, tex, rex):
        return [tex[0].astype(_F32) * accs[0] + tex[1].astype(_F32) * accs[1] + tex[2].astype(_F32) * accs[2]]

    (merged, merged_s), = _ws_matmul(
        "merge", [(out_a, out_a_s), (out_b, out_b_s), (out_c, out_c_s)],
        [pair(big["attn_wo"], tn, lambda j: j, lhs=0), pair(big["gmlp_wo"], tn, lambda j: j, lhs=1),
         pair(big["rwkv_wo"], tn, lambda j: j, lhs=2)],
        [dict(ncols=tn, dtype=_BF16, col=lambda j: j, total=d)],
        merge_ep, tm=tm_m, n_steps=d // tn, seq=seq,
        tile_extras=[dict(p=gate, s=gate_s, ncols=tn, col=lambda j, a=a: j + a * n_g) for a in range(3)])

    def resid_ep(accs, tex, rex):
        return [ALPHA * tex[0] + accs[0]]

    (pre1, pre1_s), = _ws_matmul(
        "out_proj", [(merged, merged_s)], [pair(big["w_out"], tn, lambda j: j)],
        [dict(ncols=tn, dtype=_F32, col=lambda j: j, total=d)],
        resid_ep, tm=tm, n_steps=d // tn, seq=seq,
        tile_extras=[dict(p=xp, s=xs, ncols=tn, col=lambda j: j)])
    x1, x1b = _layernorm(pre1, lp["ln1_g"], lp["ln1_b"], "ln1_prompt")
    x1_s, x1b_s = _layernorm(pre1_s, lp["ln1_g"], lp["ln1_b"], "ln1_sample")

    tn_f = 256
    n_f = D_FF // tn_f

    def swiglu_ep(accs, tex, rex):
        return [jax.nn.silu(accs[0]) * accs[1]]

    (act, act_s), = _ws_matmul(
        "ffn_in", [(x1b, x1b_s)],
        [pair(big["ffn_w_in"], tn_f, lambda j: j), pair(big["ffn_w_in"], tn_f, lambda j: j + n_f)],
        [dict(ncols=tn_f, dtype=_BF16, col=lambda j: j, total=D_FF)],
        swiglu_ep, tm=tm, n_steps=n_f, seq=seq)

    kb = D_FF // 2
    part, part_s, scale = x1, x1_s, ALPHA
    for half in range(2):
        def acc_ep(accs, tex, rex, scale=scale):
            return [scale * tex[0] + accs[0]]

        (part, part_s), = _ws_matmul(
            "ffn_out%d" % half, [(act, act_s)],
            [pair(big["ffn_w_out"], tn, lambda j: j, kb=kb, kidx=half)],
            [dict(ncols=tn, dtype=_F32, col=lambda j: j, total=d)],
            acc_ep, tm=tm_m, n_steps=d // tn, seq=seq,
            tile_extras=[dict(p=part, s=part_s, ncols=tn, col=lambda j: j)])
        scale = 1.0
    x2, x2b = _layernorm(part, lp["ln2_g"], lp["ln2_b"], "ln2_prompt")
    x2_s, x2b_s = _layernorm(part_s, lp["ln2_g"], lp["ln2_b"], "ln2_sample")

    caches = dict(
        k_p=k_f.reshape(n_batch, seq, KV_HEADS, HEAD_DIM), v_p=v_f.reshape(n_batch, seq, KV_HEADS, HEAD_DIM),
        ik_p=ki_f.reshape(n_batch, seq, IDX_DIM), wkv_p=wkv_p,
        sh_p=hc.reshape(n_batch, seq, C_COLS)[:, -1],
        k_s=k_fs[:n_dec].reshape(n_dec, 1, KV_HEADS, HEAD_DIM), v_s=v_fs[:n_dec].reshape(n_dec, 1, KV_HEADS, HEAD_DIM),
        ik_s=ki_fs[:n_dec].reshape(n_dec, 1, IDX_DIM), wkv_s=wkv_s, sh_s=hc_s[:n_dec],
        gv_s=vn_rows.reshape(n_dec, 1, GMLP_WIDTH))
    return x2, x2b, x2_s, x2b_s, caches


def kernel(x_prompt, x_sample, cache_k, cache_v, cache_idx_k, page_table, state_wkv, state_shift, w_in, attn_wo, gmlp_ln_g, gmlp_ln_b, gmlp_ws, gmlp_bs, gmlp_wo, rwkv_mu, rwkv_w0, rwkv_w2, rwkv_a0, rwkv_a2, rwkv_kk, rwkv_ka, rwkv_rk, rwkv_gn_g, rwkv_gn_b, rwkv_wo, w_out, ln1_g, ln1_b, ffn_w_in, ffn_w_out, ln2_g, ln2_b):
    n_batch, seq, d = x_prompt.shape
    n_dec = x_sample.shape[0]
    depth = w_in.shape[0]
    assert x_sample.shape[1] == 1 and n_dec <= SAMPLE_ROWS and n_dec % SCAN_BATCH == 0
    assert n_batch % SCAN_BATCH == 0 and d == D_MODEL and seq % CHUNK == 0
    xp = x_prompt.reshape(n_batch * seq, d)
    xs = _pad_rows(x_sample.reshape(n_dec, d))
    xpb, xsb = xp.astype(_BF16), xs.astype(_BF16)
    big = dict(w_in_t=jnp.swapaxes(w_in, 1, 2).reshape(depth * w_in.shape[2], d), depth=depth, attn_wo=attn_wo, gmlp_wo=gmlp_wo, rwkv_wo=rwkv_wo, w_out=w_out,
               ffn_w_in=ffn_w_in, ffn_w_out=ffn_w_out)
    per_layer = []
    for l in range(depth):
        lp = dict(gmlp_ln_g=gmlp_ln_g[l], gmlp_ln_b=gmlp_ln_b[l], gmlp_ws=gmlp_ws[l], gmlp_bs=gmlp_bs[l],
                  rwkv_mu=rwkv_mu[l], rwkv_w0=rwkv_w0[l], rwkv_w2=rwkv_w2[l], rwkv_a0=rwkv_a0[l],
                  rwkv_a2=rwkv_a2[l], rwkv_kk=rwkv_kk[l], rwkv_ka=rwkv_ka[l], rwkv_rk=rwkv_rk[l],
                  rwkv_gn_g=rwkv_gn_g[l], rwkv_gn_b=rwkv_gn_b[l], ln1_g=ln1_g[l], ln1_b=ln1_b[l],
                  ln2_g=ln2_g[l], ln2_b=ln2_b[l])
        xp, xpb, xs, xsb, c = _layer(xp, xpb, xs, xsb, seq, n_dec, l, lp, big, cache_k, cache_v, cache_idx_k,
                                     page_table, state_shift[l], state_wkv[l])
        per_layer.append(c)

    def stack(name):
        return jnp.stack([c[name] for c in per_layer])

    return (xp.reshape(n_batch, seq, d), xs[:n_dec].reshape(n_dec, 1, d),
            stack("k_p"), stack("v_p"), stack("ik_p"), stack("wkv_p"), stack("sh_p"),
            stack("k_s"), stack("v_s"), stack("ik_s"), stack("wkv_s"), stack("sh_s"), stack("gv_s"))
```

```python
import functools

import jax
import jax.numpy as jnp
from jax import lax
from jax.experimental import pallas as pl
from jax.experimental.pallas import tpu as pltpu

D_MODEL = 4096
PAST_LEN = 16384
PAGE_SIZE = 128
N_HEADS = 16
KV_HEADS = 4
HEAD_DIM = 128
ROT_DIM = HEAD_DIM // 4
ROPE_THETA = 500000.0
IDX_HEADS = 32
IDX_DIM = 128
TOPK_MAX = 256
A_Q = N_HEADS * HEAD_DIM
A_KV = KV_HEADS * HEAD_DIM
CHUNK = 128
GMLP_WIDTH = D_MODEL // 2
GMLP_GROUPS = 8
RWKV_HEAD = 64
RWKV_WIDTH = D_MODEL // 2
RWKV_HEADS = RWKV_WIDTH // RWKV_HEAD
D_DECAY = max(32, int(round(D_MODEL ** 0.5 * 1.8 / 32)) * 32)
D_AAA = D_DECAY
D_FF = ((8 * D_MODEL // 3 + 255) // 256) * 256
A_COLS = A_Q + 2 * A_KV + IDX_HEADS * IDX_DIM + IDX_DIM + IDX_HEADS
B_COLS = 2 * GMLP_WIDTH
C_COLS = 3 * RWKV_WIDTH + D_DECAY + D_AAA
G_COLS = 3 * D_MODEL
DEPTH = 2
ALPHA = (2 * DEPTH) ** 0.25
LN_EPS = 1e-5
GN_EPS = 64e-5

LANES = 128
SAMPLE_ROWS = 16
VMEM_LIMIT = 58 * 1024 * 1024
KEY_CHUNK = 512
INT_MIN = -(2 ** 31)
NEG_BIG = -1e30
LOG2E = 1.4426950408889634

_F32 = jnp.float32
_BF16 = jnp.bfloat16
_NT = (((1,), (1,)), ((), ()))


def _params(n_grid):
    return pltpu.CompilerParams(dimension_semantics=("arbitrary",) * n_grid,
                                vmem_limit_bytes=VMEM_LIMIT)


def _ws_matmul(name, lhs, pairs, outs, epilogue, *, tm, n_steps, seq, tile_extras=(), row_extras=()):
    mp = lhs[0][0].shape[0]
    n_i = mp // tm
    tpb = seq // tm
    n_batch = mp // seq

    in_specs, args = [], []
    for a, (xp, xs) in enumerate(lhs):
        kb, kidx = [(p["kb"], p["kidx"]) for p in pairs if p["lhs"] == a][0]
        in_specs += [pl.BlockSpec((tm, kb), lambda j, i, kidx=kidx: (i, kidx)),
                     pl.BlockSpec((SAMPLE_ROWS, kb), lambda j, i, kidx=kidx: (0, kidx))]
        args += [xp, xs]
    for p in pairs:
        if p.get("row0") is not None:
            in_specs.append(pl.BlockSpec((pl.Element(p["ncols"]), pl.Element(p["kb"])),
                                         lambda j, i, p=p: (p["row0"](j), 0)))
        else:
            in_specs.append(pl.BlockSpec((None, p["kb"], p["ncols"]),
                                         lambda j, i, p=p: (p["layer"], p["kidx"], p["wcol"](j))))
        args.append(p["w"])
    for e in tile_extras:
        in_specs += [pl.BlockSpec((tm, e["ncols"]), lambda j, i, e=e: (i, e["col"](j))),
                     pl.BlockSpec((SAMPLE_ROWS, e["ncols"]), lambda j, i, e=e: (0, e["col"](j)))]
        args += [e["p"], e["s"]]
    for e in row_extras:
        in_specs += [pl.BlockSpec((tm, LANES), lambda j, i: (i % tpb, 0)),
                     pl.BlockSpec((SAMPLE_ROWS, LANES), lambda j, i: (0, 0))]
        args += [e["p"], e["s"]]

    out_specs, out_shapes = [], []
    for o in outs:
        if o.get("hm"):
            hpt = o["ncols"] // LANES
            out_shapes.append(jax.ShapeDtypeStruct((n_batch, o["total"] // LANES, seq, LANES), o["dtype"]))
            out_specs.append(pl.BlockSpec((1, hpt, tm, LANES),
                                          lambda j, i, o=o: (i // tpb, o["col"](j), i % tpb, 0)))
        else:
            out_shapes.append(jax.ShapeDtypeStruct((mp, o["total"]), o["dtype"]))
            out_specs.append(pl.BlockSpec((tm, o["ncols"]), lambda j, i, o=o: (i, o["col"](j))))
        out_shapes.append(jax.ShapeDtypeStruct((SAMPLE_ROWS, o["total"]), o["dtype"]))
        out_specs.append(pl.BlockSpec((SAMPLE_ROWS, o["ncols"]), lambda j, i, o=o: (0, o["col"](j))))

    scratch = [pltpu.VMEM((p["kb"], p["ncols"]), _BF16) for p in pairs]
    n_lhs, n_pairs, n_te, n_re, n_out = len(lhs), len(pairs), len(tile_extras), len(row_extras), len(outs)

    def kernel(*refs):
        pos = 0
        x_refs = [(refs[pos + 2 * a], refs[pos + 2 * a + 1]) for a in range(n_lhs)]
        pos += 2 * n_lhs
        w_refs = refs[pos:pos + n_pairs]
        pos += n_pairs
        te_refs = [(refs[pos + 2 * a], refs[pos + 2 * a + 1]) for a in range(n_te)]
        pos += 2 * n_te
        re_refs = [(refs[pos + 2 * a], refs[pos + 2 * a + 1]) for a in range(n_re)]
        pos += 2 * n_re
        o_refs = [(refs[pos + 2 * a], refs[pos + 2 * a + 1]) for a in range(n_out)]
        pos += 2 * n_out
        wb_refs = refs[pos:pos + n_pairs]
        i = pl.program_id(1)

        def run(which):
            accs = [jnp.dot(x_refs[p["lhs"]][which][...], wb[...], preferred_element_type=_F32)
                    for p, wb in zip(pairs, wb_refs)]
            res = epilogue(accs, [t[which][...] for t in te_refs], [r[which][...] for r in re_refs])
            for o, (op_ref, os_ref), val in zip(outs, o_refs, res):
                if which == 0 and o.get("hm"):
                    for h in range(o["ncols"] // LANES):
                        op_ref[0, h] = val[:, h * LANES:(h + 1) * LANES].astype(o["dtype"])
                else:
                    (op_ref if which == 0 else os_ref)[...] = val.astype(o["dtype"])

        @pl.when(i == 0)
        def _():
            for p, w_ref, wb in zip(pairs, w_refs, wb_refs):
                if p.get("row0") is not None:
                    for kc in range(p["kb"] // LANES):
                        blk = w_ref[:, kc * LANES:(kc + 1) * LANES]
                        wb[kc * LANES:(kc + 1) * LANES, :] = blk.T.astype(_BF16)
                    continue
                rows = 256 if p["kb"] % 256 == 0 else 128

                def cast(c, carry, w_ref=w_ref, wb=wb, rows=rows):
                    r0 = pl.multiple_of(c * rows, rows)
                    wb[pl.ds(r0, rows), :] = w_ref[pl.ds(r0, rows), :].astype(_BF16)
                    return carry

                lax.fori_loop(0, p["kb"] // rows, cast, 0)
            run(1)

        run(0)

    res = pl.pallas_call(
        kernel, grid=(n_steps, n_i), in_specs=in_specs, out_specs=out_specs, out_shape=out_shapes,
        scratch_shapes=scratch, compiler_params=_params(2), name=name)(*args)
    return [(res[2 * a], res[2 * a + 1]) for a in range(n_out)]


def _rope_tiles(x, c, s1, s2):
    outs = []
    for h in range(x.shape[1] // LANES):
        xh = x[:, h * LANES:(h + 1) * LANES]
        outs.append(xh * c + pltpu.roll(xh, ROT_DIM // 2, 1) * s1 + pltpu.roll(xh, LANES - ROT_DIM // 2, 1) * s2)
    return outs[0] if len(outs) == 1 else jnp.concatenate(outs, axis=1)


def _rope_tables(pos):
    half = ROT_DIM // 2
    inv = ROPE_THETA ** (-2.0 * jnp.arange(half, dtype=_F32) / ROT_DIM)
    ang = pos.astype(_F32)[:, None] * inv[None, :]
    cos, sin = jnp.cos(ang), jnp.sin(ang)
    n = pos.shape[0]
    c = jnp.concatenate([cos, cos, jnp.ones((n, LANES - ROT_DIM), _F32)], axis=1)
    s1 = jnp.concatenate([jnp.zeros((n, half), _F32), sin, jnp.zeros((n, LANES - ROT_DIM), _F32)], axis=1)
    s2 = jnp.concatenate([-sin, jnp.zeros((n, LANES - half), _F32)], axis=1)
    return c, s1, s2


def _ln_kernel(x_ref, g_ref, b_ref, o_ref, ob_ref):
    x = x_ref[...]
    mu = jnp.mean(x, axis=1, keepdims=True)
    xc = x - mu
    var = jnp.mean(xc * xc, axis=1, keepdims=True)
    y = xc * lax.rsqrt(var + LN_EPS) * g_ref[...] + b_ref[...]
    o_ref[...] = y
    ob_ref[...] = y.astype(_BF16)


def _layernorm(x, g, b, name):
    m, d = x.shape
    tr = 256 if m % 256 == 0 else m
    return pl.pallas_call(
        _ln_kernel, grid=(m // tr,),
        in_specs=[pl.BlockSpec((tr, d), lambda i: (i, 0)), pl.BlockSpec((1, d), lambda i: (0, 0)),
                  pl.BlockSpec((1, d), lambda i: (0, 0))],
        out_specs=[pl.BlockSpec((tr, d), lambda i: (i, 0)), pl.BlockSpec((tr, d), lambda i: (i, 0))],
        out_shape=[jax.ShapeDtypeStruct((m, d), _F32), jax.ShapeDtypeStruct((m, d), _BF16)],
        compiler_params=_params(1), name=name)(x, g.reshape(1, d), b.reshape(1, d))


def _sortable(score):
    bits = pltpu.bitcast(jnp.where(score == 0.0, 0.0, score), jnp.int32)
    return bits ^ ((bits >> 31) & jnp.int32(0x7FFFFFFF))


def _topk_bias(n_ch, load_key, meta, store_bias, k_sel, shape, axes, index_bits, neg):
    kf = float(k_sel)

    def count(pred):
        if axes == (0,):
            def cb(c, part):
                x = jnp.where(pred(c), 1.0, 0.0)
                slabs = [x[i * 8:(i + 1) * 8] for i in range(x.shape[0] // 8)]
                while len(slabs) > 1:
                    slabs = [slabs[i] + slabs[i + 1] for i in range(0, len(slabs), 2)]
                return part + slabs[0]
            part = lax.fori_loop(0, n_ch, cb, jnp.zeros((8, shape[1]), _F32))
            return jnp.sum(part, axis=0, keepdims=True)

        def cb(c, cnt):
            return cnt + jnp.sum(jnp.where(pred(c), 1.0, 0.0), axis=axes, keepdims=True)
        return lax.fori_loop(0, n_ch, cb, jnp.zeros(shape, _F32))

    def bisect(it, t):
        cand = t + lax.shift_left(jnp.int32(1), jnp.int32(31) - it)
        return jnp.where(count(lambda c: load_key(c) >= cand) >= kf, cand, t)

    t = lax.fori_loop(0, 32, bisect, jnp.full(shape, INT_MIN, jnp.int32))
    n_ge = count(lambda c: (load_key(c) >= t) & meta(c)[0])

    def tie_limit():
        need = kf - count(lambda c: load_key(c) > t)

        def body(it, m):
            cand = m + lax.shift_left(jnp.int32(1), jnp.int32(index_bits - 1) - it)
            cnt = count(lambda c: (load_key(c) == t) & meta(c)[0] & (meta(c)[1] < cand))
            return jnp.where(cnt < need, cand, m)

        return lax.fori_loop(0, index_bits, body, jnp.zeros(shape, jnp.int32))

    no_limit = jnp.full(shape, 2 ** 30, jnp.int32)
    m = lax.cond(jnp.max(n_ge) > kf, tie_limit, lambda: no_limit)

    def emit(c, carry):
        keys = load_key(c)
        valid, idx = meta(c)
        store_bias(c, jnp.where((keys > t) | ((keys == t) & valid & (idx <= m)), 0.0, neg))
        return carry

    lax.fori_loop(0, n_ch, emit, 0)


def _attn_prompt_kernel(q_ref, qi_ref, ki_ref, wi_ref, k_ref, v_ref, o_ref,
                        acc_ref, key_ref, bias_ref, wb_ref, l_ref, oacc_ref, sacc_ref, m_ref,
                        *, tq, kc, k_sel, index_bits):
    q0 = pl.program_id(1) * tq
    n_ch = (q0 + tq + kc - 1) // kc
    grp = N_HEADS // KV_HEADS
    wi = wi_ref[...] * ((IDX_HEADS * IDX_DIM) ** -0.5)
    for h in range(IDX_HEADS):
        wb_ref[h] = jnp.broadcast_to(wi[:, h:h + 1], (tq, LANES))

    def meta(c):
        col = c * kc + lax.broadcasted_iota(jnp.int32, (tq, kc), 1)
        row = q0 + lax.broadcasted_iota(jnp.int32, (tq, kc), 0)
        return col <= row, col

    def meta_t(c):
        col = c * kc + lax.broadcasted_iota(jnp.int32, (kc, tq), 0)
        row = q0 + lax.broadcasted_iota(jnp.int32, (kc, tq), 1)
        return col <= row, col

    def score_chunk(c, carry):
        ki = ki_ref[pl.ds(pl.multiple_of(c * kc, kc), kc), :]
        acc_ref[...] = jnp.zeros((tq, kc), _F32)

        def idx_body(hc, carry2):
            qc = qi_ref[0, pl.ds(hc * grp, grp)].reshape(grp * tq, IDX_DIM)
            d = lax.dot_general(qc, ki, _NT, preferred_element_type=_F32)
            part = None
            for hh in range(grp):
                w = jnp.tile(wb_ref[hc * grp + hh], (1, kc // LANES))
                term = jnp.maximum(d[hh * tq:(hh + 1) * tq], 0.0) * w
                part = term if part is None else part + term
            acc_ref[...] += part
            return carry2

        lax.fori_loop(0, IDX_HEADS // grp, idx_body, 0, unroll=True)
        keys = jnp.where(meta(c)[0], _sortable(acc_ref[...]), INT_MIN)
        key_ref[c] = pltpu.bitcast(pltpu.bitcast(keys, _F32).T, jnp.int32)
        return carry

    lax.fori_loop(0, n_ch, score_chunk, 0)

    def store_bias(c, val):
        bias_ref[c] = val.T

    _topk_bias(n_ch, lambda c: key_ref[c], meta_t, store_bias, k_sel, (1, tq), (0,), index_bits, -jnp.inf)

    m_ref[...] = jnp.full(m_ref.shape, -jnp.inf, _F32)
    oacc_ref[...] = jnp.zeros(oacc_ref.shape, _F32)
    sacc_ref[...] = jnp.zeros(sacc_ref.shape, _F32)

    def logits_chunk(c, carry):
        k0 = pl.multiple_of(c * kc, kc)
        bias = jnp.tile(bias_ref[c], (grp, 1))
        for g in range(KV_HEADS):
            qg = q_ref[0, g * grp:(g + 1) * grp].reshape(grp * tq, HEAD_DIM)
            lg = lax.dot_general(qg, k_ref[0, g, pl.ds(k0, kc), :], _NT, preferred_element_type=_F32)
            lg = lg * (HEAD_DIM ** -0.5 * LOG2E) + bias
            l_ref[g, c] = lg
            m_ref[g] = jnp.maximum(m_ref[g], jnp.max(lg, axis=1, keepdims=True))
        return carry

    lax.fori_loop(0, n_ch, logits_chunk, 0)

    ones_col = jnp.where(lax.broadcasted_iota(jnp.int32, (kc, LANES), 1) == 0, 1.0, 0.0).astype(_BF16)

    def pv_chunk(c, carry):
        k0 = pl.multiple_of(c * kc, kc)
        for g in range(KV_HEADS):
            p = jnp.exp2(l_ref[g, c] - m_ref[g]).astype(_BF16)
            v_aug = jnp.concatenate([v_ref[0, g, pl.ds(k0, kc), :], ones_col], axis=1)
            pv = jnp.dot(p, v_aug, preferred_element_type=_F32)
            oacc_ref[g] += pv[:, :HEAD_DIM]
            sacc_ref[g] += pv[:, HEAD_DIM:HEAD_DIM + 1]
        return carry

    lax.fori_loop(0, n_ch, pv_chunk, 0)
    for g in range(KV_HEADS):
        out = oacc_ref[g] / sacc_ref[g]
        for hh in range(grp):
            h = g * grp + hh
            o_ref[:, h * HEAD_DIM:(h + 1) * HEAD_DIM] = out[hh * tq:(hh + 1) * tq].astype(_BF16)


def _attn_prompt(q_hm, qi_hm, ki_b, wi, k_hm, v_hm, n_batch, seq):
    tq = 128
    kc = KEY_CHUNK if seq % KEY_CHUNK == 0 else seq
    nqb = seq // tq
    grp = N_HEADS // KV_HEADS
    k_sel = min(TOPK_MAX, seq // 4)
    kern = functools.partial(_attn_prompt_kernel, tq=tq, kc=kc, k_sel=k_sel,
                             index_bits=(seq - 1).bit_length() + 1)
    return pl.pallas_call(
        kern, grid=(n_batch, nqb),
        in_specs=[
            pl.BlockSpec((1, N_HEADS, tq, HEAD_DIM), lambda b, q: (b, 0, q, 0)),
            pl.BlockSpec((1, IDX_HEADS, tq, IDX_DIM), lambda b, q: (b, 0, q, 0)),
            pl.BlockSpec((seq, IDX_DIM), lambda b, q: (b, 0)),
            pl.BlockSpec((tq, LANES), lambda b, q: (b * nqb + q, 0)),
            pl.BlockSpec((1, KV_HEADS, seq, HEAD_DIM), lambda b, q: (b, 0, 0, 0)),
            pl.BlockSpec((1, KV_HEADS, seq, HEAD_DIM), lambda b, q: (b, 0, 0, 0)),
        ],
        out_specs=pl.BlockSpec((tq, A_Q), lambda b, q: (b * nqb + q, 0)),
        out_shape=jax.ShapeDtypeStruct((n_batch * seq, A_Q), _BF16),
        scratch_shapes=[pltpu.VMEM((tq, kc), _F32), pltpu.VMEM((seq // kc, kc, tq), jnp.int32),
                        pltpu.VMEM((seq // kc, tq, kc), _F32), pltpu.VMEM((IDX_HEADS, tq, LANES), _F32),
                        pltpu.VMEM((KV_HEADS, seq // kc, grp * tq, kc), _F32),
                        pltpu.VMEM((KV_HEADS, grp * tq, HEAD_DIM), _F32),
                        pltpu.VMEM((KV_HEADS, grp * tq, 1), _F32), pltpu.VMEM((KV_HEADS, grp * tq, 1), _F32)],
        compiler_params=_params(2), name="attn_prompt")(q_hm, qi_hm, ki_b, wi, k_hm, v_hm)


SCORE_PAGES_PER_STEP = 32
ATTN_PAGES_PER_STEP = 16


def _attn_sample_score_kernel(pt_ref, qi_ref, wi_ref, kin_ref, exp_ref, *rest, n_pages, k_sel, pps):
    pages = rest[:pps]
    bias_ref, score_ref, key_ref, sel_ref = rest[pps:]
    del pt_ref
    s = pl.program_id(1)
    n_rows = n_pages + 8
    qi = qi_ref[0]
    wcol = wi_ref[0] * ((IDX_HEADS * IDX_DIM) ** -0.5)
    for r in range(pps):
        page = pages[r][0].astype(_BF16)
        d = lax.dot_general(qi, page, _NT, preferred_element_type=_F32)
        sc = jnp.sum(jnp.maximum(d, 0.0) * wcol, axis=0, keepdims=True)
        score_ref[pl.ds(s * pps + r, 1), :] = sc

    @pl.when(s == pl.num_programs(1) - 1)
    def _():
        kin = kin_ref[0].astype(_BF16).astype(_F32)
        dn = jnp.sum(qi.astype(_F32) * kin, axis=1, keepdims=True)
        sn = jnp.sum(jnp.maximum(dn, 0.0) * wcol, axis=0, keepdims=True)
        score_ref[pl.ds(n_pages, 8), :] = jnp.broadcast_to(sn, (8, LANES))
        rowi = lax.broadcasted_iota(jnp.int32, (n_rows, LANES), 0)
        lane = lax.broadcasted_iota(jnp.int32, (n_rows, LANES), 1)
        valid = (rowi < n_pages) | ((rowi == n_pages) & (lane == 0))
        pos = rowi * LANES + lane
        key_ref[...] = jnp.where(valid, _sortable(score_ref[...]), INT_MIN)

        def store_sel(c, val):
            sel_ref[...] = val

        _topk_bias(1, lambda c: key_ref[...], lambda c: (valid, pos), store_sel, k_sel, (1, 1), (0, 1),
                   (n_rows * LANES - 1).bit_length() + 1, 1.0)
        rep = jnp.dot(sel_ref[...].astype(_BF16), exp_ref[...], preferred_element_type=_F32)
        bias_ref[0] = jnp.where(rep < 0.5, 0.0, NEG_BIG)


def _attn_sample_kernel(pt_ref, q_ref, kn_ref, vn_ref, bias_ref, *rest, n_pages, pps):
    kpages = rest[:pps]
    vpages = rest[pps:2 * pps]
    o_ref, m_ref, s_ref, acc_ref = rest[2 * pps:]
    del pt_ref
    st = pl.program_id(1)
    scale = HEAD_DIM ** -0.5
    grp = N_HEADS // KV_HEADS
    n_cols = PAGE_SIZE * KV_HEADS
    q = q_ref[0]
    own = (lax.broadcasted_iota(jnp.int32, (N_HEADS, n_cols), 1) % KV_HEADS
           == lax.broadcasted_iota(jnp.int32, (N_HEADS, n_cols), 0) // grp)
    head_bias = jnp.where(own, 0.0, NEG_BIG)

    @pl.when(st == 0)
    def _():
        m_ref[...] = jnp.full(m_ref.shape, NEG_BIG, _F32)
        s_ref[...] = jnp.zeros(s_ref.shape, _F32)
        acc_ref[...] = jnp.zeros(acc_ref.shape, _F32)

    def update(logit_list, pv_fns):
        m_old = m_ref[...]
        m_new = m_old
        for lg in logit_list:
            m_new = jnp.maximum(m_new, jnp.max(lg, axis=1, keepdims=True))
        alpha = jnp.exp(m_old - m_new)
        s_new = s_ref[...] * alpha
        acc = acc_ref[...] * alpha
        for lg, pv_fn in zip(logit_list, pv_fns):
            p = jnp.exp(lg - m_new)
            s_new = s_new + jnp.sum(p, axis=1, keepdims=True)
            acc = acc + pv_fn(p)
        s_ref[...] = s_new
        acc_ref[...] = acc
        m_ref[...] = m_new

    logit_list, pv_fns = [], []
    for r in range(pps):
        kp = kpages[r][0].astype(_BF16)
        logits = lax.dot_general(q, kp, _NT, preferred_element_type=_F32) * scale
        logit_list.append(logits + head_bias + bias_ref[0, pl.ds(st * pps + r, 1), :])
        pv_fns.append(lambda p, r=r: jnp.dot(p.astype(_BF16), vpages[r][0].astype(_BF16),
                                             preferred_element_type=_F32))
    update(logit_list, pv_fns)

    @pl.when(st == pl.num_programs(1) - 1)
    def _():
        kn = kn_ref[0].astype(_BF16).astype(_F32)
        vn = vn_ref[0].astype(_BF16).astype(_F32)
        ln = jnp.sum(q.astype(_F32) * kn, axis=1, keepdims=True) * scale
        ln = ln + bias_ref[0, pl.ds(n_pages, 1), :][:, 0:1]
        update([ln], [lambda p: p * vn])
        o_ref[0] = acc_ref[...] / s_ref[...]


def _attn_sample(q_s, qi_s, wi_s, ki_s, k_s, v_s, cache_k, cache_v, cache_ik, page_table, layer):
    n_b, n_pages = page_table.shape
    depth, n_phys = cache_ik.shape[:2]
    k_sel = min(TOPK_MAX, (n_pages * PAGE_SIZE + 1) // 4)
    pps_s = min(SCORE_PAGES_PER_STEP, n_pages)
    pps_a = min(ATTN_PAGES_PER_STEP, n_pages)
    assert n_pages % pps_s == 0 and n_pages % pps_a == 0
    n_rows = n_pages + 8
    n_cols = PAGE_SIZE * KV_HEADS
    pt = page_table.reshape(-1)
    grp = N_HEADS // KV_HEADS

    def page_map(r, pps):
        return lambda b, s, pt_ref: (layer, pt_ref[b * n_pages + s * pps + r], 0, 0)

    qi3 = qi_s[:n_b].reshape(n_b, IDX_HEADS, IDX_DIM)
    wi3 = wi_s[:n_b, :IDX_HEADS].reshape(n_b, IDX_HEADS, 1)
    kin3 = ki_s[:n_b].reshape(n_b, 1, IDX_DIM)
    expand = (jnp.arange(LANES)[:, None] == jnp.arange(n_cols)[None, :] // KV_HEADS).astype(_BF16)
    bias = pl.pallas_call(
        functools.partial(_attn_sample_score_kernel, n_pages=n_pages, k_sel=k_sel, pps=pps_s),
        grid_spec=pltpu.PrefetchScalarGridSpec(
            num_scalar_prefetch=1, grid=(n_b, n_pages // pps_s),
            in_specs=[pl.BlockSpec((1, IDX_HEADS, IDX_DIM), lambda b, s, pt_ref: (b, 0, 0)),
                      pl.BlockSpec((1, IDX_HEADS, 1), lambda b, s, pt_ref: (b, 0, 0)),
                      pl.BlockSpec((1, 1, IDX_DIM), lambda b, s, pt_ref: (b, 0, 0)),
                      pl.BlockSpec((LANES, n_cols), lambda b, s, pt_ref: (0, 0))]
            + [pl.BlockSpec((None, 1, PAGE_SIZE, IDX_DIM), page_map(r, pps_s)) for r in range(pps_s)],
            out_specs=pl.BlockSpec((1, n_rows, n_cols), lambda b, s, pt_ref: (b, 0, 0)),
            scratch_shapes=[pltpu.VMEM((n_rows, LANES), _F32), pltpu.VMEM((n_rows, LANES), jnp.int32),
                            pltpu.VMEM((n_rows, LANES), _F32)]),
        out_shape=jax.ShapeDtypeStruct((n_b, n_rows, n_cols), _F32),
        compiler_params=_params(2), name="attn_sample_score")(
            pt, qi3, wi3, kin3, expand, *([cache_ik] * pps_s))

    pk = cache_k.reshape(depth, n_phys, n_cols, HEAD_DIM)
    pv = cache_v.reshape(depth, n_phys, n_cols, HEAD_DIM)
    q3 = q_s[:n_b].reshape(n_b, N_HEADS, HEAD_DIM)
    kn = jnp.repeat(k_s[:n_b].reshape(n_b, KV_HEADS, HEAD_DIM), grp, axis=1)
    vn = jnp.repeat(v_s[:n_b].reshape(n_b, KV_HEADS, HEAD_DIM), grp, axis=1)
    head_spec = pl.BlockSpec((1, N_HEADS, HEAD_DIM), lambda b, s, pt_ref: (b, 0, 0))
    out = pl.pallas_call(
        functools.partial(_attn_sample_kernel, n_pages=n_pages, pps=pps_a),
        grid_spec=pltpu.PrefetchScalarGridSpec(
            num_scalar_prefetch=1, grid=(n_b, n_pages // pps_a),
            in_specs=[head_spec, head_spec, head_spec,
                      pl.BlockSpec((1, n_rows, n_cols), lambda b, s, pt_ref: (b, 0, 0))]
            + [pl.BlockSpec((None, 1, n_cols, HEAD_DIM), page_map(r, pps_a)) for r in range(pps_a)]
            + [pl.BlockSpec((None, 1, n_cols, HEAD_DIM), page_map(r, pps_a)) for r in range(pps_a)],
            out_specs=head_spec,
            scratch_shapes=[pltpu.VMEM((N_HEADS, 1), _F32), pltpu.VMEM((N_HEADS, 1), _F32),
                            pltpu.VMEM((N_HEADS, HEAD_DIM), _F32)]),
        out_shape=jax.ShapeDtypeStruct((n_b, N_HEADS, HEAD_DIM), _F32),
        compiler_params=_params(2), name="attn_sample")(
            pt, q3, kn, vn, bias, *([pk] * pps_a), *([pv] * pps_a))
    return out.reshape(n_b, A_Q)


def _gmlp_kernel(u_ref, v_ref, g_ref, b_ref, ws_ref, bs_ref, o_ref, vn_ref=None):
    v = v_ref[...]
    mu = jnp.mean(v, axis=1, keepdims=True)
    vc = v - mu
    var = jnp.mean(vc * vc, axis=1, keepdims=True)
    vn = vc * lax.rsqrt(var + LN_EPS) * g_ref[...] + b_ref[...]
    if vn_ref is not None:
        vn_ref[...] = vn
    gw = GMLP_WIDTH // GMLP_GROUPS
    tril = (lax.broadcasted_iota(jnp.int32, (CHUNK, CHUNK), 0)
            >= lax.broadcasted_iota(jnp.int32, (CHUNK, CHUNK), 1))
    bs = bs_ref[...]
    for g in range(GMLP_GROUPS):
        wm = jnp.where(tril, ws_ref[g], 0.0).astype(_BF16)
        mixed = jnp.dot(wm, vn[:, g * gw:(g + 1) * gw].astype(_BF16), preferred_element_type=_F32)
        mixed = mixed + bs[:, g:g + 1]
        o_ref[:, g * gw:(g + 1) * gw] = (u_ref[:, g * gw:(g + 1) * gw] * mixed).astype(_BF16)


def _gmlp(hb, ln_g, ln_b, ws, bs, want_vn, name):
    m = hb.shape[0]
    w = GMLP_WIDTH
    out_shape = [jax.ShapeDtypeStruct((m, w), _BF16)]
    out_specs = [pl.BlockSpec((CHUNK, w), lambda c: (c, 0))]
    if want_vn:
        out_shape.append(jax.ShapeDtypeStruct((m, w), _F32))
        out_specs.append(pl.BlockSpec((CHUNK, w), lambda c: (c, 0)))
    return pl.pallas_call(
        _gmlp_kernel, grid=(m // CHUNK,),
        in_specs=[pl.BlockSpec((CHUNK, w), lambda c: (c, 0)), pl.BlockSpec((CHUNK, w), lambda c: (c, 1)),
                  pl.BlockSpec((1, w), lambda c: (0, 0)), pl.BlockSpec((1, w), lambda c: (0, 0)),
                  pl.BlockSpec((GMLP_GROUPS, CHUNK, CHUNK), lambda c: (0, 0, 0)),
                  pl.BlockSpec((CHUNK, GMLP_GROUPS), lambda c: (0, 0))],
        out_specs=out_specs, out_shape=out_shape, compiler_params=_params(1), name=name)(
            hb, hb, ln_g.reshape(1, w), ln_b.reshape(1, w), ws, jnp.transpose(bs))


def _head_sums(x, blk):
    outs = []
    for c in range(x.shape[1] // LANES):
        xc = x[:, c * LANES:(c + 1) * LANES]
        hi = xc.astype(_BF16)
        lo = (xc - hi.astype(_F32)).astype(_BF16)
        outs.append(jnp.dot(hi, blk, preferred_element_type=_F32) + jnp.dot(lo, blk, preferred_element_type=_F32))
    return jnp.concatenate(outs, axis=1)


def _rwkv_pre_kernel(pc_ref, sh_ref, prev_ref, mu_ref, w0_ref, w2_ref, a0_ref, a2_ref,
                     r_ref, w_ref, k_ref, v_ref, a_ref, *, tr, tiles_per_batch, roll_shift):
    pc = pc_ref[...]
    if roll_shift:
        first = pl.program_id(0) % tiles_per_batch == 0
        before = jnp.where(first, prev_ref[0], sh_ref[7:8, :])
        rolled = pltpu.roll(pc, 1, 0)
        rowi = lax.broadcasted_iota(jnp.int32, pc.shape, 0)
        shifted = jnp.where(rowi == 0, before, rolled)
    else:
        shifted = sh_ref[...]
    y = pc + mu_ref[...] * (shifted - pc)
    wdt = RWKV_WIDTH
    r = y[:, 0:wdt]
    k = y[:, wdt:2 * wdt]
    v = y[:, 2 * wdt:3 * wdt]
    wd = y[:, 3 * wdt:3 * wdt + D_DECAY]
    ad = y[:, 3 * wdt + D_DECAY:3 * wdt + D_DECAY + D_AAA]
    z = -(w0_ref[...] + jnp.dot(jnp.tanh(wd).astype(_BF16), w2_ref[...].astype(_BF16),
                                preferred_element_type=_F32))
    softplus = jnp.maximum(z, 0.0) + jnp.log(1.0 + jnp.exp(-jnp.abs(z)))
    decay = jnp.exp(-jnp.exp(-softplus - 0.5))
    a = jax.nn.sigmoid(a0_ref[...] + jnp.dot(ad.astype(_BF16), a2_ref[...].astype(_BF16),
                                             preferred_element_type=_F32))
    r_ref[...] = r
    w_ref[...] = decay
    k_ref[...] = k
    v_ref[...] = v
    a_ref[...] = a


def _seg_blk():
    seg = jnp.arange(LANES) // RWKV_HEAD
    return (seg[:, None] == seg[None, :]).astype(_BF16)


def _rwkv_pre(hc, shift_src, lp, seq, roll_shift, name):
    m = hc.shape[0]
    tr = 256 if (roll_shift and seq % 256 == 0) else (seq if roll_shift else m)
    tpb = seq // tr if roll_shift else 1
    wdt = RWKV_WIDTH
    if roll_shift:
        n_b = m // seq
        sh_spec = pl.BlockSpec((8, C_COLS), lambda i: (jnp.maximum(i * (tr // 8) - 1, 0), 0))
        prev = shift_src.reshape(n_b, 1, C_COLS)
        prev_spec = pl.BlockSpec((1, 1, C_COLS), lambda i: (i // tpb, 0, 0))
        sh_arg = hc
    else:
        sh_spec = pl.BlockSpec((tr, C_COLS), lambda i: (i, 0))
        prev = jnp.zeros((1, 1, C_COLS), _F32)
        prev_spec = pl.BlockSpec((1, 1, C_COLS), lambda i: (0, 0, 0))
        sh_arg = shift_src

    def vec(n):
        return pl.BlockSpec((1, n), lambda i: (0, 0))

    kern = functools.partial(_rwkv_pre_kernel, tr=tr, tiles_per_batch=tpb, roll_shift=roll_shift)
    return pl.pallas_call(
        kern, grid=(m // tr,),
        in_specs=[pl.BlockSpec((tr, C_COLS), lambda i: (i, 0)), sh_spec, prev_spec, vec(C_COLS), vec(wdt),
                  pl.BlockSpec((D_DECAY, wdt), lambda i: (0, 0)), vec(wdt),
                  pl.BlockSpec((D_AAA, wdt), lambda i: (0, 0))],
        out_specs=[pl.BlockSpec((tr, wdt), lambda i: (i, 0))] * 5,
        out_shape=[jax.ShapeDtypeStruct((m, wdt), _F32)] * 5,
        compiler_params=_params(1), name=name)(
            hc, sh_arg, prev, lp["rwkv_mu"].reshape(1, C_COLS), lp["rwkv_w0"].reshape(1, wdt), lp["rwkv_w2"],
            lp["rwkv_a0"].reshape(1, wdt), lp["rwkv_a2"])


SCAN_BATCH = LANES // RWKV_HEADS
HEAD_PAIRS = RWKV_HEADS // 2


def _rwkv_scan_kernel(*refs, tt, halves):
    cur = refs[0:5]
    nxt = refs[5:10]
    kkp_ref, ka_ref, s0_ref, o_ref, st_ref = refs[10:15]
    z = (refs[15:21], refs[21:27])
    zo = refs[27]
    _scan_block(cur, nxt, kkp_ref, ka_ref, s0_ref, o_ref, st_ref, z, zo, pl.program_id(0), tt, halves)


def _scan_block(cur, nxt, kkp_ref, ka_ref, s0_ref, o_ref, st_ref, z, zo, step_idx, tt, halves):
    n = RWKV_HEAD

    def to_lanes(srcs, off, dsts, t):
        def lanes(src):
            x = jnp.concatenate([src[bb, off + t] for bb in range(SCAN_BATCH)], axis=0)
            x2 = jnp.concatenate([x, pltpu.roll(x, LANES // 2, 1)], axis=0)
            return x2.T[:n]
        r, w, k, v, a = [lanes(src) for src in srcs]
        kk = k * kkp_ref[...]
        kk = kk / jnp.maximum(jnp.sqrt(jnp.sum(kk * kk, axis=0, keepdims=True)), 1e-12)
        zr, zw, zk, zv, zkk, zb = dsts
        zr[t] = r
        zw[t] = w
        zk[t] = k * (1.0 + (a - 1.0) * ka_ref[...])
        zv[t] = v
        zkk[t] = kk
        zb[t] = kk * a

    def from_lanes(t):
        y = zo[t]
        yt = jnp.concatenate([y, pltpu.roll(y, LANES // 2, 1)], axis=0).T[:LANES // 2]
        for bb in range(SCAN_BATCH):
            o_ref[bb, t] = yt[bb * HEAD_PAIRS:(bb + 1) * HEAD_PAIRS]

    def update(zs, t, t_out):
        zr, zw, zk, zv, zkk, zb = zs
        acc = st_ref[0] * zkk[t, pl.ds(0, 1), :]
        for j in range(1, n):
            acc = acc + st_ref[j] * zkk[t, pl.ds(j, 1), :]
        sa = -acc
        vt = zv[t]
        out = None
        for j in range(n):
            sj = st_ref[j] * zw[t, pl.ds(j, 1), :] + sa * zb[t, pl.ds(j, 1), :] + vt * zk[t, pl.ds(j, 1), :]
            st_ref[j] = sj
            term = sj * zr[t, pl.ds(j, 1), :]
            out = term if out is None else out + term
        zo[t_out] = out

    def fill_first():
        def body(t, carry):
            to_lanes(cur, 0, z[0], t)
            return carry
        lax.fori_loop(0, tt, body, 0)

    if halves == 1:
        @pl.when(step_idx == 0)
        def _():
            st_ref[...] = s0_ref[...]

        fill_first()

        def body(t, carry):
            update(z[0], t, t)
            return carry

        lax.fori_loop(0, tt, body, 0)

        def out_body(t, carry):
            from_lanes(t)
            return carry

        lax.fori_loop(0, tt, out_body, 0)
        return

    @pl.when(step_idx == 0)
    def _():
        st_ref[...] = s0_ref[...]
        zo[...] = jnp.zeros(zo.shape, _F32)
        fill_first()

    def first_half(t, carry):
        to_lanes(cur, tt, z[1], t)
        from_lanes(jnp.maximum(t - 1, 0))
        update(z[0], t, t)
        return carry

    def second_half(t, carry):
        to_lanes(nxt, 0, z[0], t)
        from_lanes(tt + t - 1)
        update(z[1], t, tt + t)
        return carry

    lax.fori_loop(0, tt, first_half, 0)
    lax.fori_loop(0, tt, second_half, 0)
    from_lanes(2 * tt - 1)


def _head_param_to_lanes(p):
    q = jnp.transpose(p.reshape(HEAD_PAIRS, 2, RWKV_HEAD), (2, 1, 0))
    return jnp.broadcast_to(q[:, :, None, :], (RWKV_HEAD, 2, SCAN_BATCH, HEAD_PAIRS)).reshape(RWKV_HEAD, LANES)


def _rwkv_scan(vecs, s0, lp, seq, name):
    n = RWKV_HEAD
    halves, tt = (2, 8) if seq % 16 == 0 else (1, seq)
    blk = halves * tt
    n_half = seq // tt
    shape4 = (SCAN_BATCH, seq, HEAD_PAIRS, LANES)
    vspec = pl.BlockSpec((SCAN_BATCH, blk, HEAD_PAIRS, LANES), lambda s: (0, s, 0, 0))
    nspec = pl.BlockSpec((SCAN_BATCH, tt, HEAD_PAIRS, LANES),
                         lambda s: (0, jnp.minimum(halves * (s + 1), n_half - 1), 0, 0))
    sspec = pl.BlockSpec((n, n, LANES), lambda s: (0, 0, 0))
    pspec = pl.BlockSpec((n, LANES), lambda s: (0, 0))
    args = [x.reshape(shape4) for x in vecs]
    o, st = pl.pallas_call(
        functools.partial(_rwkv_scan_kernel, tt=tt, halves=halves), grid=(seq // blk,),
        in_specs=[vspec] * 5 + [nspec] * 5 + [pspec, pspec, sspec], out_specs=[vspec, sspec],
        out_shape=[jax.ShapeDtypeStruct(shape4, _F32), jax.ShapeDtypeStruct((n, n, LANES), _F32)],
        scratch_shapes=[pltpu.VMEM((tt, n, LANES), _F32)] * 12 + [pltpu.VMEM((blk, n, LANES), _F32)],
        compiler_params=_params(1), name=name)(
            *args, *args, _head_param_to_lanes(lp["rwkv_kk"]), _head_param_to_lanes(lp["rwkv_ka"]), s0)
    return o.reshape(SCAN_BATCH * seq, RWKV_WIDTH), st


def _state_to_lanes(s):
    s5 = s.reshape(SCAN_BATCH, HEAD_PAIRS, 2, RWKV_HEAD, RWKV_HEAD)
    return jnp.transpose(s5, (4, 3, 2, 0, 1)).reshape(RWKV_HEAD, RWKV_HEAD, LANES)


def _state_from_lanes(s):
    s5 = s.reshape(RWKV_HEAD, RWKV_HEAD, 2, SCAN_BATCH, HEAD_PAIRS)
    return jnp.transpose(s5, (3, 4, 2, 1, 0)).reshape(SCAN_BATCH, RWKV_HEADS, RWKV_HEAD, RWKV_HEAD)


def _rwkv_post_kernel(o_ref, r_ref, k_ref, v_ref, a_ref, g_ref, b_ref, rk_ref, ka_ref, blk_ref, out_ref):
    blk = blk_ref[...]
    o = o_ref[...]
    inv = 1.0 / RWKV_HEAD
    m = _head_sums(o, blk) * inv
    oc = o - m
    var = _head_sums(oc * oc, blk) * inv
    on = oc * lax.rsqrt(var + GN_EPS) * g_ref[...] + b_ref[...]
    k = k_ref[...] * (1.0 + (a_ref[...] - 1.0) * ka_ref[...])
    bonus = _head_sums(r_ref[...] * k * rk_ref[...], blk) * v_ref[...]
    out_ref[...] = (on + bonus).astype(_BF16)


def _rwkv_post(o, r, k, v, a, lp, name):
    m = o.shape[0]
    tr = 256 if m % 256 == 0 else m
    wdt = RWKV_WIDTH
    row = pl.BlockSpec((tr, wdt), lambda i: (i, 0))
    vec = pl.BlockSpec((1, wdt), lambda i: (0, 0))
    return pl.pallas_call(
        _rwkv_post_kernel, grid=(m // tr,),
        in_specs=[row] * 5 + [vec] * 4 + [pl.BlockSpec((LANES, LANES), lambda i: (0, 0))],
        out_specs=row, out_shape=jax.ShapeDtypeStruct((m, wdt), _BF16),
        compiler_params=_params(1), name=name)(
            o, r, k, v, a, lp["rwkv_gn_g"].reshape(1, wdt), lp["rwkv_gn_b"].reshape(1, wdt),
            lp["rwkv_rk"].reshape(1, wdt), lp["rwkv_ka"].reshape(1, wdt), _seg_blk())


def _pad_rows(x):
    return jnp.zeros((SAMPLE_ROWS,) + x.shape[1:], x.dtype).at[:x.shape[0]].set(x)


def _layer(xp, xpb, xs, xsb, seq, n_dec, layer, lp, big, cache_k, cache_v, cache_ik, page_table,
           st_shift, st_wkv):
    mp = xp.shape[0]
    n_batch = mp // seq
    tm = 1024 if seq % 1024 == 0 else seq
    d = D_MODEL
    rope_p = _rope_tables(jnp.arange(seq, dtype=jnp.int32))
    rope_s = _rope_tables(jnp.full((SAMPLE_ROWS,), PAST_LEN, jnp.int32))
    rope_extras = [dict(p=a, s=b) for a, b in zip(rope_p, rope_s)]
    lhs_x = [(xpb, xsb)]

    def rope_ep(accs, tex, rex):
        return [_rope_tiles(accs[0], *rex)]

    def pair(w, ncols, wcol, kb=None, kidx=0, lhs=0):
        return dict(lhs=lhs, w=w, layer=layer, kb=w.shape[1] if kb is None else kb, kidx=kidx, ncols=ncols,
                    wcol=wcol)

    def pair_in(ncols, row0):
        n_in = big["w_in_t"].shape[0] // big["depth"]
        return dict(lhs=0, w=big["w_in_t"], layer=layer, kb=d, kidx=0, ncols=ncols,
                    row0=lambda j: pl.multiple_of(layer * n_in + row0(j) + 0 * j, 8))

    tn = 512
    tn_w = 1024
    tm_w = 512 if seq % 512 == 0 else seq
    (q_hm, q_s), = _ws_matmul(
        "proj_q", lhs_x, [pair_in(tn_w, lambda j: j * tn_w)],
        [dict(ncols=tn_w, dtype=_BF16, col=lambda j: j, total=A_Q, hm=True)],
        rope_ep, tm=tm_w, n_steps=A_Q // tn_w, seq=seq, row_extras=rope_extras)
    qi_start = A_Q + 2 * A_KV
    (qi_hm, qi_s), = _ws_matmul(
        "proj_qi", lhs_x, [pair_in(tn_w, lambda j: qi_start + j * tn_w)],
        [dict(ncols=tn_w, dtype=_BF16, col=lambda j: j, total=IDX_HEADS * IDX_DIM, hm=True)],
        rope_ep, tm=tm_w, n_steps=IDX_HEADS * IDX_DIM // tn_w, seq=seq, row_extras=rope_extras)

    ki_start = qi_start + IDX_HEADS * IDX_DIM

    def k_ep(accs, tex, rex):
        kr = _rope_tiles(accs[0], *rex)
        kir = _rope_tiles(accs[1], *rex)
        return [kr, kr, kir, kir]

    zero = lambda j: 0
    (k_f, k_fs), (k_hm, _), (ki_f, ki_fs), (ki_b, _) = _ws_matmul(
        "proj_k", lhs_x,
        [pair_in(A_KV, lambda j: A_Q), pair_in(IDX_DIM, lambda j: ki_start)],
        [dict(ncols=A_KV, dtype=_F32, col=zero, total=A_KV),
         dict(ncols=A_KV, dtype=_BF16, col=zero, total=A_KV, hm=True),
         dict(ncols=IDX_DIM, dtype=_F32, col=zero, total=IDX_DIM),
         dict(ncols=IDX_DIM, dtype=_BF16, col=zero, total=IDX_DIM)],
        k_ep, tm=tm, n_steps=1, seq=seq, row_extras=rope_extras)

    def v_ep(accs, tex, rex):
        return [accs[0], accs[0], accs[1]]

    (v_f, v_fs), (v_hm, _), (wi_f, wi_fs) = _ws_matmul(
        "proj_v", lhs_x,
        [pair_in(A_KV, lambda j: A_Q + A_KV), pair_in(LANES, lambda j: ki_start + IDX_DIM)],
        [dict(ncols=A_KV, dtype=_F32, col=zero, total=A_KV),
         dict(ncols=A_KV, dtype=_BF16, col=zero, total=A_KV, hm=True),
         dict(ncols=LANES, dtype=_F32, col=zero, total=LANES)],
        v_ep, tm=tm, n_steps=1, seq=seq)

    ident = lambda accs, tex, rex: [accs[0]]
    (hb, hb_s), = _ws_matmul(
        "proj_b", lhs_x, [pair_in(tn_w, lambda j: A_COLS + j * tn_w)],
        [dict(ncols=tn_w, dtype=_F32, col=lambda j: j, total=B_COLS)],
        ident, tm=tm_w, n_steps=B_COLS // tn_w, seq=seq)
    tn_c = 640
    (hc, hc_s), = _ws_matmul(
        "proj_c", lhs_x, [pair_in(tn_c, lambda j: A_COLS + B_COLS + j * tn_c)],
        [dict(ncols=tn_c, dtype=_F32, col=lambda j: j, total=C_COLS)],
        ident, tm=tm, n_steps=C_COLS // tn_c, seq=seq)
    (gate, gate_s), = _ws_matmul(
        "proj_g", lhs_x, [pair_in(tn_w, lambda j: A_COLS + B_COLS + C_COLS + j * tn_w)],
        [dict(ncols=tn_w, dtype=_BF16, col=lambda j: j, total=G_COLS)],
        lambda accs, tex, rex: [jax.nn.sigmoid(accs[0])], tm=tm_w, n_steps=G_COLS // tn_w, seq=seq)
    out_a = _attn_prompt(q_hm, qi_hm, ki_b, wi_f, k_hm, v_hm, n_batch, seq)
    out_a_s = _attn_sample(q_s, qi_s, wi_fs, ki_fs, k_fs, v_fs, cache_k, cache_v, cache_ik, page_table, layer)
    out_a_s = _pad_rows(out_a_s.astype(_BF16))

    out_b, = _gmlp(hb, lp["gmlp_ln_g"], lp["gmlp_ln_b"], lp["gmlp_ws"], lp["gmlp_bs"], False, "gmlp_prompt")
    hb_chunks = jnp.zeros((n_dec, CHUNK, B_COLS), _F32).at[:, 0].set(hb_s[:n_dec]).reshape(n_dec * CHUNK, B_COLS)
    ob_s, vn_s = _gmlp(hb_chunks, lp["gmlp_ln_g"], lp["gmlp_ln_b"], lp["gmlp_ws"], lp["gmlp_bs"], True,
                       "gmlp_sample")
    out_b_s = _pad_rows(ob_s.reshape(n_dec, CHUNK, GMLP_WIDTH)[:, 0])
    vn_rows = vn_s.reshape(n_dec, CHUNK, GMLP_WIDTH)[:, 0]

    pre_p = _rwkv_pre(hc, jnp.zeros((n_batch, C_COLS), _F32), lp, seq, True, "rwkv_pre_prompt")
    zero_state = jnp.zeros((RWKV_HEAD, RWKV_HEAD, LANES), _F32)
    o_parts, st_parts = [], []
    for b0 in range(0, n_batch, SCAN_BATCH):
        rows = slice(b0 * seq, (b0 + SCAN_BATCH) * seq)
        o_c, st_c = _rwkv_scan([x[rows] for x in pre_p], zero_state, lp, seq, "rwkv_scan_prompt")
        o_parts.append(o_c)
        st_parts.append(_state_from_lanes(st_c))
    o_p = o_parts[0] if len(o_parts) == 1 else jnp.concatenate(o_parts, axis=0)
    wkv_p = st_parts[0] if len(st_parts) == 1 else jnp.concatenate(st_parts, axis=0)
    out_c = _rwkv_post(o_p, pre_p[0], pre_p[2], pre_p[3], pre_p[4], lp, "rwkv_post_prompt")

    pre_s = _rwkv_pre(hc_s, _pad_rows(st_shift), lp, 1, False, "rwkv_pre_sample")
    o_parts, st_parts = [], []
    for b0 in range(0, n_dec, SCAN_BATCH):
        o_c, st_c = _rwkv_scan([x[b0:b0 + SCAN_BATCH] for x in pre_s],
                               _state_to_lanes(st_wkv[b0:b0 + SCAN_BATCH]), lp, 1, "rwkv_scan_sample")
        o_parts.append(o_c)
        st_parts.append(_state_from_lanes(st_c))
    o_s = _pad_rows(jnp.concatenate(o_parts, axis=0))
    wkv_s = jnp.concatenate(st_parts, axis=0)
    out_c_s = _rwkv_post(o_s, pre_s[0], pre_s[2], pre_s[3], pre_s[4], lp, "rwkv_post_sample")

    tm_m = 512 if seq % 512 == 0 else seq
    n_g = d // tn

    def merge_ep(accs, tex, rex):
        return [tex[0].astype(_F32) * accs[0] + tex[1].astype(_F32) * accs[1] + tex[2].astype(_F32) * accs[2]]

    (merged, merged_s), = _ws_matmul(
        "merge", [(out_a, out_a_s), (out_b, out_b_s), (out_c, out_c_s)],
        [pair(big["attn_wo"], tn, lambda j: j, lhs=0), pair(big["gmlp_wo"], tn, lambda j: j, lhs=1),
         pair(big["rwkv_wo"], tn, lambda j: j, lhs=2)],
        [dict(ncols=tn, dtype=_BF16, col=lambda j: j, total=d)],
        merge_ep, tm=tm_m, n_steps=d // tn, seq=seq,
        tile_extras=[dict(p=gate, s=gate_s, ncols=tn, col=lambda j, a=a: j + a * n_g) for a in range(3)])

    def resid_ep(accs, tex, rex):
        return [ALPHA * tex[0] + accs[0]]

    (pre1, pre1_s), = _ws_matmul(
        "out_proj", [(merged, merged_s)], [pair(big["w_out"], tn, lambda j: j)],
        [dict(ncols=tn, dtype=_F32, col=lambda j: j, total=d)],
        resid_ep, tm=tm, n_steps=d // tn, seq=seq,
        tile_extras=[dict(p=xp, s=xs, ncols=tn, col=lambda j: j)])
    x1, x1b = _layernorm(pre1, lp["ln1_g"], lp["ln1_b"], "ln1_prompt")
    x1_s, x1b_s = _layernorm(pre1_s, lp["ln1_g"], lp["ln1_b"], "ln1_sample")

    tn_f = 256
    n_f = D_FF // tn_f

    def swiglu_ep(accs, tex, rex):
        return [jax.nn.silu(accs[0]) * accs[1]]

    (act, act_s), = _ws_matmul(
        "ffn_in", [(x1b, x1b_s)],
        [pair(big["ffn_w_in"], tn_f, lambda j: j), pair(big["ffn_w_in"], tn_f, lambda j: j + n_f)],
        [dict(ncols=tn_f, dtype=_BF16, col=lambda j: j, total=D_FF)],
        swiglu_ep, tm=tm, n_steps=n_f, seq=seq)

    kb = D_FF // 2
    part, part_s, scale = x1, x1_s, ALPHA
    for half in range(2):
        def acc_ep(accs, tex, rex, scale=scale):
            return [scale * tex[0] + accs[0]]

        (part, part_s), = _ws_matmul(
            "ffn_out%d" % half, [(act, act_s)],
            [pair(big["ffn_w_out"], tn, lambda j: j, kb=kb, kidx=half)],
            [dict(ncols=tn, dtype=_F32, col=lambda j: j, total=d)],
            acc_ep, tm=tm_m, n_steps=d // tn, seq=seq,
            tile_extras=[dict(p=part, s=part_s, ncols=tn, col=lambda j: j)])
        scale = 1.0
    x2, x2b = _layernorm(part, lp["ln2_g"], lp["ln2_b"], "ln2_prompt")
    x2_s, x2b_s = _layernorm(part_s, lp["ln2_g"], lp["ln2_b"], "ln2_sample")

    caches = dict(
        k_p=k_f.reshape(n_batch, seq, KV_HEADS, HEAD_DIM), v_p=v_f.reshape(n_batch, seq, KV_HEADS, HEAD_DIM),
        ik_p=ki_f.reshape(n_batch, seq, IDX_DIM), wkv_p=wkv_p,
        sh_p=hc.reshape(n_batch, seq, C_COLS)[:, -1],
        k_s=k_fs[:n_dec].reshape(n_dec, 1, KV_HEADS, HEAD_DIM), v_s=v_fs[:n_dec].reshape(n_dec, 1, KV_HEADS, HEAD_DIM),
        ik_s=ki_fs[:n_dec].reshape(n_dec, 1, IDX_DIM), wkv_s=wkv_s, sh_s=hc_s[:n_dec],
        gv_s=vn_rows.reshape(n_dec, 1, GMLP_WIDTH))
    return x2, x2b, x2_s, x2b_s, caches


def kernel(x_prompt, x_sample, cache_k, cache_v, cache_idx_k, page_table, state_wkv, state_shift, w_in, attn_wo, gmlp_ln_g, gmlp_ln_b, gmlp_ws, gmlp_bs, gmlp_wo, rwkv_mu, rwkv_w0, rwkv_w2, rwkv_a0, rwkv_a2, rwkv_kk, rwkv_ka, rwkv_rk, rwkv_gn_g, rwkv_gn_b, rwkv_wo, w_out, ln1_g, ln1_b, ffn_w_in, ffn_w_out, ln2_g, ln2_b):
    n_batch, seq, d = x_prompt.shape
    n_dec = x_sample.shape[0]
    depth = w_in.shape[0]
    assert x_sample.shape[1] == 1 and n_dec <= SAMPLE_ROWS and n_dec % SCAN_BATCH == 0
    assert n_batch % SCAN_BATCH == 0 and d == D_MODEL and seq % CHUNK == 0
    xp = x_prompt.reshape(n_batch * seq, d)
    xs = _pad_rows(x_sample.reshape(n_dec, d))
    xpb, xsb = xp.astype(_BF16), xs.astype(_BF16)
    big = dict(w_in_t=jnp.swapaxes(w_in, 1, 2).reshape(depth * w_in.shape[2], d), depth=depth, attn_wo=attn_wo, gmlp_wo=gmlp_wo, rwkv_wo=rwkv_wo, w_out=w_out,
               ffn_w_in=ffn_w_in, ffn_w_out=ffn_w_out)
    per_layer = []
    for l in range(depth):
        lp = dict(gmlp_ln_g=gmlp_ln_g[l], gmlp_ln_b=gmlp_ln_b[l], gmlp_ws=gmlp_ws[l], gmlp_bs=gmlp_bs[l],
                  rwkv_mu=rwkv_mu[l], rwkv_w0=rwkv_w0[l], rwkv_w2=rwkv_w2[l], rwkv_a0=rwkv_a0[l],
                  rwkv_a2=rwkv_a2[l], rwkv_kk=rwkv_kk[l], rwkv_ka=rwkv_ka[l], rwkv_rk=rwkv_rk[l],
                  rwkv_gn_g=rwkv_gn_g[l], rwkv_gn_b=rwkv_gn_b[l], ln1_g=ln1_g[l], ln1_b=ln1_b[l],
                  ln2_g=ln2_g[l], ln2_b=ln2_b[l])
        xp, xpb, xs, xsb, c = _layer(xp, xpb, xs, xsb, seq, n_dec, l, lp, big, cache_k, cache_v, cache_idx_k,
                                     page_table, state_shift[l], state_wkv[l])
        per_layer.append(c)

    def stack(name):
        return jnp.stack([c[name] for c in per_layer])

    return (xp.reshape(n_batch, seq, d), xs[:n_dec].reshape(n_dec, 1, d),
            stack("k_p"), stack("v_p"), stack("ik_p"), stack("wkv_p"), stack("sh_p"),
            stack("k_s"), stack("v_s"), stack("ik_s"), stack("wkv_s"), stack("sh_s"), stack("gv_s"))
```

```python
import functools

import jax
import jax.numpy as jnp
from jax import lax
from jax.experimental import pallas as pl
from jax.experimental.pallas import tpu as pltpu

D_MODEL = 4096
PAST_LEN = 16384
PAGE_SIZE = 128
N_HEADS = 16
KV_HEADS = 4
HEAD_DIM = 128
ROT_DIM = HEAD_DIM // 4
ROPE_THETA = 500000.0
IDX_HEADS = 32
IDX_DIM = 128
TOPK_MAX = 256
A_Q = N_HEADS * HEAD_DIM
A_KV = KV_HEADS * HEAD_DIM
CHUNK = 128
GMLP_WIDTH = D_MODEL // 2
GMLP_GROUPS = 8
RWKV_HEAD = 64
RWKV_WIDTH = D_MODEL // 2
RWKV_HEADS = RWKV_WIDTH // RWKV_HEAD
D_DECAY = max(32, int(round(D_MODEL ** 0.5 * 1.8 / 32)) * 32)
D_AAA = D_DECAY
D_FF = ((8 * D_MODEL // 3 + 255) // 256) * 256
A_COLS = A_Q + 2 * A_KV + IDX_HEADS * IDX_DIM + IDX_DIM + IDX_HEADS
B_COLS = 2 * GMLP_WIDTH
C_COLS = 3 * RWKV_WIDTH + D_DECAY + D_AAA
G_COLS = 3 * D_MODEL
DEPTH = 2
ALPHA = (2 * DEPTH) ** 0.25
LN_EPS = 1e-5
GN_EPS = 64e-5

LANES = 128
SAMPLE_ROWS = 16
VMEM_LIMIT = 58 * 1024 * 1024
KEY_CHUNK = 512
INT_MIN = -(2 ** 31)
NEG_BIG = -1e30
LOG2E = 1.4426950408889634

_F32 = jnp.float32
_BF16 = jnp.bfloat16
_NT = (((1,), (1,)), ((), ()))


def _params(n_grid):
    return pltpu.CompilerParams(dimension_semantics=("arbitrary",) * n_grid,
                                vmem_limit_bytes=VMEM_LIMIT)


def _ws_matmul(name, lhs, pairs, outs, epilogue, *, tm, n_steps, seq, tile_extras=(), row_extras=()):
    mp = lhs[0][0].shape[0]
    n_i = mp // tm
    tpb = seq // tm
    n_batch = mp // seq

    in_specs, args = [], []
    for a, (xp, xs) in enumerate(lhs):
        kb, kidx = [(p["kb"], p["kidx"]) for p in pairs if p["lhs"] == a][0]
        in_specs += [pl.BlockSpec((tm, kb), lambda j, i, kidx=kidx: (i, kidx)),
                     pl.BlockSpec((SAMPLE_ROWS, kb), lambda j, i, kidx=kidx: (0, kidx))]
        args += [xp, xs]
    for p in pairs:
        if p.get("row0") is not None:
            in_specs.append(pl.BlockSpec((pl.Element(p["ncols"]), pl.Element(p["kb"])),
                                         lambda j, i, p=p: (p["row0"](j), 0)))
        else:
            in_specs.append(pl.BlockSpec((None, p["kb"], p["ncols"]),
                                         lambda j, i, p=p: (p["layer"], p["kidx"], p["wcol"](j))))
        args.append(p["w"])
    for e in tile_extras:
        in_specs += [pl.BlockSpec((tm, e["ncols"]), lambda j, i, e=e: (i, e["col"](j))),
                     pl.BlockSpec((SAMPLE_ROWS, e["ncols"]), lambda j, i, e=e: (0, e["col"](j)))]
        args += [e["p"], e["s"]]
    for e in row_extras:
        in_specs += [pl.BlockSpec((tm, LANES), lambda j, i: (i % tpb, 0)),
                     pl.BlockSpec((SAMPLE_ROWS, LANES), lambda j, i: (0, 0))]
        args += [e["p"], e["s"]]

    out_specs, out_shapes = [], []
    for o in outs:
        if o.get("hm"):
            hpt = o["ncols"] // LANES
            out_shapes.append(jax.ShapeDtypeStruct((n_batch, o["total"] // LANES, seq, LANES), o["dtype"]))
            out_specs.append(pl.BlockSpec((1, hpt, tm, LANES),
                                          lambda j, i, o=o: (i // tpb, o["col"](j), i % tpb, 0)))
        else:
            out_shapes.append(jax.ShapeDtypeStruct((mp, o["total"]), o["dtype"]))
            out_specs.append(pl.BlockSpec((tm, o["ncols"]), lambda j, i, o=o: (i, o["col"](j))))
        out_shapes.append(jax.ShapeDtypeStruct((SAMPLE_ROWS, o["total"]), o["dtype"]))
        out_specs.append(pl.BlockSpec((SAMPLE_ROWS, o["ncols"]), lambda j, i, o=o: (0, o["col"](j))))

    scratch = [pltpu.VMEM((p["kb"], p["ncols"]), _BF16) for p in pairs]
    n_lhs, n_pairs, n_te, n_re, n_out = len(lhs), len(pairs), len(tile_extras), len(row_extras), len(outs)

    def kernel(*refs):
        pos = 0
        x_refs = [(refs[pos + 2 * a], refs[pos + 2 * a + 1]) for a in range(n_lhs)]
        pos += 2 * n_lhs
        w_refs = refs[pos:pos + n_pairs]
        pos += n_pairs
        te_refs = [(refs[pos + 2 * a], refs[pos + 2 * a + 1]) for a in range(n_te)]
        pos += 2 * n_te
        re_refs = [(refs[pos + 2 * a], refs[pos + 2 * a + 1]) for a in range(n_re)]
        pos += 2 * n_re
        o_refs = [(refs[pos + 2 * a], refs[pos + 2 * a + 1]) for a in range(n_out)]
        pos += 2 * n_out
        wb_refs = refs[pos:pos + n_pairs]
        i = pl.program_id(1)

        def run(which):
            accs = [jnp.dot(x_refs[p["lhs"]][which][...], wb[...], preferred_element_type=_F32)
                    for p, wb in zip(pairs, wb_refs)]
            res = epilogue(accs, [t[which][...] for t in te_refs], [r[which][...] for r in re_refs])
            for o, (op_ref, os_ref), val in zip(outs, o_refs, res):
                if which == 0 and o.get("hm"):
                    for h in range(o["ncols"] // LANES):
                        op_ref[0, h] = val[:, h * LANES:(h + 1) * LANES].astype(o["dtype"])
                else:
                    (op_ref if which == 0 else os_ref)[...] = val.astype(o["dtype"])

        @pl.when(i == 0)
        def _():
            for p, w_ref, wb in zip(pairs, w_refs, wb_refs):
                if p.get("row0") is not None:
                    for kc in range(p["kb"] // LANES):
                        blk = w_ref[:, kc * LANES:(kc + 1) * LANES]
                        wb[kc * LANES:(kc + 1) * LANES, :] = blk.T.astype(_BF16)
                    continue
                rows = 256 if p["kb"] % 256 == 0 else 128

                def cast(c, carry, w_ref=w_ref, wb=wb, rows=rows):
                    r0 = pl.multiple_of(c * rows, rows)
                    wb[pl.ds(r0, rows), :] = w_ref[pl.ds(r0, rows), :].astype(_BF16)
                    return carry

                lax.fori_loop(0, p["kb"] // rows, cast, 0)
            run(1)

        run(0)

    res = pl.pallas_call(
        kernel, grid=(n_steps, n_i), in_specs=in_specs, out_specs=out_specs, out_shape=out_shapes,
        scratch_shapes=scratch, compiler_params=_params(2), name=name)(*args)
    return [(res[2 * a], res[2 * a + 1]) for a in range(n_out)]


def _rope_tiles(x, c, s1, s2):
    outs = []
    for h in range(x.shape[1] // LANES):
        xh = x[:, h * LANES:(h + 1) * LANES]
        outs.append(xh * c + pltpu.roll(xh, ROT_DIM // 2, 1) * s1 + pltpu.roll(xh, LANES - ROT_DIM // 2, 1) * s2)
    return outs[0] if len(outs) == 1 else jnp.concatenate(outs, axis=1)


def _rope_tables(pos):
    half = ROT_DIM // 2
    inv = ROPE_THETA ** (-2.0 * jnp.arange(half, dtype=_F32) / ROT_DIM)
    ang = pos.astype(_F32)[:, None] * inv[None, :]
    cos, sin = jnp.cos(ang), jnp.sin(ang)
    n = pos.shape[0]
    c = jnp.concatenate([cos, cos, jnp.ones((n, LANES - ROT_DIM), _F32)], axis=1)
    s1 = jnp.concatenate([jnp.zeros((n, half), _F32), sin, jnp.zeros((n, LANES - ROT_DIM), _F32)], axis=1)
    s2 = jnp.concatenate([-sin, jnp.zeros((n, LANES - half), _F32)], axis=1)
    return c, s1, s2


def _ln_kernel(x_ref, g_ref, b_ref, o_ref, ob_ref):
    x = x_ref[...]
    mu = jnp.mean(x, axis=1, keepdims=True)
    xc = x - mu
    var = jnp.mean(xc * xc, axis=1, keepdims=True)
    y = xc * lax.rsqrt(var + LN_EPS) * g_ref[...] + b_ref[...]
    o_ref[...] = y
    ob_ref[...] = y.astype(_BF16)


def _layernorm(x, g, b, name):
    m, d = x.shape
    tr = 256 if m % 256 == 0 else m
    return pl.pallas_call(
        _ln_kernel, grid=(m // tr,),
        in_specs=[pl.BlockSpec((tr, d), lambda i: (i, 0)), pl.BlockSpec((1, d), lambda i: (0, 0)),
                  pl.BlockSpec((1, d), lambda i: (0, 0))],
        out_specs=[pl.BlockSpec((tr, d), lambda i: (i, 0)), pl.BlockSpec((tr, d), lambda i: (i, 0))],
        out_shape=[jax.ShapeDtypeStruct((m, d), _F32), jax.ShapeDtypeStruct((m, d), _BF16)],
        compiler_params=_params(1), name=name)(x, g.reshape(1, d), b.reshape(1, d))


def _sortable(score):
    bits = pltpu.bitcast(jnp.where(score == 0.0, 0.0, score), jnp.int32)
    return bits ^ ((bits >> 31) & jnp.int32(0x7FFFFFFF))


def _topk_bias(n_ch, load_key, meta, store_bias, k_sel, shape, axes, index_bits, neg):
    kf = float(k_sel)

    def count(pred):
        if axes == (0,):
            def cb(c, part):
                x = jnp.where(pred(c), 1.0, 0.0)
                slabs = [x[i * 8:(i + 1) * 8] for i in range(x.shape[0] // 8)]
                while len(slabs) > 1:
                    slabs = [slabs[i] + slabs[i + 1] for i in range(0, len(slabs), 2)]
                return part + slabs[0]
            part = lax.fori_loop(0, n_ch, cb, jnp.zeros((8, shape[1]), _F32))
            return jnp.sum(part, axis=0, keepdims=True)

        def cb(c, cnt):
            return cnt + jnp.sum(jnp.where(pred(c), 1.0, 0.0), axis=axes, keepdims=True)
        return lax.fori_loop(0, n_ch, cb, jnp.zeros(shape, _F32))

    def bisect(it, t):
        cand = t + lax.shift_left(jnp.int32(1), jnp.int32(31) - it)
        return jnp.where(count(lambda c: load_key(c) >= cand) >= kf, cand, t)

    t = lax.fori_loop(0, 32, bisect, jnp.full(shape, INT_MIN, jnp.int32))
    n_ge = count(lambda c: (load_key(c) >= t) & meta(c)[0])

    def tie_limit():
        need = kf - count(lambda c: load_key(c) > t)

        def body(it, m):
            cand = m + lax.shift_left(jnp.int32(1), jnp.int32(index_bits - 1) - it)
            cnt = count(lambda c: (load_key(c) == t) & meta(c)[0] & (meta(c)[1] < cand))
            return jnp.where(cnt < need, cand, m)

        return lax.fori_loop(0, index_bits, body, jnp.zeros(shape, jnp.int32))

    no_limit = jnp.full(shape, 2 ** 30, jnp.int32)
    m = lax.cond(jnp.max(n_ge) > kf, tie_limit, lambda: no_limit)

    def emit(c, carry):
        keys = load_key(c)
        valid, idx = meta(c)
        store_bias(c, jnp.where((keys > t) | ((keys == t) & valid & (idx <= m)), 0.0, neg))
        return carry

    lax.fori_loop(0, n_ch, emit, 0)


def _attn_prompt_kernel(q_ref, qi_ref, ki_ref, wi_ref, k_ref, v_ref, o_ref,
                        acc_ref, key_ref, bias_ref, wb_ref, l_ref, oacc_ref, sacc_ref, m_ref,
                        *, tq, kc, k_sel, index_bits):
    q0 = pl.program_id(1) * tq
    n_ch = (q0 + tq + kc - 1) // kc
    grp = N_HEADS // KV_HEADS
    wi = wi_ref[...] * ((IDX_HEADS * IDX_DIM) ** -0.5)
    for h in range(IDX_HEADS):
        wb_ref[h] = jnp.broadcast_to(wi[:, h:h + 1], (tq, LANES))

    def meta(c):
        col = c * kc + lax.broadcasted_iota(jnp.int32, (tq, kc), 1)
        row = q0 + lax.broadcasted_iota(jnp.int32, (tq, kc), 0)
        return col <= row, col

    def meta_t(c):
        col = c * kc + lax.broadcasted_iota(jnp.int32, (kc, tq), 0)
        row = q0 + lax.broadcasted_iota(jnp.int32, (kc, tq), 1)
        return col <= row, col

    def score_chunk(c, carry):
        ki = ki_ref[pl.ds(pl.multiple_of(c * kc, kc), kc), :]
        acc_ref[...] = jnp.zeros((tq, kc), _F32)

        def idx_body(hc, carry2):
            qc = qi_ref[0, pl.ds(hc * grp, grp)].reshape(grp * tq, IDX_DIM)
            d = lax.dot_general(qc, ki, _NT, preferred_element_type=_F32)
            part = None
            for hh in range(grp):
                w = jnp.tile(wb_ref[hc * grp + hh], (1, kc // LANES))
                term = jnp.maximum(d[hh * tq:(hh + 1) * tq], 0.0) * w
                part = term if part is None else part + term
            acc_ref[...] += part
            return carry2

        lax.fori_loop(0, IDX_HEADS // grp, idx_body, 0, unroll=True)
        keys = jnp.where(meta(c)[0], _sortable(acc_ref[...]), INT_MIN)
        key_ref[c] = pltpu.bitcast(pltpu.bitcast(keys, _F32).T, jnp.int32)
        return carry

    lax.fori_loop(0, n_ch, score_chunk, 0)

    def store_bias(c, val):
        bias_ref[c] = val.T

    _topk_bias(n_ch, lambda c: key_ref[c], meta_t, store_bias, k_sel, (1, tq), (0,), index_bits, -jnp.inf)

    m_ref[...] = jnp.full(m_ref.shape, -jnp.inf, _F32)
    oacc_ref[...] = jnp.zeros(oacc_ref.shape, _F32)
    sacc_ref[...] = jnp.zeros(sacc_ref.shape, _F32)

    def logits_chunk(c, carry):
        k0 = pl.multiple_of(c * kc, kc)
        bias = jnp.tile(bias_ref[c], (grp, 1))
        for g in range(KV_HEADS):
            qg = q_ref[0, g * grp:(g + 1) * grp].reshape(grp * tq, HEAD_DIM)
            lg = lax.dot_general(qg, k_ref[0, g, pl.ds(k0, kc), :], _NT, preferred_element_type=_F32)
            lg = lg * (HEAD_DIM ** -0.5 * LOG2E) + bias
            l_ref[g, c] = lg
            m_ref[g] = jnp.maximum(m_ref[g], jnp.max(lg, axis=1, keepdims=True))
        return carry

    lax.fori_loop(0, n_ch, logits_chunk, 0)

    ones_col = jnp.where(lax.broadcasted_iota(jnp.int32, (kc, LANES), 1) == 0, 1.0, 0.0).astype(_BF16)

    def pv_chunk(c, carry):
        k0 = pl.multiple_of(c * kc, kc)
        for g in range(KV_HEADS):
            p = jnp.exp2(l_ref[g, c] - m_ref[g]).astype(_BF16)
            v_aug = jnp.concatenate([v_ref[0, g, pl.ds(k0, kc), :], ones_col], axis=1)
            pv = jnp.dot(p, v_aug, preferred_element_type=_F32)
            oacc_ref[g] += pv[:, :HEAD_DIM]
            sacc_ref[g] += pv[:, HEAD_DIM:HEAD_DIM + 1]
        return carry

    lax.fori_loop(0, n_ch, pv_chunk, 0)
    for g in range(KV_HEADS):
        out = oacc_ref[g] / sacc_ref[g]
        for hh in range(grp):
            h = g * grp + hh
            o_ref[:, h * HEAD_DIM:(h + 1) * HEAD_DIM] = out[hh * tq:(hh + 1) * tq].astype(_BF16)


def _attn_prompt(q_hm, qi_hm, ki_b, wi, k_hm, v_hm, n_batch, seq):
    tq = 128
    kc = KEY_CHUNK if seq % KEY_CHUNK == 0 else seq
    nqb = seq // tq
    grp = N_HEADS // KV_HEADS
    k_sel = min(TOPK_MAX, seq // 4)
    kern = functools.partial(_attn_prompt_kernel, tq=tq, kc=kc, k_sel=k_sel,
                             index_bits=(seq - 1).bit_length() + 1)
    return pl.pallas_call(
        kern, grid=(n_batch, nqb),
        in_specs=[
            pl.BlockSpec((1, N_HEADS, tq, HEAD_DIM), lambda b, q: (b, 0, q, 0)),
            pl.BlockSpec((1, IDX_HEADS, tq, IDX_DIM), lambda b, q: (b, 0, q, 0)),
            pl.BlockSpec((seq, IDX_DIM), lambda b, q: (b, 0)),
            pl.BlockSpec((tq, LANES), lambda b, q: (b * nqb + q, 0)),
            pl.BlockSpec((1, KV_HEADS, seq, HEAD_DIM), lambda b, q: (b, 0, 0, 0)),
            pl.BlockSpec((1, KV_HEADS, seq, HEAD_DIM), lambda b, q: (b, 0, 0, 0)),
        ],
        out_specs=pl.BlockSpec((tq, A_Q), lambda b, q: (b * nqb + q, 0)),
        out_shape=jax.ShapeDtypeStruct((n_batch * seq, A_Q), _BF16),
        scratch_shapes=[pltpu.VMEM((tq, kc), _F32), pltpu.VMEM((seq // kc, kc, tq), jnp.int32),
                        pltpu.VMEM((seq // kc, tq, kc), _F32), pltpu.VMEM((IDX_HEADS, tq, LANES), _F32),
                        pltpu.VMEM((KV_HEADS, seq // kc, grp * tq, kc), _F32),
                        pltpu.VMEM((KV_HEADS, grp * tq, HEAD_DIM), _F32),
                        pltpu.VMEM((KV_HEADS, grp * tq, 1), _F32), pltpu.VMEM((KV_HEADS, grp * tq, 1), _F32)],
        compiler_params=_params(2), name="attn_prompt")(q_hm, qi_hm, ki_b, wi, k_hm, v_hm)


SCORE_PAGES_PER_STEP = 32
ATTN_PAGES_PER_STEP = 16


def _attn_sample_score_kernel(pt_ref, qi_ref, wi_ref, kin_ref, exp_ref, *rest, n_pages, k_sel, pps):
    pages = rest[:pps]
    bias_ref, score_ref, key_ref, sel_ref = rest[pps:]
    del pt_ref
    s = pl.program_id(1)
    n_rows = n_pages + 8
    qi = qi_ref[0]
    wcol = wi_ref[0] * ((IDX_HEADS * IDX_DIM) ** -0.5)
    for r in range(pps):
        page = pages[r][0].astype(_BF16)
        d = lax.dot_general(qi, page, _NT, preferred_element_type=_F32)
        sc = jnp.sum(jnp.maximum(d, 0.0) * wcol, axis=0, keepdims=True)
        score_ref[pl.ds(s * pps + r, 1), :] = sc

    @pl.when(s == pl.num_programs(1) - 1)
    def _():
        kin = kin_ref[0].astype(_BF16).astype(_F32)
        dn = jnp.sum(qi.astype(_F32) * kin, axis=1, keepdims=True)
        sn = jnp.sum(jnp.maximum(dn, 0.0) * wcol, axis=0, keepdims=True)
        score_ref[pl.ds(n_pages, 8), :] = jnp.broadcast_to(sn, (8, LANES))
        rowi = lax.broadcasted_iota(jnp.int32, (n_rows, LANES), 0)
        lane = lax.broadcasted_iota(jnp.int32, (n_rows, LANES), 1)
        valid = (rowi < n_pages) | ((rowi == n_pages) & (lane == 0))
        pos = rowi * LANES + lane
        key_ref[...] = jnp.where(valid, _sortable(score_ref[...]), INT_MIN)

        def store_sel(c, val):
            sel_ref[...] = val

        _topk_bias(1, lambda c: key_ref[...], lambda c: (valid, pos), store_sel, k_sel, (1, 1), (0, 1),
                   (n_rows * LANES - 1).bit_length() + 1, 1.0)
        rep = jnp.dot(sel_ref[...].astype(_BF16), exp_ref[...], preferred_element_type=_F32)
        bias_ref[0] = jnp.where(rep < 0.5, 0.0, NEG_BIG)


def _attn_sample_kernel(pt_ref, q_ref, kn_ref, vn_ref, bias_ref, *rest, n_pages, pps):
    kpages = rest[:pps]
    vpages = rest[pps:2 * pps]
    o_ref, m_ref, s_ref, acc_ref = rest[2 * pps:]
    del pt_ref
    st = pl.program_id(1)
    scale = HEAD_DIM ** -0.5
    grp = N_HEADS // KV_HEADS
    n_cols = PAGE_SIZE * KV_HEADS
    q = q_ref[0]
    own = (lax.broadcasted_iota(jnp.int32, (N_HEADS, n_cols), 1) % KV_HEADS
           == lax.broadcasted_iota(jnp.int32, (N_HEADS, n_cols), 0) // grp)
    head_bias = jnp.where(own, 0.0, NEG_BIG)

    @pl.when(st == 0)
    def _():
        m_ref[...] = jnp.full(m_ref.shape, NEG_BIG, _F32)
        s_ref[...] = jnp.zeros(s_ref.shape, _F32)
        acc_ref[...] = jnp.zeros(acc_ref.shape, _F32)

    def update(logit_list, pv_fns):
        m_old = m_ref[...]
        m_new = m_old
        for lg in logit_list:
            m_new = jnp.maximum(m_new, jnp.max(lg, axis=1, keepdims=True))
        alpha = jnp.exp(m_old - m_new)
        s_new = s_ref[...] * alpha
        acc = acc_ref[...] * alpha
        for lg, pv_fn in zip(logit_list, pv_fns):
            p = jnp.exp(lg - m_new)
            s_new = s_new + jnp.sum(p, axis=1, keepdims=True)
            acc = acc + pv_fn(p)
        s_ref[...] = s_new
        acc_ref[...] = acc
        m_ref[...] = m_new

    logit_list, pv_fns = [], []
    for r in range(pps):
        kp = kpages[r][0].astype(_BF16)
        logits = lax.dot_general(q, kp, _NT, preferred_element_type=_F32) * scale
        logit_list.append(logits + head_bias + bias_ref[0, pl.ds(st * pps + r, 1), :])
        pv_fns.append(lambda p, r=r: jnp.dot(p.astype(_BF16), vpages[r][0].astype(_BF16),
                                             preferred_element_type=_F32))
    update(logit_list, pv_fns)

    @pl.when(st == pl.num_programs(1) - 1)
    def _():
        kn = kn_ref[0].astype(_BF16).astype(_F32)
        vn = vn_ref[0].astype(_BF16).astype(_F32)
        ln = jnp.sum(q.astype(_F32) * kn, axis=1, keepdims=True) * scale
        ln = ln + bias_ref[0, pl.ds(n_pages, 1), :][:, 0:1]
        update([ln], [lambda p: p * vn])
        o_ref[0] = acc_ref[...] / s_ref[...]


def _attn_sample(q_s, qi_s, wi_s, ki_s, k_s, v_s, cache_k, cache_v, cache_ik, page_table, layer):
    n_b, n_pages = page_table.shape
    depth, n_phys = cache_ik.shape[:2]
    k_sel = min(TOPK_MAX, (n_pages * PAGE_SIZE + 1) // 4)
    pps_s = min(SCORE_PAGES_PER_STEP, n_pages)
    pps_a = min(ATTN_PAGES_PER_STEP, n_pages)
    assert n_pages % pps_s == 0 and n_pages % pps_a == 0
    n_rows = n_pages + 8
    n_cols = PAGE_SIZE * KV_HEADS
    pt = page_table.reshape(-1)
    grp = N_HEADS // KV_HEADS

    def page_map(r, pps):
        return lambda b, s, pt_ref: (layer, pt_ref[b * n_pages + s * pps + r], 0, 0)

    qi3 = qi_s[:n_b].reshape(n_b, IDX_HEADS, IDX_DIM)
    wi3 = wi_s[:n_b, :IDX_HEADS].reshape(n_b, IDX_HEADS, 1)
    kin3 = ki_s[:n_b].reshape(n_b, 1, IDX_DIM)
    expand = (jnp.arange(LANES)[:, None] == jnp.arange(n_cols)[None, :] // KV_HEADS).astype(_BF16)
    bias = pl.pallas_call(
        functools.partial(_attn_sample_score_kernel, n_pages=n_pages, k_sel=k_sel, pps=pps_s),
        grid_spec=pltpu.PrefetchScalarGridSpec(
            num_scalar_prefetch=1, grid=(n_b, n_pages // pps_s),
            in_specs=[pl.BlockSpec((1, IDX_HEADS, IDX_DIM), lambda b, s, pt_ref: (b, 0, 0)),
                      pl.BlockSpec((1, IDX_HEADS, 1), lambda b, s, pt_ref: (b, 0, 0)),
                      pl.BlockSpec((1, 1, IDX_DIM), lambda b, s, pt_ref: (b, 0, 0)),
                      pl.BlockSpec((LANES, n_cols), lambda b, s, pt_ref: (0, 0))]
            + [pl.BlockSpec((None, 1, PAGE_SIZE, IDX_DIM), page_map(r, pps_s)) for r in range(pps_s)],
            out_specs=pl.BlockSpec((1, n_rows, n_cols), lambda b, s, pt_ref: (b, 0, 0)),
            scratch_shapes=[pltpu.VMEM((n_rows, LANES), _F32), pltpu.VMEM((n_rows, LANES), jnp.int32),
                            pltpu.VMEM((n_rows, LANES), _F32)]),
        out_shape=jax.ShapeDtypeStruct((n_b, n_rows, n_cols), _F32),
        compiler_params=_params(2), name="attn_sample_score")(
            pt, qi3, wi3, kin3, expand, *([cache_ik] * pps_s))

    pk = cache_k.reshape(depth, n_phys, n_cols, HEAD_DIM)
    pv = cache_v.reshape(depth, n_phys, n_cols, HEAD_DIM)
    q3 = q_s[:n_b].reshape(n_b, N_HEADS, HEAD_DIM)
    kn = jnp.repeat(k_s[:n_b].reshape(n_b, KV_HEADS, HEAD_DIM), grp, axis=1)
    vn = jnp.repeat(v_s[:n_b].reshape(n_b, KV_HEADS, HEAD_DIM), grp, axis=1)
    head_spec = pl.BlockSpec((1, N_HEADS, HEAD_DIM), lambda b, s, pt_ref: (b, 0, 0))
    out = pl.pallas_call(
        functools.partial(_attn_sample_kernel, n_pages=n_pages, pps=pps_a),
        grid_spec=pltpu.PrefetchScalarGridSpec(
            num_scalar_prefetch=1, grid=(n_b, n_pages // pps_a),
            in_specs=[head_spec, head_spec, head_spec,
                      pl.BlockSpec((1, n_rows, n_cols), lambda b, s, pt_ref: (b, 0, 0))]
            + [pl.BlockSpec((None, 1, n_cols, HEAD_DIM), page_map(r, pps_a)) for r in range(pps_a)]
            + [pl.BlockSpec((None, 1, n_cols, HEAD_DIM), page_map(r, pps_a)) for r in range(pps_a)],
            out_specs=head_spec,
            scratch_shapes=[pltpu.VMEM((N_HEADS, 1), _F32), pltpu.VMEM((N_HEADS, 1), _F32),
                            pltpu.VMEM((N_HEADS, HEAD_DIM), _F32)]),
        out_shape=jax.ShapeDtypeStruct((n_b, N_HEADS, HEAD_DIM), _F32),
        compiler_params=_params(2), name="attn_sample")(
            pt, q3, kn, vn, bias, *([pk] * pps_a), *([pv] * pps_a))
    return out.reshape(n_b, A_Q)


def _gmlp_kernel(u_ref, v_ref, g_ref, b_ref, ws_ref, bs_ref, o_ref, vn_ref=None):
    v = v_ref[...]
    mu = jnp.mean(v, axis=1, keepdims=True)
    vc = v - mu
    var = jnp.mean(vc * vc, axis=1, keepdims=True)
    vn = vc * lax.rsqrt(var + LN_EPS) * g_ref[...] + b_ref[...]
    if vn_ref is not None:
        vn_ref[...] = vn
    gw = GMLP_WIDTH // GMLP_GROUPS
    tril = (lax.broadcasted_iota(jnp.int32, (CHUNK, CHUNK), 0)
            >= lax.broadcasted_iota(jnp.int32, (CHUNK, CHUNK), 1))
    bs = bs_ref[...]
    for g in range(GMLP_GROUPS):
        wm = jnp.where(tril, ws_ref[g], 0.0).astype(_BF16)
        mixed = jnp.dot(wm, vn[:, g * gw:(g + 1) * gw].astype(_BF16), preferred_element_type=_F32)
        mixed = mixed + bs[:, g:g + 1]
        o_ref[:, g * gw:(g + 1) * gw] = (u_ref[:, g * gw:(g + 1) * gw] * mixed).astype(_BF16)


def _gmlp(hb, ln_g, ln_b, ws, bs, want_vn, name):
    m = hb.shape[0]
    w = GMLP_WIDTH
    out_shape = [jax.ShapeDtypeStruct((m, w), _BF16)]
    out_specs = [pl.BlockSpec((CHUNK, w), lambda c: (c, 0))]
    if want_vn:
        out_shape.append(jax.ShapeDtypeStruct((m, w), _F32))
        out_specs.append(pl.BlockSpec((CHUNK, w), lambda c: (c, 0)))
    return pl.pallas_call(
        _gmlp_kernel, grid=(m // CHUNK,),
        in_specs=[pl.BlockSpec((CHUNK, w), lambda c: (c, 0)), pl.BlockSpec((CHUNK, w), lambda c: (c, 1)),
                  pl.BlockSpec((1, w), lambda c: (0, 0)), pl.BlockSpec((1, w), lambda c: (0, 0)),
                  pl.BlockSpec((GMLP_GROUPS, CHUNK, CHUNK), lambda c: (0, 0, 0)),
                  pl.BlockSpec((CHUNK, GMLP_GROUPS), lambda c: (0, 0))],
        out_specs=out_specs, out_shape=out_shape, compiler_params=_params(1), name=name)(
            hb, hb, ln_g.reshape(1, w), ln_b.reshape(1, w), ws, jnp.transpose(bs))


def _head_sums(x, blk):
    outs = []
    for c in range(x.shape[1] // LANES):
        xc = x[:, c * LANES:(c + 1) * LANES]
        hi = xc.astype(_BF16)
        lo = (xc - hi.astype(_F32)).astype(_BF16)
        outs.append(jnp.dot(hi, blk, preferred_element_type=_F32) + jnp.dot(lo, blk, preferred_element_type=_F32))
    return jnp.concatenate(outs, axis=1)


def _rwkv_pre_kernel(pc_ref, sh_ref, prev_ref, mu_ref, w0_ref, w2_ref, a0_ref, a2_ref, kkp_ref, ka_ref, blk_ref,
                     r_ref, w_ref, k_ref, v_ref, kk_ref, b_ref, *, tr, tiles_per_batch, roll_shift):
    pc = pc_ref[...]
    if roll_shift:
        first = pl.program_id(0) % tiles_per_batch == 0
        before = jnp.where(first, prev_ref[0], sh_ref[7:8, :])
        rolled = pltpu.roll(pc, 1, 0)
        rowi = lax.broadcasted_iota(jnp.int32, pc.shape, 0)
        shifted = jnp.where(rowi == 0, before, rolled)
    else:
        shifted = sh_ref[...]
    y = pc + mu_ref[...] * (shifted - pc)
    wdt = RWKV_WIDTH
    r = y[:, 0:wdt]
    k = y[:, wdt:2 * wdt]
    v = y[:, 2 * wdt:3 * wdt]
    wd = y[:, 3 * wdt:3 * wdt + D_DECAY]
    ad = y[:, 3 * wdt + D_DECAY:3 * wdt + D_DECAY + D_AAA]
    z = -(w0_ref[...] + jnp.dot(jnp.tanh(wd).astype(_BF16), w2_ref[...].astype(_BF16),
                                preferred_element_type=_F32))
    softplus = jnp.maximum(z, 0.0) + jnp.log(1.0 + jnp.exp(-jnp.abs(z)))
    decay = jnp.exp(-jnp.exp(-softplus - 0.5))
    a = jax.nn.sigmoid(a0_ref[...] + jnp.dot(ad.astype(_BF16), a2_ref[...].astype(_BF16),
                                             preferred_element_type=_F32))
    kk = k * kkp_ref[...]
    kk = kk / jnp.maximum(jnp.sqrt(_head_sums(kk * kk, blk_ref[...])), 1e-12)
    r_ref[...] = r
    w_ref[...] = decay
    k_ref[...] = k * (1.0 + (a - 1.0) * ka_ref[...])
    v_ref[...] = v
    kk_ref[...] = kk
    b_ref[...] = kk * a


def _seg_blk():
    seg = jnp.arange(LANES) // RWKV_HEAD
    return (seg[:, None] == seg[None, :]).astype(_BF16)


def _rwkv_pre(hc, shift_src, lp, seq, roll_shift, name):
    m = hc.shape[0]
    tr = 256 if (roll_shift and seq % 256 == 0) else (seq if roll_shift else m)
    tpb = seq // tr if roll_shift else 1
    wdt = RWKV_WIDTH
    if roll_shift:
        n_b = m // seq
        sh_spec = pl.BlockSpec((8, C_COLS), lambda i: (jnp.maximum(i * (tr // 8) - 1, 0), 0))
        prev = shift_src.reshape(n_b, 1, C_COLS)
        prev_spec = pl.BlockSpec((1, 1, C_COLS), lambda i: (i // tpb, 0, 0))
        sh_arg = hc
    else:
        sh_spec = pl.BlockSpec((tr, C_COLS), lambda i: (i, 0))
        prev = jnp.zeros((1, 1, C_COLS), _F32)
        prev_spec = pl.BlockSpec((1, 1, C_COLS), lambda i: (0, 0, 0))
        sh_arg = shift_src

    def vec(n):
        return pl.BlockSpec((1, n), lambda i: (0, 0))

    kern = functools.partial(_rwkv_pre_kernel, tr=tr, tiles_per_batch=tpb, roll_shift=roll_shift)
    return pl.pallas_call(
        kern, grid=(m // tr,),
        in_specs=[pl.BlockSpec((tr, C_COLS), lambda i: (i, 0)), sh_spec, prev_spec, vec(C_COLS), vec(wdt),
                  pl.BlockSpec((D_DECAY, wdt), lambda i: (0, 0)), vec(wdt),
                  pl.BlockSpec((D_AAA, wdt), lambda i: (0, 0)), vec(wdt), vec(wdt),
                  pl.BlockSpec((LANES, LANES), lambda i: (0, 0))],
        out_specs=[pl.BlockSpec((tr, wdt), lambda i: (i, 0))] * 6,
        out_shape=[jax.ShapeDtypeStruct((m, wdt), _F32)] * 6,
        compiler_params=_params(1), name=name)(
            hc, sh_arg, prev, lp["rwkv_mu"].reshape(1, C_COLS), lp["rwkv_w0"].reshape(1, wdt), lp["rwkv_w2"],
            lp["rwkv_a0"].reshape(1, wdt), lp["rwkv_a2"], lp["rwkv_kk"].reshape(1, wdt),
            lp["rwkv_ka"].reshape(1, wdt), _seg_blk())


SCAN_BATCH = LANES // RWKV_HEADS
HEAD_PAIRS = RWKV_HEADS // 2


def _rwkv_scan_kernel(*refs, tt, halves):
    cur = refs[0:6]
    nxt = refs[6:12]
    s0_ref, o_ref, st_ref = refs[12:15]
    z = (refs[15:21], refs[21:27])
    zo = refs[27]
    _scan_block(cur, nxt, s0_ref, o_ref, st_ref, z, zo, pl.program_id(0), tt, halves)


def _scan_block(cur, nxt, s0_ref, o_ref, st_ref, z, zo, step_idx, tt, halves):
    n = RWKV_HEAD

    def to_lanes(srcs, off, dsts, t):
        def lanes(src):
            x = jnp.concatenate([src[bb, off + t] for bb in range(SCAN_BATCH)], axis=0)
            x2 = jnp.concatenate([x, pltpu.roll(x, LANES // 2, 1)], axis=0)
            return x2.T[:n]
        for src, dst in zip(srcs, dsts):
            dst[t] = lanes(src)

    def from_lanes(t):
        y = zo[t]
        yt = jnp.concatenate([y, pltpu.roll(y, LANES // 2, 1)], axis=0).T[:LANES // 2]
        for bb in range(SCAN_BATCH):
            o_ref[bb, t] = yt[bb * HEAD_PAIRS:(bb + 1) * HEAD_PAIRS]

    def update(zs, t, t_out):
        zr, zw, zk, zv, zkk, zb = zs
        acc = st_ref[0] * zkk[t, pl.ds(0, 1), :]
        for j in range(1, n):
            acc = acc + st_ref[j] * zkk[t, pl.ds(j, 1), :]
        sa = -acc
        vt = zv[t]
        out = None
        for j in range(n):
            sj = st_ref[j] * zw[t, pl.ds(j, 1), :] + sa * zb[t, pl.ds(j, 1), :] + vt * zk[t, pl.ds(j, 1), :]
            st_ref[j] = sj
            term = sj * zr[t, pl.ds(j, 1), :]
            out = term if out is None else out + term
        zo[t_out] = out

    def fill_first():
        def body(t, carry):
            to_lanes(cur, 0, z[0], t)
            return carry
        lax.fori_loop(0, tt, body, 0)

    if halves == 1:
        @pl.when(step_idx == 0)
        def _():
            st_ref[...] = s0_ref[...]

        fill_first()

        def body(t, carry):
            update(z[0], t, t)
            return carry

        lax.fori_loop(0, tt, body, 0)

        def out_body(t, carry):
            from_lanes(t)
            return carry

        lax.fori_loop(0, tt, out_body, 0)
        return

    @pl.when(step_idx == 0)
    def _():
        st_ref[...] = s0_ref[...]
        zo[...] = jnp.zeros(zo.shape, _F32)
        fill_first()

    def first_half(t, carry):
        to_lanes(cur, tt, z[1], t)
        from_lanes(jnp.maximum(t - 1, 0))
        update(z[0], t, t)
        return carry

    def second_half(t, carry):
        to_lanes(nxt, 0, z[0], t)
        from_lanes(tt + t - 1)
        update(z[1], t, tt + t)
        return carry

    lax.fori_loop(0, tt, first_half, 0)
    lax.fori_loop(0, tt, second_half, 0)
    from_lanes(2 * tt - 1)


def _rwkv_scan(vecs, s0, seq, name):
    n = RWKV_HEAD
    halves, tt = (2, 8) if seq % 16 == 0 else (1, seq)
    blk = halves * tt
    n_half = seq // tt
    shape4 = (SCAN_BATCH, seq, HEAD_PAIRS, LANES)
    vspec = pl.BlockSpec((SCAN_BATCH, blk, HEAD_PAIRS, LANES), lambda s: (0, s, 0, 0))
    nspec = pl.BlockSpec((SCAN_BATCH, tt, HEAD_PAIRS, LANES),
                         lambda s: (0, jnp.minimum(halves * (s + 1), n_half - 1), 0, 0))
    sspec = pl.BlockSpec((n, n, LANES), lambda s: (0, 0, 0))
    args = [x.reshape(shape4) for x in vecs]
    o, st = pl.pallas_call(
        functools.partial(_rwkv_scan_kernel, tt=tt, halves=halves), grid=(seq // blk,),
        in_specs=[vspec] * 6 + [nspec] * 6 + [sspec], out_specs=[vspec, sspec],
        out_shape=[jax.ShapeDtypeStruct(shape4, _F32), jax.ShapeDtypeStruct((n, n, LANES), _F32)],
        scratch_shapes=[pltpu.VMEM((tt, n, LANES), _F32)] * 12 + [pltpu.VMEM((blk, n, LANES), _F32)],
        compiler_params=_params(1), name=name)(*args, *args, s0)
    return o.reshape(SCAN_BATCH * seq, RWKV_WIDTH), st


def _state_to_lanes(s):
    s5 = s.reshape(SCAN_BATCH, HEAD_PAIRS, 2, RWKV_HEAD, RWKV_HEAD)
    return jnp.transpose(s5, (4, 3, 2, 0, 1)).reshape(RWKV_HEAD, RWKV_HEAD, LANES)


def _state_from_lanes(s):
    s5 = s.reshape(RWKV_HEAD, RWKV_HEAD, 2, SCAN_BATCH, HEAD_PAIRS)
    return jnp.transpose(s5, (3, 4, 2, 1, 0)).reshape(SCAN_BATCH, RWKV_HEADS, RWKV_HEAD, RWKV_HEAD)


def _rwkv_post_kernel(o_ref, r_ref, k_ref, v_ref, g_ref, b_ref, rk_ref, blk_ref, out_ref):
    blk = blk_ref[...]
    o = o_ref[...]
    inv = 1.0 / RWKV_HEAD
    m = _head_sums(o, blk) * inv
    oc = o - m
    var = _head_sums(oc * oc, blk) * inv
    on = oc * lax.rsqrt(var + GN_EPS) * g_ref[...] + b_ref[...]
    bonus = _head_sums(r_ref[...] * k_ref[...] * rk_ref[...], blk) * v_ref[...]
    out_ref[...] = (on + bonus).astype(_BF16)


def _rwkv_post(o, r, k, v, lp, name):
    m = o.shape[0]
    tr = 256 if m % 256 == 0 else m
    wdt = RWKV_WIDTH
    row = pl.BlockSpec((tr, wdt), lambda i: (i, 0))
    vec = pl.BlockSpec((1, wdt), lambda i: (0, 0))
    return pl.pallas_call(
        _rwkv_post_kernel, grid=(m // tr,),
        in_specs=[row] * 4 + [vec] * 3 + [pl.BlockSpec((LANES, LANES), lambda i: (0, 0))],
        out_specs=row, out_shape=jax.ShapeDtypeStruct((m, wdt), _BF16),
        compiler_params=_params(1), name=name)(
            o, r, k, v, lp["rwkv_gn_g"].reshape(1, wdt), lp["rwkv_gn_b"].reshape(1, wdt),
            lp["rwkv_rk"].reshape(1, wdt), _seg_blk())


def _pad_rows(x):
    return jnp.zeros((SAMPLE_ROWS,) + x.shape[1:], x.dtype).at[:x.shape[0]].set(x)


def _layer(xp, xpb, xs, xsb, seq, n_dec, layer, lp, big, cache_k, cache_v, cache_ik, page_table,
           st_shift, st_wkv):
    mp = xp.shape[0]
    n_batch = mp // seq
    tm = 1024 if seq % 1024 == 0 else seq
    d = D_MODEL
    rope_p = _rope_tables(jnp.arange(seq, dtype=jnp.int32))
    rope_s = _rope_tables(jnp.full((SAMPLE_ROWS,), PAST_LEN, jnp.int32))
    rope_extras = [dict(p=a, s=b) for a, b in zip(rope_p, rope_s)]
    lhs_x = [(xpb, xsb)]

    def rope_ep(accs, tex, rex):
        return [_rope_tiles(accs[0], *rex)]

    def pair(w, ncols, wcol, kb=None, kidx=0, lhs=0):
        return dict(lhs=lhs, w=w, layer=layer, kb=w.shape[1] if kb is None else kb, kidx=kidx, ncols=ncols,
                    wcol=wcol)

    def pair_in(ncols, row0):
        n_in = big["w_in_t"].shape[0] // big["depth"]
        return dict(lhs=0, w=big["w_in_t"], layer=layer, kb=d, kidx=0, ncols=ncols,
                    row0=lambda j: pl.multiple_of(layer * n_in + row0(j) + 0 * j, 8))

    tn = 512
    tn_w = 1024
    tm_w = 512 if seq % 512 == 0 else seq
    (q_hm, q_s), = _ws_matmul(
        "proj_q", lhs_x, [pair_in(tn_w, lambda j: j * tn_w)],
        [dict(ncols=tn_w, dtype=_BF16, col=lambda j: j, total=A_Q, hm=True)],
        rope_ep, tm=tm_w, n_steps=A_Q // tn_w, seq=seq, row_extras=rope_extras)
    qi_start = A_Q + 2 * A_KV
    (qi_hm, qi_s), = _ws_matmul(
        "proj_qi", lhs_x, [pair_in(tn_w, lambda j: qi_start + j * tn_w)],
        [dict(ncols=tn_w, dtype=_BF16, col=lambda j: j, total=IDX_HEADS * IDX_DIM, hm=True)],
        rope_ep, tm=tm_w, n_steps=IDX_HEADS * IDX_DIM // tn_w, seq=seq, row_extras=rope_extras)

    ki_start = qi_start + IDX_HEADS * IDX_DIM

    def k_ep(accs, tex, rex):
        kr = _rope_tiles(accs[0], *rex)
        kir = _rope_tiles(accs[1], *rex)
        return [kr, kr, kir, kir]

    zero = lambda j: 0
    (k_f, k_fs), (k_hm, _), (ki_f, ki_fs), (ki_b, _) = _ws_matmul(
        "proj_k", lhs_x,
        [pair_in(A_KV, lambda j: A_Q), pair_in(IDX_DIM, lambda j: ki_start)],
        [dict(ncols=A_KV, dtype=_F32, col=zero, total=A_KV),
         dict(ncols=A_KV, dtype=_BF16, col=zero, total=A_KV, hm=True),
         dict(ncols=IDX_DIM, dtype=_F32, col=zero, total=IDX_DIM),
         dict(ncols=IDX_DIM, dtype=_BF16, col=zero, total=IDX_DIM)],
        k_ep, tm=tm, n_steps=1, seq=seq, row_extras=rope_extras)

    def v_ep(accs, tex, rex):
        return [accs[0], accs[0], accs[1]]

    (v_f, v_fs), (v_hm, _), (wi_f, wi_fs) = _ws_matmul(
        "proj_v", lhs_x,
        [pair_in(A_KV, lambda j: A_Q + A_KV), pair_in(LANES, lambda j: ki_start + IDX_DIM)],
        [dict(ncols=A_KV, dtype=_F32, col=zero, total=A_KV),
         dict(ncols=A_KV, dtype=_BF16, col=zero, total=A_KV, hm=True),
         dict(ncols=LANES, dtype=_F32, col=zero, total=LANES)],
        v_ep, tm=tm, n_steps=1, seq=seq)

    ident = lambda accs, tex, rex: [accs[0]]
    (hb, hb_s), = _ws_matmul(
        "proj_b", lhs_x, [pair_in(tn_w, lambda j: A_COLS + j * tn_w)],
        [dict(ncols=tn_w, dtype=_F32, col=lambda j: j, total=B_COLS)],
        ident, tm=tm_w, n_steps=B_COLS // tn_w, seq=seq)
    tn_c = 640
    (hc, hc_s), = _ws_matmul(
        "proj_c", lhs_x, [pair_in(tn_c, lambda j: A_COLS + B_COLS + j * tn_c)],
        [dict(ncols=tn_c, dtype=_F32, col=lambda j: j, total=C_COLS)],
        ident, tm=tm, n_steps=C_COLS // tn_c, seq=seq)
    (gate, gate_s), = _ws_matmul(
        "proj_g", lhs_x, [pair_in(tn_w, lambda j: A_COLS + B_COLS + C_COLS + j * tn_w)],
        [dict(ncols=tn_w, dtype=_BF16, col=lambda j: j, total=G_COLS)],
        lambda accs, tex, rex: [jax.nn.sigmoid(accs[0])], tm=tm_w, n_steps=G_COLS // tn_w, seq=seq)
    out_a = _attn_prompt(q_hm, qi_hm, ki_b, wi_f, k_hm, v_hm, n_batch, seq)
    out_a_s = _attn_sample(q_s, qi_s, wi_fs, ki_fs, k_fs, v_fs, cache_k, cache_v, cache_ik, page_table, layer)
    out_a_s = _pad_rows(out_a_s.astype(_BF16))

    out_b, = _gmlp(hb, lp["gmlp_ln_g"], lp["gmlp_ln_b"], lp["gmlp_ws"], lp["gmlp_bs"], False, "gmlp_prompt")
    hb_chunks = jnp.zeros((n_dec, CHUNK, B_COLS), _F32).at[:, 0].set(hb_s[:n_dec]).reshape(n_dec * CHUNK, B_COLS)
    ob_s, vn_s = _gmlp(hb_chunks, lp["gmlp_ln_g"], lp["gmlp_ln_b"], lp["gmlp_ws"], lp["gmlp_bs"], True,
                       "gmlp_sample")
    out_b_s = _pad_rows(ob_s.reshape(n_dec, CHUNK, GMLP_WIDTH)[:, 0])
    vn_rows = vn_s.reshape(n_dec, CHUNK, GMLP_WIDTH)[:, 0]

    pre_p = _rwkv_pre(hc, jnp.zeros((n_batch, C_COLS), _F32), lp, seq, True, "rwkv_pre_prompt")
    zero_state = jnp.zeros((RWKV_HEAD, RWKV_HEAD, LANES), _F32)
    o_parts, st_parts = [], []
    for b0 in range(0, n_batch, SCAN_BATCH):
        rows = slice(b0 * seq, (b0 + SCAN_BATCH) * seq)
        o_c, st_c = _rwkv_scan([x[rows] for x in pre_p], zero_state, seq, "rwkv_scan_prompt")
        o_parts.append(o_c)
        st_parts.append(_state_from_lanes(st_c))
    o_p = o_parts[0] if len(o_parts) == 1 else jnp.concatenate(o_parts, axis=0)
    wkv_p = st_parts[0] if len(st_parts) == 1 else jnp.concatenate(st_parts, axis=0)
    out_c = _rwkv_post(o_p, pre_p[0], pre_p[2], pre_p[3], lp, "rwkv_post_prompt")

    pre_s = _rwkv_pre(hc_s, _pad_rows(st_shift), lp, 1, False, "rwkv_pre_sample")
    o_parts, st_parts = [], []
    for b0 in range(0, n_dec, SCAN_BATCH):
        o_c, st_c = _rwkv_scan([x[b0:b0 + SCAN_BATCH] for x in pre_s],
                               _state_to_lanes(st_wkv[b0:b0 + SCAN_BATCH]), 1, "rwkv_scan_sample")
        o_parts.append(o_c)
        st_parts.append(_state_from_lanes(st_c))
    o_s = _pad_rows(jnp.concatenate(o_parts, axis=0))
    wkv_s = jnp.concatenate(st_parts, axis=0)
    out_c_s = _rwkv_post(o_s, pre_s[0], pre_s[2], pre_s[3], lp, "rwkv_post_sample")

    tm_m = 512 if seq % 512 == 0 else seq
    n_g = d // tn

    def merge_ep(accs, tex, rex):
        return [tex[0].astype(_F32) * accs[0] + tex[1].astype(_F32) * accs[1] + tex[2].astype(_F32) * accs[2]]

    (merged, merged_s), = _ws_matmul(
        "merge", [(out_a, out_a_s), (out_b, out_b_s), (out_c, out_c_s)],
        [pair(big["attn_wo"], tn, lambda j: j, lhs=0), pair(big["gmlp_wo"], tn, lambda j: j, lhs=1),
         pair(big["rwkv_wo"], tn, lambda j: j, lhs=2)],
        [dict(ncols=tn, dtype=_BF16, col=lambda j: j, total=d)],
        merge_ep, tm=tm_m, n_steps=d // tn, seq=seq,
        tile_extras=[dict(p=gate, s=gate_s, ncols=tn, col=lambda j, a=a: j + a * n_g) for a in range(3)])

    def resid_ep(accs, tex, rex):
        return [ALPHA * tex[0] + accs[0]]

    (pre1, pre1_s), = _ws_matmul(
        "out_proj", [(merged, merged_s)], [pair(big["w_out"], tn, lambda j: j)],
        [dict(ncols=tn, dtype=_F32, col=lambda j: j, total=d)],
        resid_ep, tm=tm, n_steps=d // tn, seq=seq,
        tile_extras=[dict(p=xp, s=xs, ncols=tn, col=lambda j: j)])
    x1, x1b = _layernorm(pre1, lp["ln1_g"], lp["ln1_b"], "ln1_prompt")
    x1_s, x1b_s = _layernorm(pre1_s, lp["ln1_g"], lp["ln1_b"], "ln1_sample")

    tn_f = 256
    n_f = D_FF // tn_f

    def swiglu_ep(accs, tex, rex):
        return [jax.nn.silu(accs[0]) * accs[1]]

    (act, act_s), = _ws_matmul(
        "ffn_in", [(x1b, x1b_s)],
        [pair(big["ffn_w_in"], tn_f, lambda j: j), pair(big["ffn_w_in"], tn_f, lambda j: j + n_f)],
        [dict(ncols=tn_f, dtype=_BF16, col=lambda j: j, total=D_FF)],
        swiglu_ep, tm=tm, n_steps=n_f, seq=seq)

    kb = D_FF // 2
    part, part_s, scale = x1, x1_s, ALPHA
    for half in range(2):
        def acc_ep(accs, tex, rex, scale=scale):
            return [scale * tex[0] + accs[0]]

        (part, part_s), = _ws_matmul(
            "ffn_out%d" % half, [(act, act_s)],
            [pair(big["ffn_w_out"], tn, lambda j: j, kb=kb, kidx=half)],
            [dict(ncols=tn, dtype=_F32, col=lambda j: j, total=d)],
            acc_ep, tm=tm_m, n_steps=d // tn, seq=seq,
            tile_extras=[dict(p=part, s=part_s, ncols=tn, col=lambda j: j)])
        scale = 1.0
    x2, x2b = _layernorm(part, lp["ln2_g"], lp["ln2_b"], "ln2_prompt")
    x2_s, x2b_s = _layernorm(part_s, lp["ln2_g"], lp["ln2_b"], "ln2_sample")

    caches = dict(
        k_p=k_f.reshape(n_batch, seq, KV_HEADS, HEAD_DIM), v_p=v_f.reshape(n_batch, seq, KV_HEADS, HEAD_DIM),
        ik_p=ki_f.reshape(n_batch, seq, IDX_DIM), wkv_p=wkv_p,
        sh_p=hc.reshape(n_batch, seq, C_COLS)[:, -1],
        k_s=k_fs[:n_dec].reshape(n_dec, 1, KV_HEADS, HEAD_DIM), v_s=v_fs[:n_dec].reshape(n_dec, 1, KV_HEADS, HEAD_DIM),
        ik_s=ki_fs[:n_dec].reshape(n_dec, 1, IDX_DIM), wkv_s=wkv_s, sh_s=hc_s[:n_dec],
        gv_s=vn_rows.reshape(n_dec, 1, GMLP_WIDTH))
    return x2, x2b, x2_s, x2b_s, caches


def kernel(x_prompt, x_sample, cache_k, cache_v, cache_idx_k, page_table, state_wkv, state_shift, w_in, attn_wo, gmlp_ln_g, gmlp_ln_b, gmlp_ws, gmlp_bs, gmlp_wo, rwkv_mu, rwkv_w0, rwkv_w2, rwkv_a0, rwkv_a2, rwkv_kk, rwkv_ka, rwkv_rk, rwkv_gn_g, rwkv_gn_b, rwkv_wo, w_out, ln1_g, ln1_b, ffn_w_in, ffn_w_out, ln2_g, ln2_b):
    n_batch, seq, d = x_prompt.shape
    n_dec = x_sample.shape[0]
    depth = w_in.shape[0]
    assert x_sample.shape[1] == 1 and n_dec <= SAMPLE_ROWS and n_dec % SCAN_BATCH == 0
    assert n_batch % SCAN_BATCH == 0 and d == D_MODEL and seq % CHUNK == 0
    xp = x_prompt.reshape(n_batch * seq, d)
    xs = _pad_rows(x_sample.reshape(n_dec, d))
    xpb, xsb = xp.astype(_BF16), xs.astype(_BF16)
    big = dict(w_in_t=jnp.swapaxes(w_in, 1, 2).reshape(depth * w_in.shape[2], d), depth=depth, attn_wo=attn_wo, gmlp_wo=gmlp_wo, rwkv_wo=rwkv_wo, w_out=w_out,
               ffn_w_in=ffn_w_in, ffn_w_out=ffn_w_out)
    per_layer = []
    for l in range(depth):
        lp = dict(gmlp_ln_g=gmlp_ln_g[l], gmlp_ln_b=gmlp_ln_b[l], gmlp_ws=gmlp_ws[l], gmlp_bs=gmlp_bs[l],
                  rwkv_mu=rwkv_mu[l], rwkv_w0=rwkv_w0[l], rwkv_w2=rwkv_w2[l], rwkv_a0=rwkv_a0[l],
                  rwkv_a2=rwkv_a2[l], rwkv_kk=rwkv_kk[l], rwkv_ka=rwkv_ka[l], rwkv_rk=rwkv_rk[l],
                  rwkv_gn_g=rwkv_gn_g[l], rwkv_gn_b=rwkv_gn_b[l], ln1_g=ln1_g[l], ln1_b=ln1_b[l],
                  ln2_g=ln2_g[l], ln2_b=ln2_b[l])
        xp, xpb, xs, xsb, c = _layer(xp, xpb, xs, xsb, seq, n_dec, l, lp, big, cache_k, cache_v, cache_idx_k,
                                     page_table, state_shift[l], state_wkv[l])
        per_layer.append(c)

    def stack(name):
        return jnp.stack([c[name] for c in per_layer])

    return (xp.reshape(n_batch, seq, d), xs[:n_dec].reshape(n_dec, 1, d),
            stack("k_p"), stack("v_p"), stack("ik_p"), stack("wkv_p"), stack("sh_p"),
            stack("k_s"), stack("v_s"), stack("ik_s"), stack("wkv_s"), stack("sh_s"), stack("gv_s"))
```

```python
import functools

import jax
import jax.numpy as jnp
from jax import lax
from jax.experimental import pallas as pl
from jax.experimental.pallas import tpu as pltpu

D_MODEL = 4096
PAST_LEN = 16384
PAGE_SIZE = 128
N_HEADS = 16
KV_HEADS = 4
HEAD_DIM = 128
ROT_DIM = HEAD_DIM // 4
ROPE_THETA = 500000.0
IDX_HEADS = 32
IDX_DIM = 128
TOPK_MAX = 256
A_Q = N_HEADS * HEAD_DIM
A_KV = KV_HEADS * HEAD_DIM
CHUNK = 128
GMLP_WIDTH = D_MODEL // 2
GMLP_GROUPS = 8
RWKV_HEAD = 64
RWKV_WIDTH = D_MODEL // 2
RWKV_HEADS = RWKV_WIDTH // RWKV_HEAD
D_DECAY = max(32, int(round(D_MODEL ** 0.5 * 1.8 / 32)) * 32)
D_AAA = D_DECAY
D_FF = ((8 * D_MODEL // 3 + 255) // 256) * 256
A_COLS = A_Q + 2 * A_KV + IDX_HEADS * IDX_DIM + IDX_DIM + IDX_HEADS
B_COLS = 2 * GMLP_WIDTH
C_COLS = 3 * RWKV_WIDTH + D_DECAY + D_AAA
G_COLS = 3 * D_MODEL
DEPTH = 2
ALPHA = (2 * DEPTH) ** 0.25
LN_EPS = 1e-5
GN_EPS = 64e-5

LANES = 128
SAMPLE_ROWS = 16
VMEM_LIMIT = 58 * 1024 * 1024
KEY_CHUNK = 512
INT_MIN = -(2 ** 31)
NEG_BIG = -1e30
LOG2E = 1.4426950408889634

_F32 = jnp.float32
_BF16 = jnp.bfloat16
_NT = (((1,), (1,)), ((), ()))


def _params(n_grid):
    return pltpu.CompilerParams(dimension_semantics=("arbitrary",) * n_grid,
                                vmem_limit_bytes=VMEM_LIMIT)


def _ws_matmul(name, lhs, pairs, outs, epilogue, *, tm, n_steps, seq, tile_extras=(), row_extras=()):
    mp = lhs[0][0].shape[0]
    n_i = mp // tm
    tpb = seq // tm
    n_batch = mp // seq

    in_specs, args = [], []
    for a, (xp, xs) in enumerate(lhs):
        kb, kidx = [(p["kb"], p["kidx"]) for p in pairs if p["lhs"] == a][0]
        in_specs += [pl.BlockSpec((tm, kb), lambda j, i, kidx=kidx: (i, kidx)),
                     pl.BlockSpec((SAMPLE_ROWS, kb), lambda j, i, kidx=kidx: (0, kidx))]
        args += [xp, xs]
    for p in pairs:
        if p.get("row0") is not None:
            in_specs.append(pl.BlockSpec((pl.Element(p["ncols"]), pl.Element(p["kb"])),
                                         lambda j, i, p=p: (p["row0"](j), 0)))
        else:
            in_specs.append(pl.BlockSpec((None, p["kb"], p["ncols"]),
                                         lambda j, i, p=p: (p["layer"], p["kidx"], p["wcol"](j))))
        args.append(p["w"])
    for e in tile_extras:
        in_specs += [pl.BlockSpec((tm, e["ncols"]), lambda j, i, e=e: (i, e["col"](j))),
                     pl.BlockSpec((SAMPLE_ROWS, e["ncols"]), lambda j, i, e=e: (0, e["col"](j)))]
        args += [e["p"], e["s"]]
    for e in row_extras:
        in_specs += [pl.BlockSpec((tm, LANES), lambda j, i: (i % tpb, 0)),
                     pl.BlockSpec((SAMPLE_ROWS, LANES), lambda j, i: (0, 0))]
        args += [e["p"], e["s"]]

    out_specs, out_shapes = [], []
    for o in outs:
        if o.get("hm"):
            hpt = o["ncols"] // LANES
            out_shapes.append(jax.ShapeDtypeStruct((n_batch, o["total"] // LANES, seq, LANES), o["dtype"]))
            out_specs.append(pl.BlockSpec((1, hpt, tm, LANES),
                                          lambda j, i, o=o: (i // tpb, o["col"](j), i % tpb, 0)))
        else:
            out_shapes.append(jax.ShapeDtypeStruct((mp, o["total"]), o["dtype"]))
            out_specs.append(pl.BlockSpec((tm, o["ncols"]), lambda j, i, o=o: (i, o["col"](j))))
        out_shapes.append(jax.ShapeDtypeStruct((SAMPLE_ROWS, o["total"]), o["dtype"]))
        out_specs.append(pl.BlockSpec((SAMPLE_ROWS, o["ncols"]), lambda j, i, o=o: (0, o["col"](j))))

    scratch = [pltpu.VMEM((p["kb"], p["ncols"]), _BF16) for p in pairs]
    n_lhs, n_pairs, n_te, n_re, n_out = len(lhs), len(pairs), len(tile_extras), len(row_extras), len(outs)

    def kernel(*refs):
        pos = 0
        x_refs = [(refs[pos + 2 * a], refs[pos + 2 * a + 1]) for a in range(n_lhs)]
        pos += 2 * n_lhs
        w_refs = refs[pos:pos + n_pairs]
        pos += n_pairs
        te_refs = [(refs[pos + 2 * a], refs[pos + 2 * a + 1]) for a in range(n_te)]
        pos += 2 * n_te
        re_refs = [(refs[pos + 2 * a], refs[pos + 2 * a + 1]) for a in range(n_re)]
        pos += 2 * n_re
        o_refs = [(refs[pos + 2 * a], refs[pos + 2 * a + 1]) for a in range(n_out)]
        pos += 2 * n_out
        wb_refs = refs[pos:pos + n_pairs]
        i = pl.program_id(1)

        def run(which):
            accs = [jnp.dot(x_refs[p["lhs"]][which][...], wb[...], preferred_element_type=_F32)
                    for p, wb in zip(pairs, wb_refs)]
            res = epilogue(accs, [t[which][...] for t in te_refs], [r[which][...] for r in re_refs])
            for o, (op_ref, os_ref), val in zip(outs, o_refs, res):
                if which == 0 and o.get("hm"):
                    for h in range(o["ncols"] // LANES):
                        op_ref[0, h] = val[:, h * LANES:(h + 1) * LANES].astype(o["dtype"])
                else:
                    (op_ref if which == 0 else os_ref)[...] = val.astype(o["dtype"])

        @pl.when(i == 0)
        def _():
            for p, w_ref, wb in zip(pairs, w_refs, wb_refs):
                if p.get("row0") is not None:
                    for kc in range(p["kb"] // LANES):
                        blk = w_ref[:, kc * LANES:(kc + 1) * LANES]
                        wb[kc * LANES:(kc + 1) * LANES, :] = blk.T.astype(_BF16)
                    continue
                rows = 256 if p["kb"] % 256 == 0 else 128

                def cast(c, carry, w_ref=w_ref, wb=wb, rows=rows):
                    r0 = pl.multiple_of(c * rows, rows)
                    wb[pl.ds(r0, rows), :] = w_ref[pl.ds(r0, rows), :].astype(_BF16)
                    return carry

                lax.fori_loop(0, p["kb"] // rows, cast, 0)
            run(1)

        run(0)

    res = pl.pallas_call(
        kernel, grid=(n_steps, n_i), in_specs=in_specs, out_specs=out_specs, out_shape=out_shapes,
        scratch_shapes=scratch, compiler_params=_params(2), name=name)(*args)
    return [(res[2 * a], res[2 * a + 1]) for a in range(n_out)]


def _rope_tiles(x, c, s1, s2):
    outs = []
    for h in range(x.shape[1] // LANES):
        xh = x[:, h * LANES:(h + 1) * LANES]
        outs.append(xh * c + pltpu.roll(xh, ROT_DIM // 2, 1) * s1 + pltpu.roll(xh, LANES - ROT_DIM // 2, 1) * s2)
    return outs[0] if len(outs) == 1 else jnp.concatenate(outs, axis=1)


def _rope_tables(pos):
    half = ROT_DIM // 2
    inv = ROPE_THETA ** (-2.0 * jnp.arange(half, dtype=_F32) / ROT_DIM)
    ang = pos.astype(_F32)[:, None] * inv[None, :]
    cos, sin = jnp.cos(ang), jnp.sin(ang)
    n = pos.shape[0]
    c = jnp.concatenate([cos, cos, jnp.ones((n, LANES - ROT_DIM), _F32)], axis=1)
    s1 = jnp.concatenate([jnp.zeros((n, half), _F32), sin, jnp.zeros((n, LANES - ROT_DIM), _F32)], axis=1)
    s2 = jnp.concatenate([-sin, jnp.zeros((n, LANES - half), _F32)], axis=1)
    return c, s1, s2


def _ln_kernel(x_ref, g_ref, b_ref, o_ref, ob_ref):
    x = x_ref[...]
    mu = jnp.mean(x, axis=1, keepdims=True)
    xc = x - mu
    var = jnp.mean(xc * xc, axis=1, keepdims=True)
    y = xc * lax.rsqrt(var + LN_EPS) * g_ref[...] + b_ref[...]
    o_ref[...] = y
    ob_ref[...] = y.astype(_BF16)


def _layernorm(x, g, b, name):
    m, d = x.shape
    tr = 256 if m % 256 == 0 else m
    return pl.pallas_call(
        _ln_kernel, grid=(m // tr,),
        in_specs=[pl.BlockSpec((tr, d), lambda i: (i, 0)), pl.BlockSpec((1, d), lambda i: (0, 0)),
                  pl.BlockSpec((1, d), lambda i: (0, 0))],
        out_specs=[pl.BlockSpec((tr, d), lambda i: (i, 0)), pl.BlockSpec((tr, d), lambda i: (i, 0))],
        out_shape=[jax.ShapeDtypeStruct((m, d), _F32), jax.ShapeDtypeStruct((m, d), _BF16)],
        compiler_params=_params(1), name=name)(x, g.reshape(1, d), b.reshape(1, d))


def _sortable(score):
    bits = pltpu.bitcast(jnp.where(score == 0.0, 0.0, score), jnp.int32)
    return bits ^ ((bits >> 31) & jnp.int32(0x7FFFFFFF))


def _topk_bias(n_ch, load_key, meta, store_bias, k_sel, shape, axes, index_bits, neg):
    kf = float(k_sel)

    def count(pred):
        if axes == (0,):
            def cb(c, part):
                x = jnp.where(pred(c), 1.0, 0.0)
                slabs = [x[i * 8:(i + 1) * 8] for i in range(x.shape[0] // 8)]
                while len(slabs) > 1:
                    slabs = [slabs[i] + slabs[i + 1] for i in range(0, len(slabs), 2)]
                return part + slabs[0]
            part = lax.fori_loop(0, n_ch, cb, jnp.zeros((8, shape[1]), _F32))
            return jnp.sum(part, axis=0, keepdims=True)

        def cb(c, cnt):
            return cnt + jnp.sum(jnp.where(pred(c), 1.0, 0.0), axis=axes, keepdims=True)
        return lax.fori_loop(0, n_ch, cb, jnp.zeros(shape, _F32))

    def bisect(it, t):
        cand = t + lax.shift_left(jnp.int32(1), jnp.int32(31) - it)
        return jnp.where(count(lambda c: load_key(c) >= cand) >= kf, cand, t)

    t = lax.fori_loop(0, 32, bisect, jnp.full(shape, INT_MIN, jnp.int32))
    n_ge = count(lambda c: (load_key(c) >= t) & meta(c)[0])

    def tie_limit():
        need = kf - count(lambda c: load_key(c) > t)

        def body(it, m):
            cand = m + lax.shift_left(jnp.int32(1), jnp.int32(index_bits - 1) - it)
            cnt = count(lambda c: (load_key(c) == t) & meta(c)[0] & (meta(c)[1] < cand))
            return jnp.where(cnt < need, cand, m)

        return lax.fori_loop(0, index_bits, body, jnp.zeros(shape, jnp.int32))

    no_limit = jnp.full(shape, 2 ** 30, jnp.int32)
    m = lax.cond(jnp.max(n_ge) > kf, tie_limit, lambda: no_limit)

    def emit(c, carry):
        keys = load_key(c)
        valid, idx = meta(c)
        store_bias(c, jnp.where((keys > t) | ((keys == t) & valid & (idx <= m)), 0.0, neg))
        return carry

    lax.fori_loop(0, n_ch, emit, 0)


def _attn_prompt_kernel(q_ref, qi_ref, ki_ref, wi_ref, k_ref, v_ref, o_ref,
                        acc_ref, key_ref, bias_ref, wb_ref, l_ref, oacc_ref, sacc_ref, m_ref,
                        *, tq, kc, k_sel, index_bits):
    q0 = pl.program_id(1) * tq
    n_ch = (q0 + tq + kc - 1) // kc
    grp = N_HEADS // KV_HEADS
    wi = wi_ref[...] * ((IDX_HEADS * IDX_DIM) ** -0.5)
    for h in range(IDX_HEADS):
        wb_ref[h] = jnp.broadcast_to(wi[:, h:h + 1], (tq, LANES))

    def meta(c):
        col = c * kc + lax.broadcasted_iota(jnp.int32, (tq, kc), 1)
        row = q0 + lax.broadcasted_iota(jnp.int32, (tq, kc), 0)
        return col <= row, col

    def meta_t(c):
        col = c * kc + lax.broadcasted_iota(jnp.int32, (kc, tq), 0)
        row = q0 + lax.broadcasted_iota(jnp.int32, (kc, tq), 1)
        return col <= row, col

    def score_chunk(c, carry):
        ki = ki_ref[pl.ds(pl.multiple_of(c * kc, kc), kc), :]
        acc_ref[...] = jnp.zeros((tq, kc), _F32)

        def idx_body(hc, carry2):
            qc = qi_ref[0, pl.ds(hc * grp, grp)].reshape(grp * tq, IDX_DIM)
            d = lax.dot_general(qc, ki, _NT, preferred_element_type=_F32)
            part = None
            for hh in range(grp):
                w = jnp.tile(wb_ref[hc * grp + hh], (1, kc // LANES))
                term = jnp.maximum(d[hh * tq:(hh + 1) * tq], 0.0) * w
                part = term if part is None else part + term
            acc_ref[...] += part
            return carry2

        lax.fori_loop(0, IDX_HEADS // grp, idx_body, 0, unroll=True)
        keys = jnp.where(meta(c)[0], _sortable(acc_ref[...]), INT_MIN)
        key_ref[c] = pltpu.bitcast(pltpu.bitcast(keys, _F32).T, jnp.int32)
        return carry

    lax.fori_loop(0, n_ch, score_chunk, 0)

    def store_bias(c, val):
        bias_ref[c] = val.T

    _topk_bias(n_ch, lambda c: key_ref[c], meta_t, store_bias, k_sel, (1, tq), (0,), index_bits, -jnp.inf)

    m_ref[...] = jnp.full(m_ref.shape, -jnp.inf, _F32)
    oacc_ref[...] = jnp.zeros(oacc_ref.shape, _F32)
    sacc_ref[...] = jnp.zeros(sacc_ref.shape, _F32)

    def logits_chunk(c, carry):
        k0 = pl.multiple_of(c * kc, kc)
        bias = jnp.tile(bias_ref[c], (grp, 1))
        for g in range(KV_HEADS):
            kg = k_ref[0, g, pl.ds(k0, kc), :]
            for hh in range(0, grp, 2):
                rows = slice(hh * tq, (hh + 2) * tq)
                qg = q_ref[0, g * grp + hh:g * grp + hh + 2].reshape(2 * tq, HEAD_DIM)
                lg = lax.dot_general(qg, kg, _NT, preferred_element_type=_F32)
                lg = lg * (HEAD_DIM ** -0.5 * LOG2E) + bias[:2 * tq]
                l_ref[g, c, rows, :] = lg
                m_ref[g, rows, :] = jnp.maximum(m_ref[g, rows, :], jnp.max(lg, axis=1, keepdims=True))
        return carry

    lax.fori_loop(0, n_ch, logits_chunk, 0)

    ones_col = jnp.where(lax.broadcasted_iota(jnp.int32, (kc, LANES), 1) == 0, 1.0, 0.0).astype(_BF16)

    def pv_chunk(c, carry):
        k0 = pl.multiple_of(c * kc, kc)
        half = grp * tq // 2
        for g in range(KV_HEADS):
            v_aug = jnp.concatenate([v_ref[0, g, pl.ds(k0, kc), :], ones_col], axis=1)
            for r0 in (0, half):
                p = jnp.exp2(l_ref[g, c, r0:r0 + half, :] - m_ref[g, r0:r0 + half, :]).astype(_BF16)
                pv = jnp.dot(p, v_aug, preferred_element_type=_F32)
                oacc_ref[g, r0:r0 + half, :] += pv[:, :HEAD_DIM]
                sacc_ref[g, r0:r0 + half, :] += pv[:, HEAD_DIM:HEAD_DIM + 1]
        return carry

    lax.fori_loop(0, n_ch, pv_chunk, 0)
    for g in range(KV_HEADS):
        out = oacc_ref[g] / sacc_ref[g]
        for hh in range(grp):
            h = g * grp + hh
            o_ref[:, h * HEAD_DIM:(h + 1) * HEAD_DIM] = out[hh * tq:(hh + 1) * tq].astype(_BF16)


def _attn_prompt(q_hm, qi_hm, ki_b, wi, k_hm, v_hm, n_batch, seq):
    tq = 128
    kc = KEY_CHUNK if seq % KEY_CHUNK == 0 else seq
    nqb = seq // tq
    grp = N_HEADS // KV_HEADS
    k_sel = min(TOPK_MAX, seq // 4)
    kern = functools.partial(_attn_prompt_kernel, tq=tq, kc=kc, k_sel=k_sel,
                             index_bits=(seq - 1).bit_length() + 1)
    return pl.pallas_call(
        kern, grid=(n_batch, nqb),
        in_specs=[
            pl.BlockSpec((1, N_HEADS, tq, HEAD_DIM), lambda b, q: (b, 0, q, 0)),
            pl.BlockSpec((1, IDX_HEADS, tq, IDX_DIM), lambda b, q: (b, 0, q, 0)),
            pl.BlockSpec((seq, IDX_DIM), lambda b, q: (b, 0)),
            pl.BlockSpec((tq, LANES), lambda b, q: (b * nqb + q, 0)),
            pl.BlockSpec((1, KV_HEADS, seq, HEAD_DIM), lambda b, q: (b, 0, 0, 0)),
            pl.BlockSpec((1, KV_HEADS, seq, HEAD_DIM), lambda b, q: (b, 0, 0, 0)),
        ],
        out_specs=pl.BlockSpec((tq, A_Q), lambda b, q: (b * nqb + q, 0)),
        out_shape=jax.ShapeDtypeStruct((n_batch * seq, A_Q), _BF16),
        scratch_shapes=[pltpu.VMEM((tq, kc), _F32), pltpu.VMEM((seq // kc, kc, tq), jnp.int32),
                        pltpu.VMEM((seq // kc, tq, kc), _F32), pltpu.VMEM((IDX_HEADS, tq, LANES), _F32),
                        pltpu.VMEM((KV_HEADS, seq // kc, grp * tq, kc), _F32),
                        pltpu.VMEM((KV_HEADS, grp * tq, HEAD_DIM), _F32),
                        pltpu.VMEM((KV_HEADS, grp * tq, 1), _F32), pltpu.VMEM((KV_HEADS, grp * tq, 1), _F32)],
        compiler_params=_params(2), name="attn_prompt")(q_hm, qi_hm, ki_b, wi, k_hm, v_hm)


SCORE_PAGES_PER_STEP = 32
ATTN_PAGES_PER_STEP = 16


def _attn_sample_score_kernel(pt_ref, qi_ref, wi_ref, kin_ref, exp_ref, *rest, n_pages, k_sel, pps):
    pages = rest[:pps]
    bias_ref, score_ref, key_ref, sel_ref = rest[pps:]
    del pt_ref
    s = pl.program_id(1)
    n_rows = n_pages + 8
    qi = qi_ref[0]
    wcol = wi_ref[0] * ((IDX_HEADS * IDX_DIM) ** -0.5)
    for r in range(pps):
        page = pages[r][0].astype(_BF16)
        d = lax.dot_general(qi, page, _NT, preferred_element_type=_F32)
        sc = jnp.sum(jnp.maximum(d, 0.0) * wcol, axis=0, keepdims=True)
        score_ref[pl.ds(s * pps + r, 1), :] = sc

    @pl.when(s == pl.num_programs(1) - 1)
    def _():
        kin = kin_ref[0].astype(_BF16).astype(_F32)
        dn = jnp.sum(qi.astype(_F32) * kin, axis=1, keepdims=True)
        sn = jnp.sum(jnp.maximum(dn, 0.0) * wcol, axis=0, keepdims=True)
        score_ref[pl.ds(n_pages, 8), :] = jnp.broadcast_to(sn, (8, LANES))
        rowi = lax.broadcasted_iota(jnp.int32, (n_rows, LANES), 0)
        lane = lax.broadcasted_iota(jnp.int32, (n_rows, LANES), 1)
        valid = (rowi < n_pages) | ((rowi == n_pages) & (lane == 0))
        pos = rowi * LANES + lane
        key_ref[...] = jnp.where(valid, _sortable(score_ref[...]), INT_MIN)

        def store_sel(c, val):
            sel_ref[...] = val

        _topk_bias(1, lambda c: key_ref[...], lambda c: (valid, pos), store_sel, k_sel, (1, 1), (0, 1),
                   (n_rows * LANES - 1).bit_length() + 1, 1.0)
        rep = jnp.dot(sel_ref[...].astype(_BF16), exp_ref[...], preferred_element_type=_F32)
        bias_ref[0] = jnp.where(rep < 0.5, 0.0, NEG_BIG)


def _attn_sample_kernel(pt_ref, q_ref, kn_ref, vn_ref, bias_ref, *rest, n_pages, pps):
    kpages = rest[:pps]
    vpages = rest[pps:2 * pps]
    o_ref, m_ref, s_ref, acc_ref = rest[2 * pps:]
    del pt_ref
    st = pl.program_id(1)
    scale = HEAD_DIM ** -0.5
    grp = N_HEADS // KV_HEADS
    n_cols = PAGE_SIZE * KV_HEADS
    q = q_ref[0]
    own = (lax.broadcasted_iota(jnp.int32, (N_HEADS, n_cols), 1) % KV_HEADS
           == lax.broadcasted_iota(jnp.int32, (N_HEADS, n_cols), 0) // grp)
    head_bias = jnp.where(own, 0.0, NEG_BIG)

    @pl.when(st == 0)
    def _():
        m_ref[...] = jnp.full(m_ref.shape, NEG_BIG, _F32)
        s_ref[...] = jnp.zeros(s_ref.shape, _F32)
        acc_ref[...] = jnp.zeros(acc_ref.shape, _F32)

    def update(logit_list, pv_fns):
        m_old = m_ref[...]
        m_new = m_old
        for lg in logit_list:
            m_new = jnp.maximum(m_new, jnp.max(lg, axis=1, keepdims=True))
        alpha = jnp.exp(m_old - m_new)
        s_new = s_ref[...] * alpha
        acc = acc_ref[...] * alpha
        for lg, pv_fn in zip(logit_list, pv_fns):
            p = jnp.exp(lg - m_new)
            s_new = s_new + jnp.sum(p, axis=1, keepdims=True)
            acc = acc + pv_fn(p)
        s_ref[...] = s_new
        acc_ref[...] = acc
        m_ref[...] = m_new

    logit_list, pv_fns = [], []
    for r in range(pps):
        kp = kpages[r][0].astype(_BF16)
        logits = lax.dot_general(q, kp, _NT, preferred_element_type=_F32) * scale
        logit_list.append(logits + head_bias + bias_ref[0, pl.ds(st * pps + r, 1), :])
        pv_fns.append(lambda p, r=r: jnp.dot(p.astype(_BF16), vpages[r][0].astype(_BF16),
                                             preferred_element_type=_F32))
    update(logit_list, pv_fns)

    @pl.when(st == pl.num_programs(1) - 1)
    def _():
        kn = kn_ref[0].astype(_BF16).astype(_F32)
        vn = vn_ref[0].astype(_BF16).astype(_F32)
        ln = jnp.sum(q.astype(_F32) * kn, axis=1, keepdims=True) * scale
        ln = ln + bias_ref[0, pl.ds(n_pages, 1), :][:, 0:1]
        update([ln], [lambda p: p * vn])
        o_ref[0] = acc_ref[...] / s_ref[...]


def _attn_sample(q_s, qi_s, wi_s, ki_s, k_s, v_s, cache_k, cache_v, cache_ik, page_table, layer):
    n_b, n_pages = page_table.shape
    depth, n_phys = cache_ik.shape[:2]
    k_sel = min(TOPK_MAX, (n_pages * PAGE_SIZE + 1) // 4)
    pps_s = min(SCORE_PAGES_PER_STEP, n_pages)
    pps_a = min(ATTN_PAGES_PER_STEP, n_pages)
    assert n_pages % pps_s == 0 and n_pages % pps_a == 0
    n_rows = n_pages + 8
    n_cols = PAGE_SIZE * KV_HEADS
    pt = page_table.reshape(-1)
    grp = N_HEADS // KV_HEADS

    def page_map(r, pps):
        return lambda b, s, pt_ref: (layer, pt_ref[b * n_pages + s * pps + r], 0, 0)

    qi3 = qi_s[:n_b].reshape(n_b, IDX_HEADS, IDX_DIM)
    wi3 = wi_s[:n_b, :IDX_HEADS].reshape(n_b, IDX_HEADS, 1)
    kin3 = ki_s[:n_b].reshape(n_b, 1, IDX_DIM)
    expand = (jnp.arange(LANES)[:, None] == jnp.arange(n_cols)[None, :] // KV_HEADS).astype(_BF16)
    bias = pl.pallas_call(
        functools.partial(_attn_sample_score_kernel, n_pages=n_pages, k_sel=k_sel, pps=pps_s),
        grid_spec=pltpu.PrefetchScalarGridSpec(
            num_scalar_prefetch=1, grid=(n_b, n_pages // pps_s),
            in_specs=[pl.BlockSpec((1, IDX_HEADS, IDX_DIM), lambda b, s, pt_ref: (b, 0, 0)),
                      pl.BlockSpec((1, IDX_HEADS, 1), lambda b, s, pt_ref: (b, 0, 0)),
                      pl.BlockSpec((1, 1, IDX_DIM), lambda b, s, pt_ref: (b, 0, 0)),
                      pl.BlockSpec((LANES, n_cols), lambda b, s, pt_ref: (0, 0))]
            + [pl.BlockSpec((None, 1, PAGE_SIZE, IDX_DIM), page_map(r, pps_s)) for r in range(pps_s)],
            out_specs=pl.BlockSpec((1, n_rows, n_cols), lambda b, s, pt_ref: (b, 0, 0)),
            scratch_shapes=[pltpu.VMEM((n_rows, LANES), _F32), pltpu.VMEM((n_rows, LANES), jnp.int32),
                            pltpu.VMEM((n_rows, LANES), _F32)]),
        out_shape=jax.ShapeDtypeStruct((n_b, n_rows, n_cols), _F32),
        compiler_params=_params(2), name="attn_sample_score")(
            pt, qi3, wi3, kin3, expand, *([cache_ik] * pps_s))

    pk = cache_k.reshape(depth, n_phys, n_cols, HEAD_DIM)
    pv = cache_v.reshape(depth, n_phys, n_cols, HEAD_DIM)
    q3 = q_s[:n_b].reshape(n_b, N_HEADS, HEAD_DIM)
    kn = jnp.repeat(k_s[:n_b].reshape(n_b, KV_HEADS, HEAD_DIM), grp, axis=1)
    vn = jnp.repeat(v_s[:n_b].reshape(n_b, KV_HEADS, HEAD_DIM), grp, axis=1)
    head_spec = pl.BlockSpec((1, N_HEADS, HEAD_DIM), lambda b, s, pt_ref: (b, 0, 0))
    out = pl.pallas_call(
        functools.partial(_attn_sample_kernel, n_pages=n_pages, pps=pps_a),
        grid_spec=pltpu.PrefetchScalarGridSpec(
            num_scalar_prefetch=1, grid=(n_b, n_pages // pps_a),
            in_specs=[head_spec, head_spec, head_spec,
                      pl.BlockSpec((1, n_rows, n_cols), lambda b, s, pt_ref: (b, 0, 0))]
            + [pl.BlockSpec((None, 1, n_cols, HEAD_DIM), page_map(r, pps_a)) for r in range(pps_a)]
            + [pl.BlockSpec((None, 1, n_cols, HEAD_DIM), page_map(r, pps_a)) for r in range(pps_a)],
            out_specs=head_spec,
            scratch_shapes=[pltpu.VMEM((N_HEADS, 1), _F32), pltpu.VMEM((N_HEADS, 1), _F32),
                            pltpu.VMEM((N_HEADS, HEAD_DIM), _F32)]),
        out_shape=jax.ShapeDtypeStruct((n_b, N_HEADS, HEAD_DIM), _F32),
        compiler_params=_params(2), name="attn_sample")(
            pt, q3, kn, vn, bias, *([pk] * pps_a), *([pv] * pps_a))
    return out.reshape(n_b, A_Q)


def _gmlp_kernel(u_ref, v_ref, g_ref, b_ref, ws_ref, bs_ref, o_ref, vn_ref=None):
    v = v_ref[...]
    mu = jnp.mean(v, axis=1, keepdims=True)
    vc = v - mu
    var = jnp.mean(vc * vc, axis=1, keepdims=True)
    vn = vc * lax.rsqrt(var + LN_EPS) * g_ref[...] + b_ref[...]
    if vn_ref is not None:
        vn_ref[...] = vn
    gw = GMLP_WIDTH // GMLP_GROUPS
    tril = (lax.broadcasted_iota(jnp.int32, (CHUNK, CHUNK), 0)
            >= lax.broadcasted_iota(jnp.int32, (CHUNK, CHUNK), 1))
    bs = bs_ref[...]
    for g in range(GMLP_GROUPS):
        wm = jnp.where(tril, ws_ref[g], 0.0).astype(_BF16)
        mixed = jnp.dot(wm, vn[:, g * gw:(g + 1) * gw].astype(_BF16), preferred_element_type=_F32)
        mixed = mixed + bs[:, g:g + 1]
        o_ref[:, g * gw:(g + 1) * gw] = (u_ref[:, g * gw:(g + 1) * gw] * mixed).astype(_BF16)


def _gmlp(hb, ln_g, ln_b, ws, bs, want_vn, name):
    m = hb.shape[0]
    w = GMLP_WIDTH
    out_shape = [jax.ShapeDtypeStruct((m, w), _BF16)]
    out_specs = [pl.BlockSpec((CHUNK, w), lambda c: (c, 0))]
    if want_vn:
        out_shape.append(jax.ShapeDtypeStruct((m, w), _F32))
        out_specs.append(pl.BlockSpec((CHUNK, w), lambda c: (c, 0)))
    return pl.pallas_call(
        _gmlp_kernel, grid=(m // CHUNK,),
        in_specs=[pl.BlockSpec((CHUNK, w), lambda c: (c, 0)), pl.BlockSpec((CHUNK, w), lambda c: (c, 1)),
                  pl.BlockSpec((1, w), lambda c: (0, 0)), pl.BlockSpec((1, w), lambda c: (0, 0)),
                  pl.BlockSpec((GMLP_GROUPS, CHUNK, CHUNK), lambda c: (0, 0, 0)),
                  pl.BlockSpec((CHUNK, GMLP_GROUPS), lambda c: (0, 0))],
        out_specs=out_specs, out_shape=out_shape, compiler_params=_params(1), name=name)(
            hb, hb, ln_g.reshape(1, w), ln_b.reshape(1, w), ws, jnp.transpose(bs))


def _head_sums(x, blk):
    outs = []
    for c in range(x.shape[1] // LANES):
        xc = x[:, c * LANES:(c + 1) * LANES]
        hi = xc.astype(_BF16)
        lo = (xc - hi.astype(_F32)).astype(_BF16)
        outs.append(jnp.dot(hi, blk, preferred_element_type=_F32) + jnp.dot(lo, blk, preferred_element_type=_F32))
    return jnp.concatenate(outs, axis=1)


def _rwkv_pre_kernel(pc_ref, sh_ref, prev_ref, mu_ref, w0_ref, w2_ref, a0_ref, a2_ref, kkp_ref, ka_ref, blk_ref,
                     r_ref, w_ref, k_ref, v_ref, kk_ref, b_ref, *, tr, tiles_per_batch, roll_shift):
    pc = pc_ref[...]
    if roll_shift:
        first = pl.program_id(0) % tiles_per_batch == 0
        before = jnp.where(first, prev_ref[0], sh_ref[7:8, :])
        rolled = pltpu.roll(pc, 1, 0)
        rowi = lax.broadcasted_iota(jnp.int32, pc.shape, 0)
        shifted = jnp.where(rowi == 0, before, rolled)
    else:
        shifted = sh_ref[...]
    y = pc + mu_ref[...] * (shifted - pc)
    wdt = RWKV_WIDTH
    r = y[:, 0:wdt]
    k = y[:, wdt:2 * wdt]
    v = y[:, 2 * wdt:3 * wdt]
    wd = y[:, 3 * wdt:3 * wdt + D_DECAY]
    ad = y[:, 3 * wdt + D_DECAY:3 * wdt + D_DECAY + D_AAA]
    z = -(w0_ref[...] + jnp.dot(jnp.tanh(wd).astype(_BF16), w2_ref[...].astype(_BF16),
                                preferred_element_type=_F32))
    softplus = jnp.maximum(z, 0.0) + jnp.log(1.0 + jnp.exp(-jnp.abs(z)))
    decay = jnp.exp(-jnp.exp(-softplus - 0.5))
    a = jax.nn.sigmoid(a0_ref[...] + jnp.dot(ad.astype(_BF16), a2_ref[...].astype(_BF16),
                                             preferred_element_type=_F32))
    kk = k * kkp_ref[...]
    kk = kk / jnp.maximum(jnp.sqrt(_head_sums(kk * kk, blk_ref[...])), 1e-12)
    r_ref[...] = r
    w_ref[...] = decay
    k_ref[...] = k * (1.0 + (a - 1.0) * ka_ref[...])
    v_ref[...] = v
    kk_ref[...] = kk
    b_ref[...] = kk * a


def _seg_blk():
    seg = jnp.arange(LANES) // RWKV_HEAD
    return (seg[:, None] == seg[None, :]).astype(_BF16)


def _rwkv_pre(hc, shift_src, lp, seq, roll_shift, name):
    m = hc.shape[0]
    tr = 256 if (roll_shift and seq % 256 == 0) else (seq if roll_shift else m)
    tpb = seq // tr if roll_shift else 1
    wdt = RWKV_WIDTH
    if roll_shift:
        n_b = m // seq
        sh_spec = pl.BlockSpec((8, C_COLS), lambda i: (jnp.maximum(i * (tr // 8) - 1, 0), 0))
        prev = shift_src.reshape(n_b, 1, C_COLS)
        prev_spec = pl.BlockSpec((1, 1, C_COLS), lambda i: (i // tpb, 0, 0))
        sh_arg = hc
    else:
        sh_spec = pl.BlockSpec((tr, C_COLS), lambda i: (i, 0))
        prev = jnp.zeros((1, 1, C_COLS), _F32)
        prev_spec = pl.BlockSpec((1, 1, C_COLS), lambda i: (0, 0, 0))
        sh_arg = shift_src

    def vec(n):
        return pl.BlockSpec((1, n), lambda i: (0, 0))

    kern = functools.partial(_rwkv_pre_kernel, tr=tr, tiles_per_batch=tpb, roll_shift=roll_shift)
    return pl.pallas_call(
        kern, grid=(m // tr,),
        in_specs=[pl.BlockSpec((tr, C_COLS), lambda i: (i, 0)), sh_spec, prev_spec, vec(C_COLS), vec(wdt),
                  pl.BlockSpec((D_DECAY, wdt), lambda i: (0, 0)), vec(wdt),
                  pl.BlockSpec((D_AAA, wdt), lambda i: (0, 0)), vec(wdt), vec(wdt),
                  pl.BlockSpec((LANES, LANES), lambda i: (0, 0))],
        out_specs=[pl.BlockSpec((tr, wdt), lambda i: (i, 0))] * 6,
        out_shape=[jax.ShapeDtypeStruct((m, wdt), _F32)] * 6,
        compiler_params=_params(1), name=name)(
            hc, sh_arg, prev, lp["rwkv_mu"].reshape(1, C_COLS), lp["rwkv_w0"].reshape(1, wdt), lp["rwkv_w2"],
            lp["rwkv_a0"].reshape(1, wdt), lp["rwkv_a2"], lp["rwkv_kk"].reshape(1, wdt),
            lp["rwkv_ka"].reshape(1, wdt), _seg_blk())


SCAN_BATCH = LANES // RWKV_HEADS
HEAD_PAIRS = RWKV_HEADS // 2


def _rwkv_scan_kernel(*refs, tt, halves):
    cur = refs[0:6]
    nxt = refs[6:12]
    s0_ref, o_ref, st_ref = refs[12:15]
    z = (refs[15:21], refs[21:27])
    zo = refs[27]
    _scan_block(cur, nxt, s0_ref, o_ref, st_ref, z, zo, pl.program_id(0), tt, halves)


def _scan_block(cur, nxt, s0_ref, o_ref, st_ref, z, zo, step_idx, tt, halves):
    n = RWKV_HEAD

    def to_lanes(srcs, off, dsts, t):
        def lanes(src):
            x = jnp.concatenate([src[bb, off + t] for bb in range(SCAN_BATCH)], axis=0)
            x2 = jnp.concatenate([x, pltpu.roll(x, LANES // 2, 1)], axis=0)
            return x2.T[:n]
        for src, dst in zip(srcs, dsts):
            dst[t] = lanes(src)

    def from_lanes(t):
        y = zo[t]
        yt = jnp.concatenate([y, pltpu.roll(y, LANES // 2, 1)], axis=0).T[:LANES // 2]
        for bb in range(SCAN_BATCH):
            o_ref[bb, t] = yt[bb * HEAD_PAIRS:(bb + 1) * HEAD_PAIRS]

    def update(zs, t, t_out):
        zr, zw, zk, zv, zkk, zb = zs
        acc = st_ref[0] * zkk[t, pl.ds(0, 1), :]
        for j in range(1, n):
            acc = acc + st_ref[j] * zkk[t, pl.ds(j, 1), :]
        sa = -acc
        vt = zv[t]
        out = None
        for j in range(n):
            sj = st_ref[j] * zw[t, pl.ds(j, 1), :] + sa * zb[t, pl.ds(j, 1), :] + vt * zk[t, pl.ds(j, 1), :]
            st_ref[j] = sj
            term = sj * zr[t, pl.ds(j, 1), :]
            out = term if out is None else out + term
        zo[t_out] = out

    def fill_first():
        def body(t, carry):
            to_lanes(cur, 0, z[0], t)
            return carry
        lax.fori_loop(0, tt, body, 0)

    if halves == 1:
        @pl.when(step_idx == 0)
        def _():
            st_ref[...] = s0_ref[...]

        fill_first()

        def body(t, carry):
            update(z[0], t, t)
            return carry

        lax.fori_loop(0, tt, body, 0)

        def out_body(t, carry):
            from_lanes(t)
            return carry

        lax.fori_loop(0, tt, out_body, 0)
        return

    @pl.when(step_idx == 0)
    def _():
        st_ref[...] = s0_ref[...]
        zo[...] = jnp.zeros(zo.shape, _F32)
        fill_first()

    def first_half(t, carry):
        to_lanes(cur, tt, z[1], t)
        from_lanes(jnp.maximum(t - 1, 0))
        update(z[0], t, t)
        return carry

    def second_half(t, carry):
        to_lanes(nxt, 0, z[0], t)
        from_lanes(tt + t - 1)
        update(z[1], t, tt + t)
        return carry

    lax.fori_loop(0, tt, first_half, 0)
    lax.fori_loop(0, tt, second_half, 0)
    from_lanes(2 * tt - 1)


def _rwkv_scan(vecs, s0, seq, name):
    n = RWKV_HEAD
    halves, tt = (2, 8) if seq % 16 == 0 else (1, seq)
    blk = halves * tt
    n_half = seq // tt
    shape4 = (SCAN_BATCH, seq, HEAD_PAIRS, LANES)
    vspec = pl.BlockSpec((SCAN_BATCH, blk, HEAD_PAIRS, LANES), lambda s: (0, s, 0, 0))
    nspec = pl.BlockSpec((SCAN_BATCH, tt, HEAD_PAIRS, LANES),
                         lambda s: (0, jnp.minimum(halves * (s + 1), n_half - 1), 0, 0))
    sspec = pl.BlockSpec((n, n, LANES), lambda s: (0, 0, 0))
    args = [x.reshape(shape4) for x in vecs]
    o, st = pl.pallas_call(
        functools.partial(_rwkv_scan_kernel, tt=tt, halves=halves), grid=(seq // blk,),
        in_specs=[vspec] * 6 + [nspec] * 6 + [sspec], out_specs=[vspec, sspec],
        out_shape=[jax.ShapeDtypeStruct(shape4, _F32), jax.ShapeDtypeStruct((n, n, LANES), _F32)],
        scratch_shapes=[pltpu.VMEM((tt, n, LANES), _F32)] * 12 + [pltpu.VMEM((blk, n, LANES), _F32)],
        compiler_params=_params(1), name=name)(*args, *args, s0)
    return o.reshape(SCAN_BATCH * seq, RWKV_WIDTH), st


def _state_to_lanes(s):
    s5 = s.reshape(SCAN_BATCH, HEAD_PAIRS, 2, RWKV_HEAD, RWKV_HEAD)
    return jnp.transpose(s5, (4, 3, 2, 0, 1)).reshape(RWKV_HEAD, RWKV_HEAD, LANES)


def _state_from_lanes(s):
    s5 = s.reshape(RWKV_HEAD, RWKV_HEAD, 2, SCAN_BATCH, HEAD_PAIRS)
    return jnp.transpose(s5, (3, 4, 2, 1, 0)).reshape(SCAN_BATCH, RWKV_HEADS, RWKV_HEAD, RWKV_HEAD)


def _rwkv_post_kernel(o_ref, r_ref, k_ref, v_ref, g_ref, b_ref, rk_ref, blk_ref, out_ref):
    blk = blk_ref[...]
    o = o_ref[...]
    inv = 1.0 / RWKV_HEAD
    m = _head_sums(o, blk) * inv
    oc = o - m
    var = _head_sums(oc * oc, blk) * inv
    on = oc * lax.rsqrt(var + GN_EPS) * g_ref[...] + b_ref[...]
    bonus = _head_sums(r_ref[...] * k_ref[...] * rk_ref[...], blk) * v_ref[...]
    out_ref[...] = (on + bonus).astype(_BF16)


def _rwkv_post(o, r, k, v, lp, name):
    m = o.shape[0]
    tr = 256 if m % 256 == 0 else m
    wdt = RWKV_WIDTH
    row = pl.BlockSpec((tr, wdt), lambda i: (i, 0))
    vec = pl.BlockSpec((1, wdt), lambda i: (0, 0))
    return pl.pallas_call(
        _rwkv_post_kernel, grid=(m // tr,),
        in_specs=[row] * 4 + [vec] * 3 + [pl.BlockSpec((LANES, LANES), lambda i: (0, 0))],
        out_specs=row, out_shape=jax.ShapeDtypeStruct((m, wdt), _BF16),
        compiler_params=_params(1), name=name)(
            o, r, k, v, lp["rwkv_gn_g"].reshape(1, wdt), lp["rwkv_gn_b"].reshape(1, wdt),
            lp["rwkv_rk"].reshape(1, wdt), _seg_blk())


def _pad_rows(x):
    return jnp.zeros((SAMPLE_ROWS,) + x.shape[1:], x.dtype).at[:x.shape[0]].set(x)


def _layer(xp, xpb, xs, xsb, seq, n_dec, layer, lp, big, cache_k, cache_v, cache_ik, page_table,
           st_shift, st_wkv):
    mp = xp.shape[0]
    n_batch = mp // seq
    tm = 1024 if seq % 1024 == 0 else seq
    d = D_MODEL
    rope_p = _rope_tables(jnp.arange(seq, dtype=jnp.int32))
    rope_s = _rope_tables(jnp.full((SAMPLE_ROWS,), PAST_LEN, jnp.int32))
    rope_extras = [dict(p=a, s=b) for a, b in zip(rope_p, rope_s)]
    lhs_x = [(xpb, xsb)]

    def rope_ep(accs, tex, rex):
        return [_rope_tiles(accs[0], *rex)]

    def pair(w, ncols, wcol, kb=None, kidx=0, lhs=0):
        return dict(lhs=lhs, w=w, layer=layer, kb=w.shape[1] if kb is None else kb, kidx=kidx, ncols=ncols,
                    wcol=wcol)

    def pair_in(ncols, row0):
        n_in = big["w_in_t"].shape[0] // big["depth"]
        return dict(lhs=0, w=big["w_in_t"], layer=layer, kb=d, kidx=0, ncols=ncols,
                    row0=lambda j: pl.multiple_of(layer * n_in + row0(j) + 0 * j, 8))

    tn = 512
    tn_w = 1024
    tm_w = 512 if seq % 512 == 0 else seq
    (q_hm, q_s), = _ws_matmul(
        "proj_q", lhs_x, [pair_in(tn_w, lambda j: j * tn_w)],
        [dict(ncols=tn_w, dtype=_BF16, col=lambda j: j, total=A_Q, hm=True)],
        rope_ep, tm=tm_w, n_steps=A_Q // tn_w, seq=seq, row_extras=rope_extras)
    qi_start = A_Q + 2 * A_KV
    (qi_hm, qi_s), = _ws_matmul(
        "proj_qi", lhs_x, [pair_in(tn_w, lambda j: qi_start + j * tn_w)],
        [dict(ncols=tn_w, dtype=_BF16, col=lambda j: j, total=IDX_HEADS * IDX_DIM, hm=True)],
        rope_ep, tm=tm_w, n_steps=IDX_HEADS * IDX_DIM // tn_w, seq=seq, row_extras=rope_extras)

    ki_start = qi_start + IDX_HEADS * IDX_DIM

    def k_ep(accs, tex, rex):
        kr = _rope_tiles(accs[0], *rex)
        kir = _rope_tiles(accs[1], *rex)
        return [kr, kr, kir, kir]

    zero = lambda j: 0
    (k_f, k_fs), (k_hm, _), (ki_f, ki_fs), (ki_b, _) = _ws_matmul(
        "proj_k", lhs_x,
        [pair_in(A_KV, lambda j: A_Q), pair_in(IDX_DIM, lambda j: ki_start)],
        [dict(ncols=A_KV, dtype=_F32, col=zero, total=A_KV),
         dict(ncols=A_KV, dtype=_BF16, col=zero, total=A_KV, hm=True),
         dict(ncols=IDX_DIM, dtype=_F32, col=zero, total=IDX_DIM),
         dict(ncols=IDX_DIM, dtype=_BF16, col=zero, total=IDX_DIM)],
        k_ep, tm=tm, n_steps=1, seq=seq, row_extras=rope_extras)

    def v_ep(accs, tex, rex):
        return [accs[0], accs[0], accs[1]]

    (v_f, v_fs), (v_hm, _), (wi_f, wi_fs) = _ws_matmul(
        "proj_v", lhs_x,
        [pair_in(A_KV, lambda j: A_Q + A_KV), pair_in(LANES, lambda j: ki_start + IDX_DIM)],
        [dict(ncols=A_KV, dtype=_F32, col=zero, total=A_KV),
         dict(ncols=A_KV, dtype=_BF16, col=zero, total=A_KV, hm=True),
         dict(ncols=LANES, dtype=_F32, col=zero, total=LANES)],
        v_ep, tm=tm, n_steps=1, seq=seq)

    ident = lambda accs, tex, rex: [accs[0]]
    (hb, hb_s), = _ws_matmul(
        "proj_b", lhs_x, [pair_in(tn_w, lambda j: A_COLS + j * tn_w)],
        [dict(ncols=tn_w, dtype=_F32, col=lambda j: j, total=B_COLS)],
        ident, tm=tm_w, n_steps=B_COLS // tn_w, seq=seq)
    tn_c = 640
    (hc, hc_s), = _ws_matmul(
        "proj_c", lhs_x, [pair_in(tn_c, lambda j: A_COLS + B_COLS + j * tn_c)],
        [dict(ncols=tn_c, dtype=_F32, col=lambda j: j, total=C_COLS)],
        ident, tm=tm, n_steps=C_COLS // tn_c, seq=seq)
    (gate, gate_s), = _ws_matmul(
        "proj_g", lhs_x, [pair_in(tn_w, lambda j: A_COLS + B_COLS + C_COLS + j * tn_w)],
        [dict(ncols=tn_w, dtype=_BF16, col=lambda j: j, total=G_COLS)],
        lambda accs, tex, rex: [jax.nn.sigmoid(accs[0])], tm=tm_w, n_steps=G_COLS // tn_w, seq=seq)
    out_a = _attn_prompt(q_hm, qi_hm, ki_b, wi_f, k_hm, v_hm, n_batch, seq)
    out_a_s = _attn_sample(q_s, qi_s, wi_fs, ki_fs, k_fs, v_fs, cache_k, cache_v, cache_ik, page_table, layer)
    out_a_s = _pad_rows(out_a_s.astype(_BF16))

    out_b, = _gmlp(hb, lp["gmlp_ln_g"], lp["gmlp_ln_b"], lp["gmlp_ws"], lp["gmlp_bs"], False, "gmlp_prompt")
    hb_chunks = jnp.zeros((n_dec, CHUNK, B_COLS), _F32).at[:, 0].set(hb_s[:n_dec]).reshape(n_dec * CHUNK, B_COLS)
    ob_s, vn_s = _gmlp(hb_chunks, lp["gmlp_ln_g"], lp["gmlp_ln_b"], lp["gmlp_ws"], lp["gmlp_bs"], True,
                       "gmlp_sample")
    out_b_s = _pad_rows(ob_s.reshape(n_dec, CHUNK, GMLP_WIDTH)[:, 0])
    vn_rows = vn_s.reshape(n_dec, CHUNK, GMLP_WIDTH)[:, 0]

    pre_p = _rwkv_pre(hc, jnp.zeros((n_batch, C_COLS), _F32), lp, seq, True, "rwkv_pre_prompt")
    zero_state = jnp.zeros((RWKV_HEAD, RWKV_HEAD, LANES), _F32)
    o_parts, st_parts = [], []
    for b0 in range(0, n_batch, SCAN_BATCH):
        rows = slice(b0 * seq, (b0 + SCAN_BATCH) * seq)
        o_c, st_c = _rwkv_scan([x[rows] for x in pre_p], zero_state, seq, "rwkv_scan_prompt")
        o_parts.append(o_c)
        st_parts.append(_state_from_lanes(st_c))
    o_p = o_parts[0] if len(o_parts) == 1 else jnp.concatenate(o_parts, axis=0)
    wkv_p = st_parts[0] if len(st_parts) == 1 else jnp.concatenate(st_parts, axis=0)
    out_c = _rwkv_post(o_p, pre_p[0], pre_p[2], pre_p[3], lp, "rwkv_post_prompt")

    pre_s = _rwkv_pre(hc_s, _pad_rows(st_shift), lp, 1, False, "rwkv_pre_sample")
    o_parts, st_parts = [], []
    for b0 in range(0, n_dec, SCAN_BATCH):
        o_c, st_c = _rwkv_scan([x[b0:b0 + SCAN_BATCH] for x in pre_s],
                               _state_to_lanes(st_wkv[b0:b0 + SCAN_BATCH]), 1, "rwkv_scan_sample")
        o_parts.append(o_c)
        st_parts.append(_state_from_lanes(st_c))
    o_s = _pad_rows(jnp.concatenate(o_parts, axis=0))
    wkv_s = jnp.concatenate(st_parts, axis=0)
    out_c_s = _rwkv_post(o_s, pre_s[0], pre_s[2], pre_s[3], lp, "rwkv_post_sample")

    tm_m = 512 if seq % 512 == 0 else seq
    n_g = d // tn

    def merge_ep(accs, tex, rex):
        return [tex[0].astype(_F32) * accs[0] + tex[1].astype(_F32) * accs[1] + tex[2].astype(_F32) * accs[2]]

    (merged, merged_s), = _ws_matmul(
        "merge", [(out_a, out_a_s), (out_b, out_b_s), (out_c, out_c_s)],
        [pair(big["attn_wo"], tn, lambda j: j, lhs=0), pair(big["gmlp_wo"], tn, lambda j: j, lhs=1),
         pair(big["rwkv_wo"], tn, lambda j: j, lhs=2)],
        [dict(ncols=tn, dtype=_BF16, col=lambda j: j, total=d)],
        merge_ep, tm=tm_m, n_steps=d // tn, seq=seq,
        tile_extras=[dict(p=gate, s=gate_s, ncols=tn, col=lambda j, a=a: j + a * n_g) for a in range(3)])

    def resid_ep(accs, tex, rex):
        return [ALPHA * tex[0] + accs[0]]

    (pre1, pre1_s), = _ws_matmul(
        "out_proj", [(merged, merged_s)], [pair(big["w_out"], tn, lambda j: j)],
        [dict(ncols=tn, dtype=_F32, col=lambda j: j, total=d)],
        resid_ep, tm=tm, n_steps=d // tn, seq=seq,
        tile_extras=[dict(p=xp, s=xs, ncols=tn, col=lambda j: j)])
    x1, x1b = _layernorm(pre1, lp["ln1_g"], lp["ln1_b"], "ln1_prompt")
    x1_s, x1b_s = _layernorm(pre1_s, lp["ln1_g"], lp["ln1_b"], "ln1_sample")

    tn_f = 256
    n_f = D_FF // tn_f

    def swiglu_ep(accs, tex, rex):
        return [jax.nn.silu(accs[0]) * accs[1]]

    (act, act_s), = _ws_matmul(
        "ffn_in", [(x1b, x1b_s)],
        [pair(big["ffn_w_in"], tn_f, lambda j: j), pair(big["ffn_w_in"], tn_f, lambda j: j + n_f)],
        [dict(ncols=tn_f, dtype=_BF16, col=lambda j: j, total=D_FF)],
        swiglu_ep, tm=tm, n_steps=n_f, seq=seq)

    kb = D_FF // 2
    part, part_s, scale = x1, x1_s, ALPHA
    for half in range(2):
        def acc_ep(accs, tex, rex, scale=scale):
            return [scale * tex[0] + accs[0]]

        (part, part_s), = _ws_matmul(
            "ffn_out%d" % half, [(act, act_s)],
            [pair(big["ffn_w_out"], tn, lambda j: j, kb=kb, kidx=half)],
            [dict(ncols=tn, dtype=_F32, col=lambda j: j, total=d)],
            acc_ep, tm=tm_m, n_steps=d // tn, seq=seq,
            tile_extras=[dict(p=part, s=part_s, ncols=tn, col=lambda j: j)])
        scale = 1.0
    x2, x2b = _layernorm(part, lp["ln2_g"], lp["ln2_b"], "ln2_prompt")
    x2_s, x2b_s = _layernorm(part_s, lp["ln2_g"], lp["ln2_b"], "ln2_sample")

    caches = dict(
        k_p=k_f.reshape(n_batch, seq, KV_HEADS, HEAD_DIM), v_p=v_f.reshape(n_batch, seq, KV_HEADS, HEAD_DIM),
        ik_p=ki_f.reshape(n_batch, seq, IDX_DIM), wkv_p=wkv_p,
        sh_p=hc.reshape(n_batch, seq, C_COLS)[:, -1],
        k_s=k_fs[:n_dec].reshape(n_dec, 1, KV_HEADS, HEAD_DIM), v_s=v_fs[:n_dec].reshape(n_dec, 1, KV_HEADS, HEAD_DIM),
        ik_s=ki_fs[:n_dec].reshape(n_dec, 1, IDX_DIM), wkv_s=wkv_s, sh_s=hc_s[:n_dec],
        gv_s=vn_rows.reshape(n_dec, 1, GMLP_WIDTH))
    return x2, x2b, x2_s, x2b_s, caches


def kernel(x_prompt, x_sample, cache_k, cache_v, cache_idx_k, page_table, state_wkv, state_shift, w_in, attn_wo, gmlp_ln_g, gmlp_ln_b, gmlp_ws, gmlp_bs, gmlp_wo, rwkv_mu, rwkv_w0, rwkv_w2, rwkv_a0, rwkv_a2, rwkv_kk, rwkv_ka, rwkv_rk, rwkv_gn_g, rwkv_gn_b, rwkv_wo, w_out, ln1_g, ln1_b, ffn_w_in, ffn_w_out, ln2_g, ln2_b):
    n_batch, seq, d = x_prompt.shape
    n_dec = x_sample.shape[0]
    depth = w_in.shape[0]
    assert x_sample.shape[1] == 1 and n_dec <= SAMPLE_ROWS and n_dec % SCAN_BATCH == 0
    assert n_batch % SCAN_BATCH == 0 and d == D_MODEL and seq % CHUNK == 0
    xp = x_prompt.reshape(n_batch * seq, d)
    xs = _pad_rows(x_sample.reshape(n_dec, d))
    xpb, xsb = xp.astype(_BF16), xs.astype(_BF16)
    big = dict(w_in_t=jnp.swapaxes(w_in, 1, 2).reshape(depth * w_in.shape[2], d), depth=depth, attn_wo=attn_wo, gmlp_wo=gmlp_wo, rwkv_wo=rwkv_wo, w_out=w_out,
               ffn_w_in=ffn_w_in, ffn_w_out=ffn_w_out)
    per_layer = []
    for l in range(depth):
        lp = dict(gmlp_ln_g=gmlp_ln_g[l], gmlp_ln_b=gmlp_ln_b[l], gmlp_ws=gmlp_ws[l], gmlp_bs=gmlp_bs[l],
                  rwkv_mu=rwkv_mu[l], rwkv_w0=rwkv_w0[l], rwkv_w2=rwkv_w2[l], rwkv_a0=rwkv_a0[l],
                  rwkv_a2=rwkv_a2[l], rwkv_kk=rwkv_kk[l], rwkv_ka=rwkv_ka[l], rwkv_rk=rwkv_rk[l],
                  rwkv_gn_g=rwkv_gn_g[l], rwkv_gn_b=rwkv_gn_b[l], ln1_g=ln1_g[l], ln1_b=ln1_b[l],
                  ln2_g=ln2_g[l], ln2_b=ln2_b[l])
        xp, xpb, xs, xsb, c = _layer(xp, xpb, xs, xsb, seq, n_dec, l, lp, big, cache_k, cache_v, cache_idx_k,
                                     page_table, state_shift[l], state_wkv[l])
        per_layer.append(c)

    def stack(name):
        return jnp.stack([c[name] for c in per_layer])

    return (xp.reshape(n_batch, seq, d), xs[:n_dec].reshape(n_dec, 1, d),
            stack("k_p"), stack("v_p"), stack("ik_p"), stack("wkv_p"), stack("sh_p"),
            stack("k_s"), stack("v_s"), stack("ik_s"), stack("wkv_s"), stack("sh_s"), stack("gv_s"))
```
